```python
import jax, jax.numpy as jnp
from jax import lax
import numpy as np

D_MODEL = 1024
BATCH = 8
SEQ = 2048
DEPTH = 1
DEC_BATCH = 128
DEC_SEQ = 8
PAST_LEN = 2048
PAGE_SIZE = 128

HG_HEADS = 4
HG_DK = 128
HG_DV = 128
HG_KW = HG_HEADS * HG_DK
HG_WIDTH = HG_HEADS * HG_DV
HG_CHUNK = 32
FOX_HEADS = 8
FOX_DH = 64
FOX_WIDTH = FOX_HEADS * FOX_DH
Q_BLOCK = 128
D_FF = 4 * D_MODEL
N_BRANCH = 2
N_MOD = 6
RMS_EPS = 1e-6
SPLIT_SIZES = (HG_KW, HG_KW, HG_WIDTH, HG_WIDTH, FOX_WIDTH, FOX_WIDTH, FOX_WIDTH, FOX_HEADS, N_BRANCH * D_MODEL)
IN_COLS = sum(SPLIT_SIZES)
SPLIT_IDX = [int(i) for i in np.cumsum(SPLIT_SIZES)[:-1]]

kernel_name = "hybrid_hgrn2_fox_adaln_step"


def rmsnorm(x, w):
    x32 = x.astype(jnp.float32)
    r = x32 * lax.rsqrt(jnp.mean(x32 * x32, axis=-1, keepdims=True) + RMS_EPS)
    return (r * w.astype(jnp.float32)).astype(x.dtype)


def hgrn_chunked(q, k, v, logf, s0):
    B, L, H, DK = q.shape
    DV = v.shape[-1]
    C = HG_CHUNK if L % HG_CHUNK == 0 else L
    n = L // C

    def to_chunks(a):
        return jnp.moveaxis(a.astype(jnp.float32).reshape(B, n, C, H, a.shape[-1]), 1, 0)

    causal = jnp.tril(jnp.ones((C, C), dtype=bool))

    def step(S, inp):
        qc, kc, vc, gc = inp
        b = jnp.cumsum(gc, axis=1)
        qt = qc * jnp.exp(b)
        kt = kc * jnp.exp(-b)
        a = jnp.where(causal, jnp.einsum('bthk,bshk->bhts', qt, kt), 0.0)
        o = jnp.einsum('bchk,bhkv->bchv', qt, S) + jnp.einsum('bhts,bshv->bthv', a, vc)
        bl = b[:, -1:]
        S = jnp.exp(bl[:, 0])[..., None] * S + jnp.einsum('bshk,bshv->bhkv', kc * jnp.exp(bl - b), vc)
        return S, o

    S, o = lax.scan(step, s0.astype(jnp.float32), (to_chunks(q), to_chunks(k), to_chunks(v), to_chunks(logf)))
    o = jnp.moveaxis(o, 0, 1).reshape(B, L, H, DV)
    return o.astype(v.dtype), S.astype(s0.dtype)


def fox_attend(q, k, v, dq, dk, q_pos, k_pos):
    B, Lq, H, Dh = q.shape
    bq = Q_BLOCK if Lq % Q_BLOCK == 0 else Lq
    nb = Lq // bq
    scale = Dh ** -0.5
    qs = jnp.moveaxis(q.reshape(B, nb, bq, H, Dh), 1, 0)
    dqs = jnp.moveaxis(dq.reshape(B, nb, bq, H), 1, 0)
    qps = q_pos.reshape(nb, bq)
    dk_t = jnp.transpose(dk, (0, 2, 1))[:, :, None, :]

    def blk(args):
        qb, dqb, qpb = args
        s = jnp.einsum('bqhd,bkhd->bhqk', qb, k, preferred_element_type=jnp.float32) * scale
        s = s + jnp.transpose(dqb, (0, 2, 1))[..., None] - dk_t
        mask = k_pos[None, :] <= qpb[:, None]
        s = jnp.where(mask, s, -jnp.inf)
        p = jax.nn.softmax(s, axis=-1)
        return jnp.einsum('bhqk,bkhd->bqhd', p.astype(v.dtype), v)

    o = lax.map(blk, (qs, dqs, qps))
    return jnp.moveaxis(o, 0, 1).reshape(B, Lq, H, Dh)


def token_mixer(h, lw, lb, s0, past):
    B, L, _ = h.shape
    hq, hf, hi, hg, fq, fk, fv, ff, mg = jnp.split(h @ lw['w_in'], SPLIT_IDX, axis=-1)
    hf32 = hf.astype(jnp.float32)
    f = lb + (1.0 - lb) * jax.nn.sigmoid(hf32)
    g_log = jnp.log(f)
    k_in = (1.0 - lb) * jax.nn.sigmoid(-hf32)
    q_h = hq.reshape(B, L, HG_HEADS, HG_DK) * (HG_DK ** -0.5)
    o_h, s_new = hgrn_chunked(q_h, k_in.reshape(B, L, HG_HEADS, HG_DK),
                              hi.reshape(B, L, HG_HEADS, HG_DV),
                              g_log.reshape(B, L, HG_HEADS, HG_DK), s0)
    o_h = rmsnorm(o_h, lw['hg_onorm']) * jax.nn.silu(hg.reshape(B, L, HG_HEADS, HG_DV))
    br_h = o_h.reshape(B, L, HG_WIDTH) @ lw['w_br_h']
    q_f = fq.reshape(B, L, FOX_HEADS, FOX_DH)
    k_f = fk.reshape(B, L, FOX_HEADS, FOX_DH)
    v_f = fv.reshape(B, L, FOX_HEADS, FOX_DH)
    logf = jax.nn.log_sigmoid(ff.astype(jnp.float32) + lw['fox_b_f'].astype(jnp.float32))
    if past is None:
        kk, vv, lf_all = k_f, v_f, logf
    else:
        pk, pv, plf = past
        kk = jnp.concatenate([pk, k_f], axis=1)
        vv = jnp.concatenate([pv, v_f], axis=1)
        lf_all = jnp.concatenate([plf.astype(jnp.float32), logf], axis=1)
    Lk = kk.shape[1]
    D = jnp.cumsum(lf_all, axis=1)
    k_pos = jnp.arange(Lk, dtype=jnp.int32)
    q_pos = k_pos[Lk - L:]
    o_f = fox_attend(q_f, kk, vv, D[:, Lk - L:], D, q_pos, k_pos)
    br_f = o_f.reshape(B, L, FOX_WIDTH) @ lw['w_br_f']
    g_h, g_f = jnp.split(jax.nn.sigmoid(mg), N_BRANCH, axis=-1)
    y = (g_h * br_h + g_f * br_f) @ lw['w_out']
    return y, (k_f, v_f, logf.astype(h.dtype), s_new)


def layer(x, c, lw, lb, s0, past):
    mod = (jax.nn.silu(c) @ lw['ada_w'] + lw['ada_b'])[:, None, :]
    sh1, sc1, gt1, sh2, sc2, gt2 = jnp.split(mod, N_MOD, axis=-1)
    h = rmsnorm(x, lw['n1_pre']) * (1.0 + sc1) + sh1
    y, new_state = token_mixer(h, lw, lb, s0, past)
    x = x + gt1 * rmsnorm(y, lw['n1_post'])
    h = rmsnorm(x, lw['n2_pre']) * (1.0 + sc2) + sh2
    u = jnp.square(jax.nn.relu(h @ lw['w_up'])) @ lw['w_down']
    x = x + gt2 * rmsnorm(u, lw['n2_post'])
    return x, new_state


def setup_inputs(seed: int = 0) -> dict:
    key = jax.random.key(seed)
    ks = jax.random.split(key, 32)
    f32 = jnp.float32

    def nrm(k, shape, scale):
        return jax.random.normal(k, shape, f32) * scale

    n_pages = PAST_LEN // PAGE_SIZE
    n_used = DEC_BATCH * n_pages
    n_phys = n_used + max(1, n_used // 4)
    page_table = jax.random.permutation(ks[0], n_phys)[:n_used].reshape(DEC_BATCH, n_pages).astype(jnp.int32)
    return {
        'x_prompt': nrm(ks[1], (BATCH, SEQ, D_MODEL), 1.0),
        'x_sample': nrm(ks[2], (DEC_BATCH, DEC_SEQ, D_MODEL), 1.0),
        'c_prompt': nrm(ks[3], (BATCH, D_MODEL), 1.0),
        'c_sample': nrm(ks[4], (DEC_BATCH, D_MODEL), 1.0),
        'cache_k': nrm(ks[5], (DEPTH, n_phys, PAGE_SIZE, FOX_HEADS, FOX_DH), 1.0),
        'cache_v': nrm(ks[6], (DEPTH, n_phys, PAGE_SIZE, FOX_HEADS, FOX_DH), 1.0),
        'cache_logf': jax.nn.log_sigmoid(2.0 + nrm(ks[7], (DEPTH, n_phys, PAGE_SIZE, FOX_HEADS), 1.0)),
        'state_hgrn': nrm(ks[8], (DEPTH, DEC_BATCH, HG_HEADS, HG_DK, HG_DV), 0.5),
        'page_table': page_table,
        'ada_w': nrm(ks[9], (DEPTH, D_MODEL, N_MOD * D_MODEL), 0.5 * D_MODEL ** -0.5),
        'ada_b': nrm(ks[10], (DEPTH, N_MOD * D_MODEL), 0.02),
        'norm_mix_pre': 1.0 + nrm(ks[11], (DEPTH, D_MODEL), 0.02),
        'norm_mix_post': 1.0 + nrm(ks[12], (DEPTH, D_MODEL), 0.02),
        'norm_mlp_pre': 1.0 + nrm(ks[13], (DEPTH, D_MODEL), 0.02),
        'norm_mlp_post': 1.0 + nrm(ks[14], (DEPTH, D_MODEL), 0.02),
        'w_in': nrm(ks[15], (DEPTH, D_MODEL, IN_COLS), D_MODEL ** -0.5),
        'hgrn_lower_bounds': nrm(ks[16], (DEPTH + 1, HG_KW), 0.1),
        'hgrn_onorm': 1.0 + nrm(ks[17], (DEPTH, HG_DV), 0.02),
        'fox_b_f': 2.0 + nrm(ks[18], (DEPTH, FOX_HEADS), 0.1),
        'w_br_h': nrm(ks[19], (DEPTH, HG_WIDTH, D_MODEL), HG_WIDTH ** -0.5),
        'w_br_f': nrm(ks[20], (DEPTH, FOX_WIDTH, D_MODEL), FOX_WIDTH ** -0.5),
        'w_out': nrm(ks[21], (DEPTH, D_MODEL, D_MODEL), D_MODEL ** -0.5),
        'w_mlp_up': nrm(ks[22], (DEPTH, D_MODEL, D_FF), D_MODEL ** -0.5),
        'w_mlp_down': nrm(ks[23], (DEPTH, D_FF, D_MODEL), D_FF ** -0.5),
    }


def reference(x_prompt, x_sample, c_prompt, c_sample, cache_k, cache_v, cache_logf, state_hgrn, page_table,
              ada_w, ada_b, norm_mix_pre, norm_mix_post, norm_mlp_pre, norm_mlp_post, w_in,
              hgrn_lower_bounds, hgrn_onorm, fox_b_f, w_br_h, w_br_f, w_out, w_mlp_up, w_mlp_down):
    lb_all = jnp.cumsum(jax.nn.softmax(hgrn_lower_bounds.astype(jnp.float32), axis=0), axis=0)
    n_dec, n_pages = page_table.shape
    yp, ys = x_prompt, x_sample
    kp_l, vp_l, lfp_l, sp_l, ks_l, vs_l, lfs_l, ss_l = [], [], [], [], [], [], [], []
    for l in range(DEPTH):
        lw = {'ada_w': ada_w[l], 'ada_b': ada_b[l], 'n1_pre': norm_mix_pre[l], 'n1_post': norm_mix_post[l],
              'n2_pre': norm_mlp_pre[l], 'n2_post': norm_mlp_post[l], 'w_in': w_in[l],
              'hg_onorm': hgrn_onorm[l], 'fox_b_f': fox_b_f[l], 'w_br_h': w_br_h[l], 'w_br_f': w_br_f[l],
              'w_out': w_out[l], 'w_up': w_mlp_up[l], 'w_down': w_mlp_down[l]}
        lb = lb_all[l]
        s0 = jnp.zeros((x_prompt.shape[0], HG_HEADS, HG_DK, HG_DV), x_prompt.dtype)
        yp, (kp, vp, lfp, sp) = layer(yp, c_prompt, lw, lb, s0, None)
        past_k = cache_k[l][page_table].reshape(n_dec, n_pages * PAGE_SIZE, FOX_HEADS, FOX_DH)
        past_v = cache_v[l][page_table].reshape(n_dec, n_pages * PAGE_SIZE, FOX_HEADS, FOX_DH)
        past_lf = cache_logf[l][page_table].reshape(n_dec, n_pages * PAGE_SIZE, FOX_HEADS)
        ys, (kss, vss, lfs, ss) = layer(ys, c_sample, lw, lb, state_hgrn[l], (past_k, past_v, past_lf))
        kp_l.append(kp); vp_l.append(vp); lfp_l.append(lfp); sp_l.append(sp)
        ks_l.append(kss); vs_l.append(vss); lfs_l.append(lfs); ss_l.append(ss)
    k_prompt = jnp.stack(kp_l)
    v_prompt = jnp.stack(vp_l)
    logf_prompt = jnp.stack(lfp_l)
    s_prompt = jnp.stack(sp_l)
    k_sample = jnp.stack(ks_l)
    v_sample = jnp.stack(vs_l)
    logf_sample = jnp.stack(lfs_l)
    s_sample = jnp.stack(ss_l)
    return (yp, ys, k_prompt, v_prompt, logf_prompt, s_prompt, k_sample, v_sample, logf_sample, s_sample)
```

```python
import functools

import jax
import jax.numpy as jnp
from jax import lax
from jax.experimental import pallas as pl
from jax.experimental.pallas import tpu as pltpu

F32 = jnp.float32
BF16 = jnp.bfloat16

LANES = 128
SUBLANES = 8
VMEM_LIMIT_BYTES = 56 * 1024 * 1024

D_MODEL = 1024
HG_HEADS = 4
HG_DK = 128
HG_DV = 128
HG_CHUNK = 32
FOX_HEADS = 8
FOX_DH = 64
FOX_WIDTH = FOX_HEADS * FOX_DH
HG_WIDTH = HG_HEADS * HG_DV
PAGE_SIZE = 128
N_MOD = 6
RMS_EPS = 1e-6
NEG_INF = float("-inf")

_NT = (((1,), (1,)), ((), ()))
_TN = (((0,), (0,)), ((), ()))


def _params(*sem):
    return pltpu.CompilerParams(dimension_semantics=sem, vmem_limit_bytes=VMEM_LIMIT_BYTES)


def _resident(shape):
    nd = len(shape)
    return pl.BlockSpec(shape, lambda *_: (0,) * nd, pipeline_mode=pl.Buffered(1))


def _sigmoid_pair(x):
    t = jnp.exp(-jnp.abs(x))
    r = 1.0 / (1.0 + t)
    tr = t * r
    pos = x >= 0
    return jnp.where(pos, r, tr), jnp.where(pos, tr, r)


def _split3(x):
    hi = x.astype(BF16)
    r1 = x - hi.astype(F32)
    mid = r1.astype(BF16)
    lo = (r1 - mid.astype(F32)).astype(BF16)
    return hi, mid, lo


def _dot_f32_lhs(x, w):
    hi, mid, lo = _split3(x)
    d = lambda p: jnp.dot(p, w, preferred_element_type=F32)
    return (d(lo) + d(mid)) + d(hi)


def _cumsum_rows(x, period):
    row = lax.broadcasted_iota(jnp.int32, x.shape, 0) & (period - 1)
    s = 1
    while s < period:
        x = x + jnp.where(row >= s, pltpu.roll(x, s, axis=0), 0.0)
        s *= 2
    return x


def _rms(x, w):
    return x * lax.rsqrt(jnp.mean(x * x, axis=-1, keepdims=True) + RMS_EPS) * w


def _ada_kernel(c_ref, w_ref, b_ref, o_ref):
    c = c_ref[...]
    s, _ = _sigmoid_pair(c)
    a = (c * s).astype(BF16)
    o_ref[...] = jnp.dot(a, w_ref[...].astype(BF16), preferred_element_type=F32) + b_ref[...]


def _ada(c, w, b, tn=1536):
    m, d = c.shape
    n = w.shape[1]
    return pl.pallas_call(
        _ada_kernel,
        grid=(n // tn,),
        in_specs=[pl.BlockSpec((m, d), lambda j: (0, 0)),
                  pl.BlockSpec((d, tn), lambda j: (0, j)),
                  pl.BlockSpec((1, tn), lambda j: (0, j))],
        out_specs=pl.BlockSpec((m, tn), lambda j: (0, j)),
        out_shape=jax.ShapeDtypeStruct((m, n), F32),
        compiler_params=_params("arbitrary"),
        name="ada",
    )(c, w, b)


def _inproj_kernel(x_ref, mod_ref, n1_ref, lbraw_ref, bf_ref, wa_ref, wff_ref, wmg_ref,
                   q_ref, g_ref, k_ref, v_ref, gate_ref, fq_ref, fk_ref, fv_ref,
                   lf8_ref, lf128_ref, mg_ref, *, kv_transposed):
    g_, r_, d = x_ref.shape
    m = g_ * r_
    x = x_ref[...]
    mod = mod_ref[...]
    sh1 = mod[:, :, 0:d]
    sc1 = mod[:, :, d:2 * d]
    h = _rms(x, n1_ref[...]) * (1.0 + sc1) + sh1
    hb = h.reshape(m, d).astype(BF16)

    raw = lbraw_ref[...]
    e = jnp.exp(raw - jnp.max(raw, axis=0, keepdims=True))
    lb = e[0:1, :] / jnp.sum(e, axis=0, keepdims=True)

    w = HG_WIDTH

    def proj(c):
        return lax.dot_general(hb, wa_ref[c * w:(c + 1) * w, :], _NT, preferred_element_type=F32)

    def proj_t(c):
        return lax.dot_general(wa_ref[c * w:(c + 1) * w, :], hb, _NT, preferred_element_type=F32)

    q_ref[...] = proj(0) * (HG_DK ** -0.5)
    s_pos, s_neg = _sigmoid_pair(proj(1))
    g_ref[...] = jnp.log(lb + (1.0 - lb) * s_pos)
    k_ref[...] = (1.0 - lb) * s_neg
    v_ref[...] = proj(2).astype(BF16)
    hg = proj(3)
    gate_ref[...] = (hg * _sigmoid_pair(hg)[0]).astype(BF16)
    fq_ref[...] = (proj(4) * (FOX_DH ** -0.5)).astype(BF16)
    if kv_transposed:
        fk_ref[0] = proj_t(5)
        fv_ref[0] = proj_t(6)
    else:
        fk_ref[...] = proj(5)
        fv_ref[...] = proj(6)
    z = lax.dot_general(hb, wff_ref[...], _NT, preferred_element_type=F32) + bf_ref[...]
    lf = jnp.minimum(z, 0.0) - jnp.log1p(jnp.exp(-jnp.abs(z)))
    lf128_ref[...] = lf
    lf8_ref[...] = lf[:, 0:FOX_HEADS]
    for c in range(wmg_ref.shape[0] // w):
        mgc = lax.dot_general(hb, wmg_ref[c * w:(c + 1) * w, :], _NT, preferred_element_type=F32)
        mg_ref[:, c * w:(c + 1) * w] = _sigmoid_pair(mgc)[0].astype(BF16)


def _inproj(x3, mod3, n1, lbraw, bf128, wa, wff, wmg, groups, rows, kv_transposed):
    nb, r_all, d = x3.shape
    nt = r_all // rows
    steps = (nb // groups) * nt
    t = nb * r_all
    m = groups * rows
    xmap = lambda s: (s // nt, s % nt, 0)
    mmap = lambda s: (s // nt, 0, 0)
    omap = lambda s: (s, 0)

    def out(width, dtype):
        return jax.ShapeDtypeStruct((t, width), dtype), pl.BlockSpec((m, width), omap)

    if kv_transposed:
        kv = (jax.ShapeDtypeStruct((nb, FOX_WIDTH, r_all), F32),
              pl.BlockSpec((1, FOX_WIDTH, m), lambda s: (s // nt, 0, s % nt)))
    else:
        kv = out(FOX_WIDTH, F32)
    outs = [out(HG_WIDTH, F32), out(HG_WIDTH, F32), out(HG_WIDTH, F32), out(HG_WIDTH, BF16),
            out(HG_WIDTH, BF16), out(FOX_WIDTH, BF16), kv, kv,
            out(FOX_HEADS, F32), out(LANES, F32), out(wmg.shape[0], BF16)]
    return pl.pallas_call(
        functools.partial(_inproj_kernel, kv_transposed=kv_transposed),
        grid=(steps,),
        in_specs=[pl.BlockSpec((groups, rows, d), xmap),
                  pl.BlockSpec((groups, 1, mod3.shape[2]), mmap),
                  _resident(n1.shape), _resident(lbraw.shape), _resident(bf128.shape),
                  _resident(wa.shape), _resident(wff.shape), _resident(wmg.shape)],
        out_specs=[o[1] for o in outs],
        out_shape=[o[0] for o in outs],
        compiler_params=_params("arbitrary"),
        name="inproj",
    )(x3, mod3, n1, lbraw, bf128, wa, wff, wmg)


def _decay_kernel(lf_ref, e_ref, dexp_ref, drow_ref):
    seq = lf_ref.shape[1]
    carry = jnp.zeros((1, LANES), F32)
    for j in range(seq // LANES):
        sl = slice(j * LANES, (j + 1) * LANES)
        cs = _cumsum_rows(lf_ref[0, sl, :], LANES) + carry
        carry = cs[LANES - 1:LANES, :]
        dexp_ref[0, sl, :] = _dot_f32_lhs(cs, e_ref[...])
        drow_ref[0, :, sl] = cs.T[0:FOX_HEADS, :]


def _decay(lf3, expand):
    b, seq, _ = lf3.shape
    return pl.pallas_call(
        _decay_kernel,
        grid=(b,),
        in_specs=[pl.BlockSpec((1, seq, LANES), lambda i: (i, 0, 0)), _resident(expand.shape)],
        out_specs=[pl.BlockSpec((1, seq, FOX_WIDTH), lambda i: (i, 0, 0)),
                   pl.BlockSpec((1, FOX_HEADS, seq), lambda i: (i, 0, 0))],
        out_shape=[jax.ShapeDtypeStruct((b, seq, FOX_WIDTH), F32),
                   jax.ShapeDtypeStruct((b, FOX_HEADS, seq), F32)],
        compiler_params=_params("arbitrary"),
        name="decay",
    )(lf3, expand)


def _hgrn_block(q, k, g, v, chunk, state_in, state_out):
    rows = q.shape[0]
    n_chunks = rows // chunk
    b = _cumsum_rows(g, chunk)
    qt = (q * jnp.exp(b)).astype(BF16)
    kt = (k * jnp.exp(-b)).astype(BF16)
    a = lax.dot_general(qt, kt, _NT, preferred_element_type=F32)
    ri = lax.broadcasted_iota(jnp.int32, (rows, rows), 0)
    ci = lax.broadcasted_iota(jnp.int32, (rows, rows), 1)
    shift = chunk.bit_length() - 1
    keep = (ri - ci).astype(jnp.uint32) <= (ri & (chunk - 1)).astype(jnp.uint32)
    a = jnp.where(keep, a, 0.0).astype(BF16)
    o_intra = jnp.dot(a, v, preferred_element_type=F32)

    b3 = b.reshape(n_chunks, chunk, HG_DK)
    bl3 = b3[:, chunk - 1:chunk, :]
    kd = (k.reshape(n_chunks, chunk, HG_DK) * jnp.exp(bl3 - b3)).reshape(rows, HG_DK).astype(BF16)
    vrow = lax.broadcasted_iota(jnp.int32, v.shape, 0) >> shift
    o_parts = []
    st = None
    for c in range(n_chunks):
        st = state_in(c, st)
        o_parts.append(lax.dot_general(qt[c * chunk:(c + 1) * chunk, :], st.astype(BF16), _NT,
                                       preferred_element_type=F32))
        vz = jnp.where(vrow == c, v, jnp.zeros_like(v))
        st = st * jnp.exp(bl3[c]) + lax.dot_general(vz, kd, _TN, preferred_element_type=F32)
        state_out(c, st)
    return o_intra + jnp.concatenate(o_parts, axis=0)


def _hgrn_prompt_kernel(q_ref, k_ref, g_ref, v_ref, o_ref, s_ref, st_ref, *, chunk):
    seq = q_ref.shape[0]
    st_ref[...] = jnp.zeros_like(st_ref)

    def body(j, carry):
        rs = pl.ds(pl.multiple_of(j * LANES, LANES), LANES)

        def state_in(c, prev):
            return st_ref[...] if prev is None else prev

        def state_out(c, st):
            if c == LANES // chunk - 1:
                st_ref[...] = st

        o_ref[rs, :] = _hgrn_block(q_ref[rs, :], k_ref[rs, :], g_ref[rs, :], v_ref[rs, :],
                                   chunk, state_in, state_out)
        return carry

    lax.fori_loop(0, seq // LANES, body, 0)
    s_ref[0, 0] = st_ref[...].T


def _hgrn_prompt(q, k, g, v, batch, seq, chunk):
    spec = pl.BlockSpec((seq, HG_DK), lambda b, h: (b, h))
    return pl.pallas_call(
        functools.partial(_hgrn_prompt_kernel, chunk=chunk),
        grid=(batch, HG_HEADS),
        in_specs=[spec, spec, spec, spec],
        out_specs=[spec, pl.BlockSpec((1, 1, HG_DK, HG_DV), lambda b, h: (b, h, 0, 0))],
        out_shape=[jax.ShapeDtypeStruct((batch * seq, HG_WIDTH), F32),
                   jax.ShapeDtypeStruct((batch, HG_HEADS, HG_DK, HG_DV), F32)],
        scratch_shapes=[pltpu.VMEM((HG_DV, HG_DK), F32)],
        compiler_params=_params("arbitrary", "arbitrary"),
        name="hgrn_prompt",
    )(q, k, g, v)


def _hgrn_sample_kernel(q_ref, k_ref, g_ref, v_ref, s0_ref, o_ref, s_ref, *, chunk):
    def state_in(c, prev):
        return s0_ref[c, 0].T

    def state_out(c, st):
        s_ref[c, 0] = st.T

    o_ref[...] = _hgrn_block(q_ref[...], k_ref[...], g_ref[...], v_ref[...], chunk, state_in, state_out)


def _hgrn_sample(q, k, g, v, s0, chunk):
    t = q.shape[0]
    per = LANES // chunk
    spec = pl.BlockSpec((LANES, HG_DK), lambda i, h: (i, h))
    sspec = pl.BlockSpec((per, 1, HG_DK, HG_DV), lambda i, h: (i, h, 0, 0))
    return pl.pallas_call(
        functools.partial(_hgrn_sample_kernel, chunk=chunk),
        grid=(t // LANES, HG_HEADS),
        in_specs=[spec, spec, spec, spec, sspec],
        out_specs=[spec, sspec],
        out_shape=[jax.ShapeDtypeStruct((t, HG_WIDTH), F32),
                   jax.ShapeDtypeStruct(s0.shape, F32)],
        compiler_params=_params("arbitrary", "arbitrary"),
        name="hgrn_sample",
    )(q, k, g, v, s0)


def _fox_prompt_kernel(q_ref, k_ref, v_ref, dq_ref, dk_ref, o_ref, kb_ref, vb_ref):
    tq = q_ref.shape[0]
    i = pl.program_id(2)

    @pl.when(i == 0)
    def _():
        for jj in range(kb_ref.shape[0]):
            kb_ref[jj] = k_ref[:, jj * tq:(jj + 1) * tq].astype(BF16)
            vb_ref[jj] = v_ref[:, jj * tq:(jj + 1) * tq].astype(BF16)

    q = q_ref[...]
    dq = dq_ref[0]
    lane = lax.broadcasted_iota(jnp.int32, (tq, LANES), 1)
    ri = lax.broadcasted_iota(jnp.int32, (tq, tq), 0)
    ci = lax.broadcasted_iota(jnp.int32, (tq, tq), 1)
    outs = []
    for e in range(LANES // FOX_DH):
        qe = jnp.where(lane // FOX_DH == e, q, jnp.zeros_like(q))
        dqe = dq[:, e * FOX_DH:e * FOX_DH + 1]

        def tile(j, carry, diagonal):
            m, l, acc = carry
            s = jnp.dot(qe, kb_ref[j], preferred_element_type=F32)
            s = s + (dqe - dk_ref[j, e:e + 1, :])
            if diagonal:
                s = jnp.where(ci <= ri, s, NEG_INF)
            m_new = jnp.maximum(m, jnp.max(s, axis=-1, keepdims=True))
            p = jnp.exp(s - m_new)
            alpha = jnp.exp(m - m_new)
            l = alpha * l + jnp.sum(p, axis=-1, keepdims=True)
            acc = alpha * acc + lax.dot_general(p.astype(BF16), vb_ref[j], _NT, preferred_element_type=F32)
            return m_new, l, acc

        init = (jnp.full((tq, 1), NEG_INF, F32), jnp.zeros((tq, 1), F32), jnp.zeros((tq, LANES), F32))
        carry = lax.fori_loop(0, i, lambda j, c: tile(j, c, False), init)
        _, l, acc = tile(i, carry, True)
        outs.append(acc / l)
    o_ref[...] = jnp.where(lane < FOX_DH, outs[0], outs[1]).astype(BF16)


def _fox_prompt(fq, fk, fv, dexp, dk5, batch, seq, tq):
    nq = seq // tq
    pairs = FOX_WIDTH // LANES
    return pl.pallas_call(
        _fox_prompt_kernel,
        grid=(batch, pairs, nq),
        in_specs=[pl.BlockSpec((tq, LANES), lambda b, p, i: (b * nq + i, p)),
                  pl.BlockSpec((None, LANES, seq), lambda b, p, i: (b, p, 0)),
                  pl.BlockSpec((None, LANES, seq), lambda b, p, i: (b, p, 0)),
                  pl.BlockSpec((1, tq, LANES), lambda b, p, i: (b, i, p)),
                  pl.BlockSpec((None, None, nq, LANES // FOX_DH, tq), lambda b, p, i: (b, p, 0, 0, 0))],
        out_specs=pl.BlockSpec((tq, LANES), lambda b, p, i: (b * nq + i, p)),
        out_shape=jax.ShapeDtypeStruct((batch * seq, FOX_WIDTH), BF16),
        scratch_shapes=[pltpu.VMEM((nq, LANES, tq), BF16), pltpu.VMEM((nq, LANES, tq), BF16)],
        compiler_params=_params("arbitrary", "arbitrary", "arbitrary"),
        name="fox_prompt",
    )(fq, fk, fv, dexp, dk5)


def _fox_sample_kernel(pt_ref, q_ref, kn_ref, vn_ref, lfn_ref, sfx_ref, *refs, n_pages):
    del pt_ref
    k_refs = refs[:n_pages]
    v_refs = refs[n_pages:2 * n_pages]
    lf_refs = refs[2 * n_pages:3 * n_pages]
    o_ref = refs[3 * n_pages]
    steps = q_ref.shape[1]
    rows = FOX_HEADS * steps
    pad = PAGE_SIZE - steps

    def per_head_rows(x8):
        return jnp.broadcast_to(x8[:, None, :], (FOX_HEADS, steps, x8.shape[1])).reshape(rows, x8.shape[1])

    q = q_ref[0].astype(F32)
    q_rows = jnp.concatenate([q] * FOX_HEADS, axis=0)
    row_head = lax.broadcasted_iota(jnp.int32, (rows, FOX_WIDTH), 0) // steps
    lane_head = lax.broadcasted_iota(jnp.int32, (rows, FOX_WIDTH), 1) // FOX_DH
    head_mask = row_head == lane_head
    qbd = jnp.where(head_mask, q_rows, 0.0).astype(BF16)

    pn = _cumsum_rows(lfn_ref[0], steps)
    pn_rows = jnp.concatenate([pn] * FOX_HEADS, axis=0)
    r_h = lax.broadcasted_iota(jnp.int32, (rows, LANES), 0) // steps
    r_i = lax.broadcasted_iota(jnp.int32, (rows, LANES), 0) % steps
    c_l = lax.broadcasted_iota(jnp.int32, (rows, LANES), 1)
    pcol = jnp.sum(jnp.where(c_l == r_h, pn_rows, 0.0), axis=1, keepdims=True)
    pn_t = jnp.concatenate([pn, jnp.zeros((pad, LANES), F32)], axis=0).T[0:FOX_HEADS, :]

    lf_all = jnp.concatenate([lf_refs[j][0] for j in range(n_pages)], axis=0)
    sfx = _dot_f32_lhs(lf_all, sfx_ref[...])
    carry = jnp.zeros((FOX_HEADS, PAGE_SIZE), F32)
    page_bias = [None] * n_pages
    for j in reversed(range(n_pages)):
        blk = sfx[j * FOX_HEADS:(j + 1) * FOX_HEADS, :]
        page_bias[j] = blk[:, 0:PAGE_SIZE] + carry
        carry = carry + blk[:, PAGE_SIZE:2 * PAGE_SIZE]

    s_tiles = []
    for j in range(n_pages):
        s = jnp.dot(qbd, k_refs[j][0].astype(BF16), preferred_element_type=F32)
        s_tiles.append(s + (per_head_rows(page_bias[j]) + pcol))
    kn = jnp.concatenate([kn_ref[0], jnp.zeros((pad, FOX_WIDTH), F32)], axis=0).astype(BF16)
    s_new = lax.dot_general(qbd, kn, _NT, preferred_element_type=F32) + (pcol - per_head_rows(pn_t))
    s_tiles.append(jnp.where(c_l <= r_i, s_new, NEG_INF))

    m_el = s_tiles[0]
    for s in s_tiles[1:]:
        m_el = jnp.maximum(m_el, s)
    m = jnp.max(m_el, axis=1, keepdims=True)
    p_tiles = [jnp.exp(s - m) for s in s_tiles]
    l_el = p_tiles[0]
    for p in p_tiles[1:]:
        l_el = l_el + p
    l = jnp.sum(l_el, axis=1, keepdims=True)

    vn = jnp.concatenate([vn_ref[0], jnp.zeros((pad, FOX_WIDTH), F32)], axis=0).astype(BF16)
    o = jnp.dot(p_tiles[n_pages].astype(BF16), vn, preferred_element_type=F32)
    for j in range(n_pages):
        o = o + lax.dot_general(p_tiles[j].astype(BF16), v_refs[j][0].astype(BF16), _NT,
                                preferred_element_type=F32)
    o = jnp.where(head_mask, o / l, 0.0)
    o_ref[0] = jnp.sum(o.reshape(FOX_HEADS, steps, FOX_WIDTH), axis=0).astype(BF16)


def _fox_sample(page_table, fq3, kn3, vn3, lfn3, sfx_mat, cache_kt, cache_vt, cache_lft):
    nseq, n_pages = page_table.shape
    steps = fq3.shape[1]
    seq_spec = lambda w: pl.BlockSpec((1, steps, w), lambda n, pt: (n, 0, 0))

    def page_spec(shape, j):
        return pl.BlockSpec((1,) + shape, lambda n, pt: (pt[n * n_pages + j], 0, 0))

    in_specs = [seq_spec(FOX_WIDTH), seq_spec(FOX_WIDTH), seq_spec(FOX_WIDTH), seq_spec(LANES),
                pl.BlockSpec(sfx_mat.shape, lambda n, pt: (0, 0))]
    in_specs += [page_spec((FOX_WIDTH, PAGE_SIZE), j) for j in range(n_pages)]
    in_specs += [page_spec((FOX_WIDTH, PAGE_SIZE), j) for j in range(n_pages)]
    in_specs += [page_spec((FOX_HEADS, PAGE_SIZE), j) for j in range(n_pages)]
    grid_spec = pltpu.PrefetchScalarGridSpec(
        num_scalar_prefetch=1,
        grid=(nseq,),
        in_specs=in_specs,
        out_specs=pl.BlockSpec((1, steps, FOX_WIDTH), lambda n, pt: (n, 0, 0)),
    )
    return pl.pallas_call(
        functools.partial(_fox_sample_kernel, n_pages=n_pages),
        grid_spec=grid_spec,
        out_shape=jax.ShapeDtypeStruct((nseq, steps, FOX_WIDTH), BF16),
        compiler_params=_params("arbitrary"),
        name="fox_sample",
    )(page_table.reshape(-1), fq3, kn3, vn3, lfn3, sfx_mat,
      *([cache_kt] * n_pages), *([cache_vt] * n_pages), *([cache_lft] * n_pages))


def _post_kernel(x_ref, mod_ref, oh_ref, gate_ref, of_ref, mg_ref, onorm_ref, n1post_ref, n2pre_ref,
                 n2post_ref, wbh_ref, wbf_ref, wout_ref, wup_ref, wdn_ref, y_ref, *, ff_chunk):
    g_, r_, d = x_ref.shape
    m = g_ * r_
    mod = mod_ref[...]
    gt1 = mod[:, :, 2 * d:3 * d]
    sh2 = mod[:, :, 3 * d:4 * d]
    sc2 = mod[:, :, 4 * d:5 * d]
    gt2 = mod[:, :, 5 * d:6 * d]

    oh = oh_ref[...]
    parts = [_rms(oh[:, h * HG_DV:(h + 1) * HG_DV], onorm_ref[...]) for h in range(HG_HEADS)]
    ohn = (jnp.concatenate(parts, axis=1) * gate_ref[...].astype(F32)).astype(BF16)
    br_h = jnp.dot(ohn, wbh_ref[...], preferred_element_type=F32)
    br_f = jnp.dot(of_ref[...], wbf_ref[...], preferred_element_type=F32)
    mg = mg_ref[...].astype(F32)
    z = (mg[:, 0:d] * br_h + mg[:, d:2 * d] * br_f).astype(BF16)
    y = jnp.dot(z, wout_ref[...], preferred_element_type=F32).reshape(g_, r_, d)
    x1 = x_ref[...] + gt1 * _rms(y, n1post_ref[...])
    h2 = (_rms(x1, n2pre_ref[...]) * (1.0 + sc2) + sh2).reshape(m, d).astype(BF16)
    u = jnp.zeros((m, d), F32)
    for c in range(wup_ref.shape[1] // ff_chunk):
        cs = slice(c * ff_chunk, (c + 1) * ff_chunk)
        a = jnp.maximum(jnp.dot(h2, wup_ref[:, cs], preferred_element_type=F32), 0.0)
        u = u + jnp.dot((a * a).astype(BF16), wdn_ref[cs, :], preferred_element_type=F32)
    y_ref[...] = x1 + gt2 * _rms(u.reshape(g_, r_, d), n2post_ref[...])


def _post(x3, mod3, oh, gate, of, mg, onorm, n1post, n2pre, n2post, wbh, wbf, wout, wup, wdn,
          groups, rows, ff_chunk=1024):
    nb, r_all, d = x3.shape
    nt = r_all // rows
    steps = (nb // groups) * nt
    m = groups * rows
    xmap = lambda s: (s // nt, s % nt, 0)
    mmap = lambda s: (s // nt, 0, 0)
    tmap = lambda s: (s, 0)
    tok = lambda a: pl.BlockSpec((m, a.shape[1]), tmap)
    consts = [onorm, n1post, n2pre, n2post, wbh, wbf, wout, wup, wdn]
    return pl.pallas_call(
        functools.partial(_post_kernel, ff_chunk=ff_chunk),
        grid=(steps,),
        in_specs=[pl.BlockSpec((groups, rows, d), xmap),
                  pl.BlockSpec((groups, 1, mod3.shape[2]), mmap),
                  tok(oh), tok(gate), tok(of), tok(mg)] + [_resident(c.shape) for c in consts],
        out_specs=pl.BlockSpec((groups, rows, d), xmap),
        out_shape=jax.ShapeDtypeStruct(x3.shape, F32),
        compiler_params=_params("arbitrary"),
        name="post",
    )(x3, mod3, oh, gate, of, mg, *consts)


def _head_expand_matrix():
    r = lax.broadcasted_iota(jnp.int32, (LANES, FOX_WIDTH), 0)
    c = lax.broadcasted_iota(jnp.int32, (LANES, FOX_WIDTH), 1) // FOX_DH
    return (r == c).astype(BF16)


def _page_suffix_matrix():
    r = lax.broadcasted_iota(jnp.int32, (PAGE_SIZE, 2 * PAGE_SIZE), 0)
    c = lax.broadcasted_iota(jnp.int32, (PAGE_SIZE, 2 * PAGE_SIZE), 1)
    return jnp.logical_or(r > c, c >= PAGE_SIZE).astype(BF16)


def kernel(x_prompt, x_sample, c_prompt, c_sample, cache_k, cache_v, cache_logf, state_hgrn, page_table,
           ada_w, ada_b, norm_mix_pre, norm_mix_post, norm_mlp_pre, norm_mlp_post, w_in,
           hgrn_lower_bounds, hgrn_onorm, fox_b_f, w_br_h, w_br_f, w_out, w_mlp_up, w_mlp_down):
    batch, seq, d = x_prompt.shape
    nseq, steps, _ = x_sample.shape
    layer = 0
    n_phys = cache_k.shape[1]

    w_t = jnp.transpose(w_in[layer]).astype(BF16)
    n_a = 4 * HG_WIDTH + 3 * FOX_WIDTH
    wa = w_t[:n_a]
    wff = jnp.pad(w_t[n_a:n_a + FOX_HEADS], ((0, LANES - FOX_HEADS), (0, 0)))
    wmg = w_t[n_a + FOX_HEADS:]
    bf128 = jnp.pad(fox_b_f[layer], (0, LANES - FOX_HEADS)).reshape(1, LANES)
    lbraw = hgrn_lower_bounds
    vec3 = lambda v: v.reshape(1, 1, -1)
    wbh, wbf = w_br_h[layer].astype(BF16), w_br_f[layer].astype(BF16)
    wout = w_out[layer].astype(BF16)
    wup, wdn = w_mlp_up[layer].astype(BF16), w_mlp_down[layer].astype(BF16)
    onorm = hgrn_onorm[layer].reshape(1, HG_DV)

    n_c = batch + nseq
    c_pad = -n_c % (2 * SUBLANES)
    c_all = jnp.concatenate([c_prompt, c_sample, jnp.zeros((c_pad, d), F32)], axis=0)
    mod = _ada(c_all, ada_w[layer], ada_b[layer].reshape(1, -1))
    mod_p = mod[:batch].reshape(batch, 1, -1)
    mod_s = mod[batch:n_c].reshape(nseq, 1, -1)

    tm = 256
    sgroups = tm // steps
    proj_args = (vec3(norm_mix_pre[layer]), lbraw, bf128, wa, wff, wmg)
    (q_p, g_p, k_p, v_p, gate_p, fq_p, fk_p, fv_p, lf8_p, lf128_p, mg_p) = _inproj(
        x_prompt, mod_p, *proj_args, groups=1, rows=tm, kv_transposed=True)
    (q_s, g_s, k_s, v_s, gate_s, fq_s, fk_s, fv_s, lf8_s, lf128_s, mg_s) = _inproj(
        x_sample, mod_s, *proj_args, groups=sgroups, rows=steps, kv_transposed=False)

    chunk_p = HG_CHUNK if seq % HG_CHUNK == 0 else seq
    oh_p, s_p = _hgrn_prompt(q_p, k_p, g_p, v_p, batch, seq, chunk_p)
    oh_s, s_s = _hgrn_sample(q_s, k_s, g_s, v_s, state_hgrn[layer], steps)

    tq = 256
    dexp, drow = _decay(lf128_p.reshape(batch, seq, LANES), _head_expand_matrix())
    pairs = FOX_WIDTH // LANES
    dk5 = drow.reshape(batch, pairs, LANES // FOX_DH, seq // tq, tq).transpose(0, 1, 3, 2, 4)
    of_p = _fox_prompt(fq_p, fk_p, fv_p, dexp, dk5, batch, seq, tq)

    ckt = jnp.transpose(cache_k[layer], (0, 2, 3, 1)).reshape(n_phys, FOX_WIDTH, PAGE_SIZE)
    cvt = jnp.transpose(cache_v[layer], (0, 2, 3, 1)).reshape(n_phys, FOX_WIDTH, PAGE_SIZE)
    clt = jnp.transpose(cache_logf[layer], (0, 2, 1))
    of_s = _fox_sample(page_table, fq_s.reshape(nseq, steps, FOX_WIDTH), fk_s.reshape(nseq, steps, FOX_WIDTH),
                       fv_s.reshape(nseq, steps, FOX_WIDTH), lf128_s.reshape(nseq, steps, LANES),
                       _page_suffix_matrix(), ckt, cvt, clt).reshape(nseq * steps, FOX_WIDTH)

    post_args = (onorm, vec3(norm_mix_post[layer]), vec3(norm_mlp_pre[layer]), vec3(norm_mlp_post[layer]),
                 wbh, wbf, wout, wup, wdn)
    y_p = _post(x_prompt, mod_p, oh_p, gate_p, of_p, mg_p, *post_args, groups=1, rows=tm)
    y_s = _post(x_sample, mod_s, oh_s, gate_s, of_s, mg_s, *post_args, groups=sgroups, rows=steps)

    k_prompt = fk_p.reshape(1, batch, FOX_HEADS, FOX_DH, seq).transpose(0, 1, 4, 2, 3)
    v_prompt = fv_p.reshape(1, batch, FOX_HEADS, FOX_DH, seq).transpose(0, 1, 4, 2, 3)
    logf_prompt = lf8_p.reshape(1, batch, seq, FOX_HEADS)
    k_sample = fk_s.reshape(1, nseq, steps, FOX_HEADS, FOX_DH)
    v_sample = fv_s.reshape(1, nseq, steps, FOX_HEADS, FOX_DH)
    logf_sample = lf8_s.reshape(1, nseq, steps, FOX_HEADS)
    return (y_p, y_s, k_prompt, v_prompt, logf_prompt, s_p[None], k_sample, v_sample, logf_sample, s_s[None])
```

```python
import functools

import jax
import jax.numpy as jnp
from jax import lax
from jax.experimental import pallas as pl
from jax.experimental.pallas import tpu as pltpu

F32 = jnp.float32
BF16 = jnp.bfloat16

LANES = 128
SUBLANES = 8
VMEM_LIMIT_BYTES = 56 * 1024 * 1024

D_MODEL = 1024
HG_HEADS = 4
HG_DK = 128
HG_DV = 128
HG_CHUNK = 32
FOX_HEADS = 8
FOX_DH = 64
FOX_WIDTH = FOX_HEADS * FOX_DH
HG_WIDTH = HG_HEADS * HG_DV
PAGE_SIZE = 128
N_MOD = 6
RMS_EPS = 1e-6
NEG_INF = float("-inf")
LOG2E = 1.4426950408889634
N_PIECES = 3
BIAS_SLOTS = 8

_NT = (((1,), (1,)), ((), ()))
_TN = (((0,), (0,)), ((), ()))


def _params(*sem):
    return pltpu.CompilerParams(dimension_semantics=sem, vmem_limit_bytes=VMEM_LIMIT_BYTES)


def _resident(shape):
    nd = len(shape)
    return pl.BlockSpec(shape, lambda *_: (0,) * nd, pipeline_mode=pl.Buffered(1))


def _sigmoid_pair(x):
    t = jnp.exp(-jnp.abs(x))
    r = 1.0 / (1.0 + t)
    tr = t * r
    pos = x >= 0
    return jnp.where(pos, r, tr), jnp.where(pos, tr, r)


def _split3(x):
    hi = x.astype(BF16)
    r1 = x - hi.astype(F32)
    mid = r1.astype(BF16)
    lo = (r1 - mid.astype(F32)).astype(BF16)
    return hi, mid, lo


def _dot_f32_lhs(x, w):
    hi, mid, lo = _split3(x)
    d = lambda p: jnp.dot(p, w, preferred_element_type=F32)
    return (d(lo) + d(mid)) + d(hi)


def _cumsum_rows(x, period):
    row = lax.broadcasted_iota(jnp.int32, x.shape, 0) & (period - 1)
    s = 1
    while s < period:
        x = x + jnp.where(row >= s, pltpu.roll(x, s, axis=0), 0.0)
        s *= 2
    return x


def _rms(x, w):
    return x * lax.rsqrt(jnp.mean(x * x, axis=-1, keepdims=True) + RMS_EPS) * w


def _ada_kernel(c_ref, w_ref, b_ref, o_ref):
    c = c_ref[...]
    s, _ = _sigmoid_pair(c)
    a = (c * s).astype(BF16)
    o_ref[...] = jnp.dot(a, w_ref[...].astype(BF16), preferred_element_type=F32) + b_ref[...]


def _ada(c, w, b, tn=1536):
    m, d = c.shape
    n = w.shape[1]
    return pl.pallas_call(
        _ada_kernel,
        grid=(n // tn,),
        in_specs=[pl.BlockSpec((m, d), lambda j: (0, 0)),
                  pl.BlockSpec((d, tn), lambda j: (0, j)),
                  pl.BlockSpec((1, tn), lambda j: (0, j))],
        out_specs=pl.BlockSpec((m, tn), lambda j: (0, j)),
        out_shape=jax.ShapeDtypeStruct((m, n), F32),
        compiler_params=_params("arbitrary"),
        name="ada",
    )(c, w, b)


def _inproj_kernel(x_ref, mod_ref, n1_ref, lbraw_ref, bf_ref, wa_ref, wff_ref, wmg_ref,
                   q_ref, g_ref, k_ref, v_ref, gate_ref, fq_ref, fk_ref, fv_ref,
                   lf8_ref, lf128_ref, mg_ref, *, fox_transposed):
    g_, r_, d = x_ref.shape
    m = g_ * r_
    x = x_ref[...]
    mod = mod_ref[...]
    sh1 = mod[:, :, 0:d]
    sc1 = mod[:, :, d:2 * d]
    h = _rms(x, n1_ref[...]) * (1.0 + sc1) + sh1
    hb = h.reshape(m, d).astype(BF16)

    raw = lbraw_ref[...]
    e = jnp.exp(raw - jnp.max(raw, axis=0, keepdims=True))
    lb = e[0:1, :] / jnp.sum(e, axis=0, keepdims=True)

    w = HG_WIDTH

    def proj(c):
        return lax.dot_general(hb, wa_ref[c * w:(c + 1) * w, :], _NT, preferred_element_type=F32)

    def proj_t(c):
        return lax.dot_general(wa_ref[c * w:(c + 1) * w, :], hb, _NT, preferred_element_type=F32)

    q_ref[...] = proj(0) * (HG_DK ** -0.5)
    s_pos, s_neg = _sigmoid_pair(proj(1))
    g_ref[...] = jnp.log(lb + (1.0 - lb) * s_pos)
    k_ref[...] = (1.0 - lb) * s_neg
    v_ref[...] = proj(2).astype(BF16)
    hg = proj(3)
    gate_ref[...] = (hg * _sigmoid_pair(hg)[0]).astype(BF16)
    if fox_transposed:
        fq_ref[0] = (proj_t(4) * (LOG2E * FOX_DH ** -0.5)).astype(BF16)
        fk_ref[0] = proj_t(5)
        fv_ref[0] = proj_t(6)
    else:
        fq_ref[...] = (proj(4) * (FOX_DH ** -0.5)).astype(BF16)
        fk_ref[...] = proj(5)
        fv_ref[...] = proj(6)
    z = lax.dot_general(hb, wff_ref[...], _NT, preferred_element_type=F32) + bf_ref[...]
    lf = jnp.minimum(z, 0.0) - jnp.log1p(jnp.exp(-jnp.abs(z)))
    lf128_ref[...] = lf
    lf8_ref[...] = lf[:, 0:FOX_HEADS]
    for c in range(wmg_ref.shape[0] // w):
        mgc = lax.dot_general(hb, wmg_ref[c * w:(c + 1) * w, :], _NT, preferred_element_type=F32)
        mg_ref[:, c * w:(c + 1) * w] = _sigmoid_pair(mgc)[0].astype(BF16)


def _inproj(x3, mod3, n1, lbraw, bf128, wa, wff, wmg, groups, rows, fox_transposed):
    assert groups == 1 or not fox_transposed
    nb, r_all, d = x3.shape
    nt = r_all // rows
    steps = (nb // groups) * nt
    t = nb * r_all
    m = groups * rows
    xmap = lambda s: (s // nt, s % nt, 0)
    mmap = lambda s: (s // nt, 0, 0)
    omap = lambda s: (s, 0)

    def out(width, dtype):
        return jax.ShapeDtypeStruct((t, width), dtype), pl.BlockSpec((m, width), omap)

    def fox(dtype):
        if not fox_transposed:
            return out(FOX_WIDTH, dtype)
        return (jax.ShapeDtypeStruct((nb, FOX_WIDTH, r_all), dtype),
                pl.BlockSpec((1, FOX_WIDTH, m), lambda s: (s // nt, 0, s % nt)))

    outs = [out(HG_WIDTH, F32), out(HG_WIDTH, F32), out(HG_WIDTH, F32), out(HG_WIDTH, BF16),
            out(HG_WIDTH, BF16), fox(BF16), fox(F32), fox(F32),
            out(FOX_HEADS, F32), out(LANES, F32), out(wmg.shape[0], BF16)]
    return pl.pallas_call(
        functools.partial(_inproj_kernel, fox_transposed=fox_transposed),
        grid=(steps,),
        in_specs=[pl.BlockSpec((groups, rows, d), xmap),
                  pl.BlockSpec((groups, 1, mod3.shape[2]), mmap),
                  _resident(n1.shape), _resident(lbraw.shape), _resident(bf128.shape),
                  _resident(wa.shape), _resident(wff.shape), _resident(wmg.shape)],
        out_specs=[o[1] for o in outs],
        out_shape=[o[0] for o in outs],
        compiler_params=_params("arbitrary"),
        name="inproj",
    )(x3, mod3, n1, lbraw, bf128, wa, wff, wmg)


def _decay_kernel(lf_ref, place_ref, bk_ref, bqt_ref):
    seq = lf_ref.shape[1]
    lane = lax.broadcasted_iota(jnp.int32, (LANES, LANES), 1)
    slot = lane & (BIAS_SLOTS - 1)
    used = lane < FOX_HEADS * BIAS_SLOTS
    ones_k = jnp.where(jnp.logical_and(used, slot < N_PIECES), 1.0, 0.0)
    ones_q = jnp.where(jnp.logical_and(used, jnp.logical_and(slot >= N_PIECES, slot < 2 * N_PIECES)), 1.0, 0.0)
    carry = jnp.zeros((1, LANES), F32)
    for j in range(seq // LANES):
        sl = slice(j * LANES, (j + 1) * LANES)
        cs = _cumsum_rows(lf_ref[0, sl, :], LANES) + carry
        carry = cs[LANES - 1:LANES, :]
        pieces = _split3(cs * LOG2E)

        def place(base):
            return sum(jnp.dot(pieces[j], place_ref[base + j], preferred_element_type=F32)
                       for j in range(N_PIECES))

        bk_ref[0, sl, :] = (ones_k - place(N_PIECES)).astype(BF16)
        bqt_ref[0, :, sl] = (ones_q + place(0)).T.astype(BF16)


def _decay(lf3, place):
    b, seq, _ = lf3.shape
    return pl.pallas_call(
        _decay_kernel,
        grid=(b,),
        in_specs=[pl.BlockSpec((1, seq, LANES), lambda i: (i, 0, 0)), _resident(place.shape)],
        out_specs=[pl.BlockSpec((1, seq, LANES), lambda i: (i, 0, 0)),
                   pl.BlockSpec((1, LANES, seq), lambda i: (i, 0, 0))],
        out_shape=[jax.ShapeDtypeStruct((b, seq, LANES), BF16),
                   jax.ShapeDtypeStruct((b, LANES, seq), BF16)],
        compiler_params=_params("arbitrary"),
        name="decay",
    )(lf3, place)


def _hgrn_block(q, k, g, v, chunk, state_in, state_out):
    rows = q.shape[0]
    n_chunks = rows // chunk
    shift = chunk.bit_length() - 1
    b = _cumsum_rows(g, chunk)
    qt = (q * jnp.exp(b)).astype(BF16)
    kt = (k * jnp.exp(-b)).astype(BF16)
    b3 = b.reshape(n_chunks, chunk, HG_DK)
    bl3 = b3[:, chunk - 1:chunk, :]
    kd = (k.reshape(n_chunks, chunk, HG_DK) * jnp.exp(bl3 - b3)).reshape(rows, HG_DK).astype(BF16)

    a = lax.dot_general(qt, kt, _NT, preferred_element_type=F32)
    v_t = v.astype(F32).T.astype(BF16)
    col_chunk = lax.broadcasted_iota(jnp.int32, v_t.shape, 1) >> shift
    v_blocks = jnp.concatenate([jnp.where(col_chunk == c, v_t, jnp.zeros_like(v_t)) for c in range(n_chunks)],
                               axis=0)
    incr = jnp.dot(v_blocks, kd, preferred_element_type=F32)
    ri = lax.broadcasted_iota(jnp.int32, (rows, rows), 0)
    ci = lax.broadcasted_iota(jnp.int32, (rows, rows), 1)
    keep = (ri - ci).astype(jnp.uint32) <= (ri & (chunk - 1)).astype(jnp.uint32)
    o_intra = jnp.dot(jnp.where(keep, a, 0.0).astype(BF16), v, preferred_element_type=F32)

    entering = []
    st = None
    for c in range(n_chunks):
        st = state_in(c, st)
        entering.append(st.astype(BF16))
        st = st * jnp.exp(bl3[c]) + incr[c * HG_DV:(c + 1) * HG_DV, :]
        state_out(c, st)
    o_inter = [lax.dot_general(qt[c * chunk:(c + 1) * chunk, :], entering[c], _NT, preferred_element_type=F32)
               for c in range(n_chunks)]
    return o_intra + jnp.concatenate(o_inter, axis=0)


def _hgrn_prompt_kernel(q_ref, k_ref, g_ref, v_ref, o_ref, s_ref, *, chunk):
    seq = q_ref.shape[0]

    def body(j, st0):
        rs = pl.ds(pl.multiple_of(j * LANES, LANES), LANES)
        last = []

        def state_in(c, prev):
            return st0 if prev is None else prev

        def state_out(c, st):
            if c == LANES // chunk - 1:
                last.append(st)

        o_ref[rs, :] = _hgrn_block(q_ref[rs, :], k_ref[rs, :], g_ref[rs, :], v_ref[rs, :],
                                   chunk, state_in, state_out)
        return last[0]

    st = lax.fori_loop(0, seq // LANES, body, jnp.zeros((HG_DV, HG_DK), F32), unroll=2)
    s_ref[0, 0] = st.T


def _hgrn_prompt(q, k, g, v, batch, seq, chunk):
    spec = pl.BlockSpec((seq, HG_DK), lambda b, h: (b, h))
    return pl.pallas_call(
        functools.partial(_hgrn_prompt_kernel, chunk=chunk),
        grid=(batch, HG_HEADS),
        in_specs=[spec, spec, spec, spec],
        out_specs=[spec, pl.BlockSpec((1, 1, HG_DK, HG_DV), lambda b, h: (b, h, 0, 0))],
        out_shape=[jax.ShapeDtypeStruct((batch * seq, HG_WIDTH), F32),
                   jax.ShapeDtypeStruct((batch, HG_HEADS, HG_DK, HG_DV), F32)],
        compiler_params=_params("arbitrary", "arbitrary"),
        name="hgrn_prompt",
    )(q, k, g, v)


def _hgrn_sample_kernel(q_ref, k_ref, g_ref, v_ref, s0_ref, o_ref, s_ref, *, chunk):
    def state_in(c, prev):
        return s0_ref[c, 0].T

    def state_out(c, st):
        s_ref[c, 0] = st.T

    o_ref[...] = _hgrn_block(q_ref[...], k_ref[...], g_ref[...], v_ref[...], chunk, state_in, state_out)


def _hgrn_sample(q, k, g, v, s0, chunk):
    t = q.shape[0]
    per = LANES // chunk
    spec = pl.BlockSpec((LANES, HG_DK), lambda i, h: (i, h))
    sspec = pl.BlockSpec((per, 1, HG_DK, HG_DV), lambda i, h: (i, h, 0, 0))
    return pl.pallas_call(
        functools.partial(_hgrn_sample_kernel, chunk=chunk),
        grid=(t // LANES, HG_HEADS),
        in_specs=[spec, spec, spec, spec, sspec],
        out_specs=[spec, sspec],
        out_shape=[jax.ShapeDtypeStruct((t, HG_WIDTH), F32),
                   jax.ShapeDtypeStruct(s0.shape, F32)],
        compiler_params=_params("arbitrary", "arbitrary"),
        name="hgrn_sample",
    )(q, k, g, v, s0)


ONES_ROWS = 16


def _fox_prompt_kernel(qt_ref, kt_ref, vt_ref, bk_ref, bqt_ref, o_ref, kn_ref, va_ref, *, tk):
    tq = qt_ref.shape[1]
    seq = kt_ref.shape[1]
    n_heads = LANES // FOX_DH
    pair = pl.program_id(1)
    i = pl.program_id(2)

    @pl.when(i == 0)
    def _():
        for c in range(seq // LANES):
            cs = slice(c * LANES, (c + 1) * LANES)
            kn_ref[cs, 0:LANES] = kt_ref[:, cs].T.astype(BF16)
        kn_ref[:, LANES:2 * LANES] = bk_ref[...]
        for c in range(seq // tk):
            for e in range(n_heads):
                va_ref[c, e, 0:FOX_DH, :] = vt_ref[e * FOX_DH:(e + 1) * FOX_DH, c * tk:(c + 1) * tk].astype(BF16)
                va_ref[c, e, FOX_DH:FOX_DH + ONES_ROWS, :] = jnp.ones((ONES_ROWS, tk), BF16)

    qt = qt_ref[...]
    bqt = bqt_ref[...]
    row = lax.broadcasted_iota(jnp.int32, qt.shape, 0)
    rhs = []
    for e in range(n_heads):
        q_e = jnp.where(row // FOX_DH == e, qt, jnp.zeros_like(qt))
        b_e = jnp.where(row // BIAS_SLOTS == pair * n_heads + e, bqt, jnp.zeros_like(bqt))
        rhs.append(jnp.concatenate([q_e, b_e], axis=0))
    k_pos = lax.broadcasted_iota(jnp.int32, (tk, tq), 0)
    q_pos = lax.broadcasted_iota(jnp.int32, (tk, tq), 1) + i * tq

    def tile(jj, carry, masked):
        lhs = kn_ref[pl.ds(pl.multiple_of(jj * tk, tk), tk), :]
        ss = [jnp.dot(lhs, rhs[e], preferred_element_type=F32) for e in range(n_heads)]
        ps, stats = [], []
        for e in range(n_heads):
            m = carry[e][0]
            s = jnp.where(k_pos + jj * tk <= q_pos, ss[e], NEG_INF) if masked else ss[e]
            m_new = jnp.maximum(m, jnp.max(s, axis=0, keepdims=True))
            ps.append(jnp.exp2(s - m_new).astype(BF16))
            stats.append((m_new, jnp.exp2(m - m_new)))
        out = []
        for e in range(n_heads):
            _, l, acc = carry[e]
            m_new, alpha = stats[e]
            pv = jnp.dot(va_ref[jj, e], ps[e], preferred_element_type=F32)
            out.append((m_new, alpha * l + pv[FOX_DH:FOX_DH + 1, :], alpha * acc + pv[0:FOX_DH, :]))
        return tuple(out)

    init = tuple((jnp.full((1, tq), NEG_INF, F32), jnp.zeros((1, tq), F32), jnp.zeros((FOX_DH, tq), F32))
                 for _ in range(n_heads))
    n_full = (i * tq) // tk
    carry = lax.fori_loop(0, n_full, lambda jj, c: tile(jj, c, False), init)
    carry = tile(n_full, carry, True)
    o_t = jnp.concatenate([acc / l for (_, l, acc) in carry], axis=0)
    o_ref[...] = o_t.T.astype(BF16)


def _fox_prompt(fqt, fkt, fvt, bk, bqt, batch, seq, tq, tk):
    assert tk % tq == 0 and seq % tk == 0
    nq = seq // tq
    pairs = FOX_WIDTH // LANES
    n_heads = LANES // FOX_DH
    whole = lambda b, p, i: (b, p, 0)
    return pl.pallas_call(
        functools.partial(_fox_prompt_kernel, tk=tk),
        grid=(batch, pairs, nq),
        in_specs=[pl.BlockSpec((None, LANES, tq), lambda b, p, i: (b, p, i)),
                  pl.BlockSpec((None, LANES, seq), whole),
                  pl.BlockSpec((None, LANES, seq), whole),
                  pl.BlockSpec((None, seq, LANES), lambda b, p, i: (b, 0, 0)),
                  pl.BlockSpec((None, LANES, tq), lambda b, p, i: (b, 0, i))],
        out_specs=pl.BlockSpec((tq, LANES), lambda b, p, i: (b * nq + i, p)),
        out_shape=jax.ShapeDtypeStruct((batch * seq, FOX_WIDTH), BF16),
        scratch_shapes=[pltpu.VMEM((seq, 2 * LANES), BF16),
                        pltpu.VMEM((seq // tk, n_heads, FOX_DH + ONES_ROWS, tk), BF16)],
        compiler_params=_params("arbitrary", "arbitrary", "arbitrary"),
        name="fox_prompt",
    )(fqt, fkt, fvt, bk, bqt)


def _fox_sample_kernel(pt_ref, q_ref, kn_ref, vn_ref, lfn_ref, sfx_ref, *refs, n_pages):
    del pt_ref
    k_refs = refs[:n_pages]
    v_refs = refs[n_pages:2 * n_pages]
    lf_refs = refs[2 * n_pages:3 * n_pages]
    o_ref = refs[3 * n_pages]
    steps = q_ref.shape[1]
    rows = FOX_HEADS * steps
    pad = PAGE_SIZE - steps

    def per_head_rows(x8):
        return jnp.broadcast_to(x8[:, None, :], (FOX_HEADS, steps, x8.shape[1])).reshape(rows, x8.shape[1])

    q = q_ref[0].astype(F32)
    q_rows = jnp.concatenate([q] * FOX_HEADS, axis=0)
    row_head = lax.broadcasted_iota(jnp.int32, (rows, FOX_WIDTH), 0) // steps
    lane_head = lax.broadcasted_iota(jnp.int32, (rows, FOX_WIDTH), 1) // FOX_DH
    head_mask = row_head == lane_head
    qbd = jnp.where(head_mask, q_rows, 0.0).astype(BF16)

    pn = _cumsum_rows(lfn_ref[0], steps)
    pn_rows = jnp.concatenate([pn] * FOX_HEADS, axis=0)
    r_h = lax.broadcasted_iota(jnp.int32, (rows, LANES), 0) // steps
    r_i = lax.broadcasted_iota(jnp.int32, (rows, LANES), 0) % steps
    c_l = lax.broadcasted_iota(jnp.int32, (rows, LANES), 1)
    pcol = jnp.sum(jnp.where(c_l == r_h, pn_rows, 0.0), axis=1, keepdims=True)
    pn_t = jnp.concatenate([pn, jnp.zeros((pad, LANES), F32)], axis=0).T[0:FOX_HEADS, :]

    lf_all = jnp.concatenate([lf_refs[j][0] for j in range(n_pages)], axis=0)
    sfx = _dot_f32_lhs(lf_all, sfx_ref[...])
    carry = jnp.zeros((FOX_HEADS, PAGE_SIZE), F32)
    page_bias = [None] * n_pages
    for j in reversed(range(n_pages)):
        blk = sfx[j * FOX_HEADS:(j + 1) * FOX_HEADS, :]
        page_bias[j] = blk[:, 0:PAGE_SIZE] + carry
        carry = carry + blk[:, PAGE_SIZE:2 * PAGE_SIZE]

    s_tiles = []
    for j in range(n_pages):
        s = jnp.dot(qbd, k_refs[j][0].astype(BF16), preferred_element_type=F32)
        s_tiles.append(s + (per_head_rows(page_bias[j]) + pcol))
    kn = jnp.concatenate([kn_ref[0], jnp.zeros((pad, FOX_WIDTH), F32)], axis=0).astype(BF16)
    s_new = lax.dot_general(qbd, kn, _NT, preferred_element_type=F32) + (pcol - per_head_rows(pn_t))
    s_tiles.append(jnp.where(c_l <= r_i, s_new, NEG_INF))

    m_el = s_tiles[0]
    for s in s_tiles[1:]:
        m_el = jnp.maximum(m_el, s)
    m = jnp.max(m_el, axis=1, keepdims=True)
    p_tiles = [jnp.exp(s - m) for s in s_tiles]
    l_el = p_tiles[0]
    for p in p_tiles[1:]:
        l_el = l_el + p
    l = jnp.sum(l_el, axis=1, keepdims=True)

    vn = jnp.concatenate([vn_ref[0], jnp.zeros((pad, FOX_WIDTH), F32)], axis=0).astype(BF16)
    o = jnp.dot(p_tiles[n_pages].astype(BF16), vn, preferred_element_type=F32)
    for j in range(n_pages):
        o = o + lax.dot_general(p_tiles[j].astype(BF16), v_refs[j][0].astype(BF16), _NT,
                                preferred_element_type=F32)
    o = jnp.where(head_mask, o / l, 0.0)
    o_ref[0] = jnp.sum(o.reshape(FOX_HEADS, steps, FOX_WIDTH), axis=0).astype(BF16)


def _fox_sample(page_table, fq3, kn3, vn3, lfn3, sfx_mat, cache_kt, cache_vt, cache_lft):
    nseq, n_pages = page_table.shape
    steps = fq3.shape[1]
    seq_spec = lambda w: pl.BlockSpec((1, steps, w), lambda n, pt: (n, 0, 0))

    def page_spec(shape, j):
        return pl.BlockSpec((1,) + shape, lambda n, pt: (pt[n * n_pages + j], 0, 0))

    in_specs = [seq_spec(FOX_WIDTH), seq_spec(FOX_WIDTH), seq_spec(FOX_WIDTH), seq_spec(LANES),
                pl.BlockSpec(sfx_mat.shape, lambda n, pt: (0, 0))]
    in_specs += [page_spec((FOX_WIDTH, PAGE_SIZE), j) for j in range(n_pages)]
    in_specs += [page_spec((FOX_WIDTH, PAGE_SIZE), j) for j in range(n_pages)]
    in_specs += [page_spec((FOX_HEADS, PAGE_SIZE), j) for j in range(n_pages)]
    grid_spec = pltpu.PrefetchScalarGridSpec(
        num_scalar_prefetch=1,
        grid=(nseq,),
        in_specs=in_specs,
        out_specs=pl.BlockSpec((1, steps, FOX_WIDTH), lambda n, pt: (n, 0, 0)),
    )
    return pl.pallas_call(
        functools.partial(_fox_sample_kernel, n_pages=n_pages),
        grid_spec=grid_spec,
        out_shape=jax.ShapeDtypeStruct((nseq, steps, FOX_WIDTH), BF16),
        compiler_params=_params("arbitrary"),
        name="fox_sample",
    )(page_table.reshape(-1), fq3, kn3, vn3, lfn3, sfx_mat,
      *([cache_kt] * n_pages), *([cache_vt] * n_pages), *([cache_lft] * n_pages))


def _post_kernel(x_ref, mod_ref, oh_ref, gate_ref, of_ref, mg_ref, onorm_ref, n1post_ref, n2pre_ref,
                 n2post_ref, wbh_ref, wbf_ref, wout_ref, wup_ref, wdn_ref, y_ref, *, ff_chunk):
    g_, r_, d = x_ref.shape
    m = g_ * r_
    mod = mod_ref[...]
    gt1 = mod[:, :, 2 * d:3 * d]
    sh2 = mod[:, :, 3 * d:4 * d]
    sc2 = mod[:, :, 4 * d:5 * d]
    gt2 = mod[:, :, 5 * d:6 * d]

    oh = oh_ref[...]
    parts = [_rms(oh[:, h * HG_DV:(h + 1) * HG_DV], onorm_ref[...]) for h in range(HG_HEADS)]
    ohn = (jnp.concatenate(parts, axis=1) * gate_ref[...].astype(F32)).astype(BF16)
    br_h = jnp.dot(ohn, wbh_ref[...], preferred_element_type=F32)
    br_f = jnp.dot(of_ref[...], wbf_ref[...], preferred_element_type=F32)
    mg = mg_ref[...].astype(F32)
    z = (mg[:, 0:d] * br_h + mg[:, d:2 * d] * br_f).astype(BF16)
    y = jnp.dot(z, wout_ref[...], preferred_element_type=F32).reshape(g_, r_, d)
    x1 = x_ref[...] + gt1 * _rms(y, n1post_ref[...])
    h2 = (_rms(x1, n2pre_ref[...]) * (1.0 + sc2) + sh2).reshape(m, d).astype(BF16)
    u = jnp.zeros((m, d), F32)
    for c in range(wup_ref.shape[1] // ff_chunk):
        cs = slice(c * ff_chunk, (c + 1) * ff_chunk)
        a = jnp.maximum(jnp.dot(h2, wup_ref[:, cs], preferred_element_type=F32), 0.0)
        u = u + jnp.dot((a * a).astype(BF16), wdn_ref[cs, :], preferred_element_type=F32)
    y_ref[...] = x1 + gt2 * _rms(u.reshape(g_, r_, d), n2post_ref[...])


def _post(x3, mod3, oh, gate, of, mg, onorm, n1post, n2pre, n2post, wbh, wbf, wout, wup, wdn,
          groups, rows, ff_chunk=1024):
    nb, r_all, d = x3.shape
    nt = r_all // rows
    steps = (nb // groups) * nt
    m = groups * rows
    xmap = lambda s: (s // nt, s % nt, 0)
    mmap = lambda s: (s // nt, 0, 0)
    tmap = lambda s: (s, 0)
    tok = lambda a: pl.BlockSpec((m, a.shape[1]), tmap)
    consts = [onorm, n1post, n2pre, n2post, wbh, wbf, wout, wup, wdn]
    return pl.pallas_call(
        functools.partial(_post_kernel, ff_chunk=ff_chunk),
        grid=(steps,),
        in_specs=[pl.BlockSpec((groups, rows, d), xmap),
                  pl.BlockSpec((groups, 1, mod3.shape[2]), mmap),
                  tok(oh), tok(gate), tok(of), tok(mg)] + [_resident(c.shape) for c in consts],
        out_specs=pl.BlockSpec((groups, rows, d), xmap),
        out_shape=jax.ShapeDtypeStruct(x3.shape, F32),
        compiler_params=_params("arbitrary"),
        name="post",
    )(x3, mod3, oh, gate, of, mg, *consts)


def _bias_place_matrices():
    shape = (2 * N_PIECES, LANES, LANES)
    s = lax.broadcasted_iota(jnp.int32, shape, 0)
    r = lax.broadcasted_iota(jnp.int32, shape, 1)
    c = lax.broadcasted_iota(jnp.int32, shape, 2)
    return jnp.logical_and(r < FOX_HEADS, c == BIAS_SLOTS * r + s).astype(BF16)


def _page_suffix_matrix():
    r = lax.broadcasted_iota(jnp.int32, (PAGE_SIZE, 2 * PAGE_SIZE), 0)
    c = lax.broadcasted_iota(jnp.int32, (PAGE_SIZE, 2 * PAGE_SIZE), 1)
    return jnp.logical_or(r > c, c >= PAGE_SIZE).astype(BF16)


def kernel(x_prompt, x_sample, c_prompt, c_sample, cache_k, cache_v, cache_logf, state_hgrn, page_table,
           ada_w, ada_b, norm_mix_pre, norm_mix_post, norm_mlp_pre, norm_mlp_post, w_in,
           hgrn_lower_bounds, hgrn_onorm, fox_b_f, w_br_h, w_br_f, w_out, w_mlp_up, w_mlp_down):
    batch, seq, d = x_prompt.shape
    nseq, steps, _ = x_sample.shape
    layer = 0
    n_phys = cache_k.shape[1]

    w_t = jnp.transpose(w_in[layer]).astype(BF16)
    n_a = 4 * HG_WIDTH + 3 * FOX_WIDTH
    wa = w_t[:n_a]
    wff = jnp.pad(w_t[n_a:n_a + FOX_HEADS], ((0, LANES - FOX_HEADS), (0, 0)))
    wmg = w_t[n_a + FOX_HEADS:]
    bf128 = jnp.pad(fox_b_f[layer], (0, LANES - FOX_HEADS)).reshape(1, LANES)
    lbraw = hgrn_lower_bounds
    vec3 = lambda v: v.reshape(1, 1, -1)
    wbh, wbf = w_br_h[layer].astype(BF16), w_br_f[layer].astype(BF16)
    wout = w_out[layer].astype(BF16)
    wup, wdn = w_mlp_up[layer].astype(BF16), w_mlp_down[layer].astype(BF16)
    onorm = hgrn_onorm[layer].reshape(1, HG_DV)

    n_c = batch + nseq
    c_pad = -n_c % (2 * SUBLANES)
    c_all = jnp.concatenate([c_prompt, c_sample, jnp.zeros((c_pad, d), F32)], axis=0)
    mod = _ada(c_all, ada_w[layer], ada_b[layer].reshape(1, -1))
    mod_p = mod[:batch].reshape(batch, 1, -1)
    mod_s = mod[batch:n_c].reshape(nseq, 1, -1)

    tm = 256
    sgroups = tm // steps
    proj_args = (vec3(norm_mix_pre[layer]), lbraw, bf128, wa, wff, wmg)
    (q_p, g_p, k_p, v_p, gate_p, fq_p, fk_p, fv_p, lf8_p, lf128_p, mg_p) = _inproj(
        x_prompt, mod_p, *proj_args, groups=1, rows=tm, fox_transposed=True)
    (q_s, g_s, k_s, v_s, gate_s, fq_s, fk_s, fv_s, lf8_s, lf128_s, mg_s) = _inproj(
        x_sample, mod_s, *proj_args, groups=sgroups, rows=steps, fox_transposed=False)

    chunk_p = HG_CHUNK if seq % HG_CHUNK == 0 else seq
    oh_p, s_p = _hgrn_prompt(q_p, k_p, g_p, v_p, batch, seq, chunk_p)
    oh_s, s_s = _hgrn_sample(q_s, k_s, g_s, v_s, state_hgrn[layer], steps)

    tq, tk = 256, 512
    bias_k, bias_qt = _decay(lf128_p.reshape(batch, seq, LANES), _bias_place_matrices())
    of_p = _fox_prompt(fq_p, fk_p, fv_p, bias_k, bias_qt, batch, seq, tq, tk)

    ckt = jnp.transpose(cache_k[layer], (0, 2, 3, 1)).reshape(n_phys, FOX_WIDTH, PAGE_SIZE)
    cvt = jnp.transpose(cache_v[layer], (0, 2, 3, 1)).reshape(n_phys, FOX_WIDTH, PAGE_SIZE)
    clt = jnp.transpose(cache_logf[layer], (0, 2, 1))
    of_s = _fox_sample(page_table, fq_s.reshape(nseq, steps, FOX_WIDTH), fk_s.reshape(nseq, steps, FOX_WIDTH),
                       fv_s.reshape(nseq, steps, FOX_WIDTH), lf128_s.reshape(nseq, steps, LANES),
                       _page_suffix_matrix(), ckt, cvt, clt).reshape(nseq * steps, FOX_WIDTH)

    post_args = (onorm, vec3(norm_mix_post[layer]), vec3(norm_mlp_pre[layer]), vec3(norm_mlp_post[layer]),
                 wbh, wbf, wout, wup, wdn)
    y_p = _post(x_prompt, mod_p, oh_p, gate_p, of_p, mg_p, *post_args, groups=1, rows=tm)
    y_s = _post(x_sample, mod_s, oh_s, gate_s, of_s, mg_s, *post_args, groups=sgroups, rows=steps)

    k_prompt = fk_p.reshape(1, batch, FOX_HEADS, FOX_DH, seq).transpose(0, 1, 4, 2, 3)
    v_prompt = fv_p.reshape(1, batch, FOX_HEADS, FOX_DH, seq).transpose(0, 1, 4, 2, 3)
    logf_prompt = lf8_p.reshape(1, batch, seq, FOX_HEADS)
    k_sample = fk_s.reshape(1, nseq, steps, FOX_HEADS, FOX_DH)
    v_sample = fv_s.reshape(1, nseq, steps, FOX_HEADS, FOX_DH)
    logf_sample = lf8_s.reshape(1, nseq, steps, FOX_HEADS)
    return (y_p, y_s, k_prompt, v_prompt, logf_prompt, s_p[None], k_sample, v_sample, logf_sample, s_s[None])
```

```python
import functools

import jax
import jax.numpy as jnp
from jax import lax
from jax.experimental import pallas as pl
from jax.experimental.pallas import tpu as pltpu

F32 = jnp.float32
BF16 = jnp.bfloat16

LANES = 128
SUBLANES = 8
VMEM_LIMIT_BYTES = 56 * 1024 * 1024

D_MODEL = 1024
HG_HEADS = 4
HG_DK = 128
HG_DV = 128
HG_CHUNK = 32
FOX_HEADS = 8
FOX_DH = 64
FOX_WIDTH = FOX_HEADS * FOX_DH
HG_WIDTH = HG_HEADS * HG_DV
PAGE_SIZE = 128
N_MOD = 6
RMS_EPS = 1e-6
NEG_INF = float("-inf")
LOG2E = 1.4426950408889634
N_PIECES = 3
BIAS_SLOTS = 8

_NT = (((1,), (1,)), ((), ()))
_TN = (((0,), (0,)), ((), ()))


def _params(*sem):
    return pltpu.CompilerParams(dimension_semantics=sem, vmem_limit_bytes=VMEM_LIMIT_BYTES)


def _resident(shape):
    nd = len(shape)
    return pl.BlockSpec(shape, lambda *_: (0,) * nd, pipeline_mode=pl.Buffered(1))


def _sigmoid_pair(x):
    t = jnp.exp(-jnp.abs(x))
    r = 1.0 / (1.0 + t)
    tr = t * r
    pos = x >= 0
    return jnp.where(pos, r, tr), jnp.where(pos, tr, r)


def _split3(x):
    hi = x.astype(BF16)
    r1 = x - hi.astype(F32)
    mid = r1.astype(BF16)
    lo = (r1 - mid.astype(F32)).astype(BF16)
    return hi, mid, lo


def _dot_f32_lhs(x, w):
    hi, mid, lo = _split3(x)
    d = lambda p: jnp.dot(p, w, preferred_element_type=F32)
    return (d(lo) + d(mid)) + d(hi)


def _cumsum_rows(x, period):
    row = lax.broadcasted_iota(jnp.int32, x.shape, 0) & (period - 1)
    s = 1
    while s < period:
        x = x + jnp.where(row >= s, pltpu.roll(x, s, axis=0), 0.0)
        s *= 2
    return x


def _rms(x, w):
    return x * lax.rsqrt(jnp.mean(x * x, axis=-1, keepdims=True) + RMS_EPS) * w


def _ada_kernel(c_ref, w_ref, b_ref, o_ref):
    c = c_ref[...]
    s, _ = _sigmoid_pair(c)
    a = (c * s).astype(BF16)
    o_ref[...] = jnp.dot(a, w_ref[...].astype(BF16), preferred_element_type=F32) + b_ref[...]


def _ada(c, w, b, tn=1536):
    m, d = c.shape
    n = w.shape[1]
    return pl.pallas_call(
        _ada_kernel,
        grid=(n // tn,),
        in_specs=[pl.BlockSpec((m, d), lambda j: (0, 0)),
                  pl.BlockSpec((d, tn), lambda j: (0, j)),
                  pl.BlockSpec((1, tn), lambda j: (0, j))],
        out_specs=pl.BlockSpec((m, tn), lambda j: (0, j)),
        out_shape=jax.ShapeDtypeStruct((m, n), F32),
        compiler_params=_params("arbitrary"),
        name="ada",
    )(c, w, b)


def _inproj_kernel(x_ref, mod_ref, n1_ref, lbraw_ref, bf_ref, wa_ref, wff_ref, wmg_ref,
                   q_ref, g_ref, k_ref, v_ref, gate_ref, fq_ref, fk_ref, fv_ref,
                   lf8_ref, lf128_ref, mg_ref, *, fox_transposed):
    g_, r_, d = x_ref.shape
    m = g_ * r_
    x = x_ref[...]
    mod = mod_ref[...]
    sh1 = mod[:, :, 0:d]
    sc1 = mod[:, :, d:2 * d]
    h = _rms(x, n1_ref[...]) * (1.0 + sc1) + sh1
    hb = h.reshape(m, d).astype(BF16)

    raw = lbraw_ref[...]
    e = jnp.exp(raw - jnp.max(raw, axis=0, keepdims=True))
    lb = e[0:1, :] / jnp.sum(e, axis=0, keepdims=True)

    w = HG_WIDTH

    def proj(c):
        return lax.dot_general(hb, wa_ref[c * w:(c + 1) * w, :], _NT, preferred_element_type=F32)

    def proj_t(c):
        return lax.dot_general(wa_ref[c * w:(c + 1) * w, :], hb, _NT, preferred_element_type=F32)

    def merge_gate(c):
        mgc = lax.dot_general(hb, wmg_ref[c * w:(c + 1) * w, :], _NT, preferred_element_type=F32)
        mg_ref[:, c * w:(c + 1) * w] = _sigmoid_pair(mgc)[0].astype(BF16)

    def fox(c, ref, scale=None):
        y = proj_t(c) if fox_transposed else proj(c)
        y = y if scale is None else y * scale
        if fox_transposed:
            ref[0] = y.astype(ref.dtype)
        else:
            ref[...] = y.astype(ref.dtype)

    assert wmg_ref.shape[0] == 4 * w
    merge_gate(0)
    q_ref[...] = proj(0) * (HG_DK ** -0.5)
    merge_gate(1)
    v_ref[...] = proj(2).astype(BF16)
    merge_gate(2)
    fox(4, fq_ref, (LOG2E if fox_transposed else 1.0) * FOX_DH ** -0.5)
    merge_gate(3)
    fox(5, fk_ref)
    s_pos, s_neg = _sigmoid_pair(proj(1))
    g_ref[...] = jnp.log(lb + (1.0 - lb) * s_pos)
    k_ref[...] = (1.0 - lb) * s_neg
    fox(6, fv_ref)
    hg = proj(3)
    gate_ref[...] = (hg * _sigmoid_pair(hg)[0]).astype(BF16)
    z = lax.dot_general(hb, wff_ref[...], _NT, preferred_element_type=F32) + bf_ref[...]
    lf = jnp.minimum(z, 0.0) - jnp.log1p(jnp.exp(-jnp.abs(z)))
    lf128_ref[...] = lf
    lf8_ref[...] = lf[:, 0:FOX_HEADS]


def _inproj(x3, mod3, n1, lbraw, bf128, wa, wff, wmg, groups, rows, fox_transposed):
    assert groups == 1 or not fox_transposed
    nb, r_all, d = x3.shape
    nt = r_all // rows
    steps = (nb // groups) * nt
    t = nb * r_all
    m = groups * rows
    xmap = lambda s: (s // nt, s % nt, 0)
    mmap = lambda s: (s // nt, 0, 0)
    omap = lambda s: (s, 0)

    def out(width, dtype):
        return jax.ShapeDtypeStruct((t, width), dtype), pl.BlockSpec((m, width), omap)

    def fox(dtype):
        if not fox_transposed:
            return out(FOX_WIDTH, dtype)
        return (jax.ShapeDtypeStruct((nb, FOX_WIDTH, r_all), dtype),
                pl.BlockSpec((1, FOX_WIDTH, m), lambda s: (s // nt, 0, s % nt)))

    outs = [out(HG_WIDTH, F32), out(HG_WIDTH, F32), out(HG_WIDTH, F32), out(HG_WIDTH, BF16),
            out(HG_WIDTH, BF16), fox(BF16), fox(F32), fox(F32),
            out(FOX_HEADS, F32), out(LANES, F32), out(wmg.shape[0], BF16)]
    return pl.pallas_call(
        functools.partial(_inproj_kernel, fox_transposed=fox_transposed),
        grid=(steps,),
        in_specs=[pl.BlockSpec((groups, rows, d), xmap),
                  pl.BlockSpec((groups, 1, mod3.shape[2]), mmap),
                  _resident(n1.shape), _resident(lbraw.shape), _resident(bf128.shape),
                  _resident(wa.shape), _resident(wff.shape), _resident(wmg.shape)],
        out_specs=[o[1] for o in outs],
        out_shape=[o[0] for o in outs],
        compiler_params=_params("arbitrary"),
        name="inproj",
    )(x3, mod3, n1, lbraw, bf128, wa, wff, wmg)


def _decay_kernel(lf_ref, place_ref, bk_ref, bqt_ref):
    seq = lf_ref.shape[1]
    lane = lax.broadcasted_iota(jnp.int32, (LANES, LANES), 1)
    slot = lane & (BIAS_SLOTS - 1)
    used = lane < FOX_HEADS * BIAS_SLOTS
    ones_k = jnp.where(jnp.logical_and(used, slot < N_PIECES), 1.0, 0.0)
    ones_q = jnp.where(jnp.logical_and(used, jnp.logical_and(slot >= N_PIECES, slot < 2 * N_PIECES)), 1.0, 0.0)
    carry = jnp.zeros((1, LANES), F32)
    for j in range(seq // LANES):
        sl = slice(j * LANES, (j + 1) * LANES)
        cs = _cumsum_rows(lf_ref[0, sl, :], LANES) + carry
        carry = cs[LANES - 1:LANES, :]
        pieces = _split3(cs * LOG2E)

        def place(base):
            return sum(jnp.dot(pieces[j], place_ref[base + j], preferred_element_type=F32)
                       for j in range(N_PIECES))

        bk_ref[0, sl, :] = (ones_k - place(N_PIECES)).astype(BF16)
        bqt_ref[0, :, sl] = (ones_q + place(0)).T.astype(BF16)


def _decay(lf3, place):
    b, seq, _ = lf3.shape
    return pl.pallas_call(
        _decay_kernel,
        grid=(b,),
        in_specs=[pl.BlockSpec((1, seq, LANES), lambda i: (i, 0, 0)), _resident(place.shape)],
        out_specs=[pl.BlockSpec((1, seq, LANES), lambda i: (i, 0, 0)),
                   pl.BlockSpec((1, LANES, seq), lambda i: (i, 0, 0))],
        out_shape=[jax.ShapeDtypeStruct((b, seq, LANES), BF16),
                   jax.ShapeDtypeStruct((b, LANES, seq), BF16)],
        compiler_params=_params("arbitrary"),
        name="decay",
    )(lf3, place)


def _hgrn_block(q, k, g, v, chunk, state_in, state_out):
    rows = q.shape[0]
    n_chunks = rows // chunk
    shift = chunk.bit_length() - 1
    b = _cumsum_rows(g, chunk)
    qt = (q * jnp.exp(b)).astype(BF16)
    kt = (k * jnp.exp(-b)).astype(BF16)
    b3 = b.reshape(n_chunks, chunk, HG_DK)
    bl3 = b3[:, chunk - 1:chunk, :]
    kd = (k.reshape(n_chunks, chunk, HG_DK) * jnp.exp(bl3 - b3)).reshape(rows, HG_DK).astype(BF16)

    a = lax.dot_general(qt, kt, _NT, preferred_element_type=F32)
    v_t = v.astype(F32).T.astype(BF16)
    col_chunk = lax.broadcasted_iota(jnp.int32, v_t.shape, 1) >> shift
    v_blocks = jnp.concatenate([jnp.where(col_chunk == c, v_t, jnp.zeros_like(v_t)) for c in range(n_chunks)],
                               axis=0)
    incr = jnp.dot(v_blocks, kd, preferred_element_type=F32)
    ri = lax.broadcasted_iota(jnp.int32, (rows, rows), 0)
    ci = lax.broadcasted_iota(jnp.int32, (rows, rows), 1)
    keep = (ri - ci).astype(jnp.uint32) <= (ri & (chunk - 1)).astype(jnp.uint32)
    o_intra = jnp.dot(jnp.where(keep, a, 0.0).astype(BF16), v, preferred_element_type=F32)

    entering = []
    st = None
    for c in range(n_chunks):
        st = state_in(c, st)
        entering.append(st.astype(BF16))
        st = st * jnp.exp(bl3[c]) + incr[c * HG_DV:(c + 1) * HG_DV, :]
        state_out(c, st)
    o_inter = [lax.dot_general(qt[c * chunk:(c + 1) * chunk, :], entering[c], _NT, preferred_element_type=F32)
               for c in range(n_chunks)]
    return o_intra + jnp.concatenate(o_inter, axis=0)


def _hgrn_prompt_kernel(q_ref, k_ref, g_ref, v_ref, o_ref, s_ref, *, chunk):
    seq = q_ref.shape[0]

    def body(j, st0):
        rs = pl.ds(pl.multiple_of(j * LANES, LANES), LANES)
        last = []

        def state_in(c, prev):
            return st0 if prev is None else prev

        def state_out(c, st):
            if c == LANES // chunk - 1:
                last.append(st)

        o_ref[rs, :] = _hgrn_block(q_ref[rs, :], k_ref[rs, :], g_ref[rs, :], v_ref[rs, :],
                                   chunk, state_in, state_out)
        return last[0]

    st = lax.fori_loop(0, seq // LANES, body, jnp.zeros((HG_DV, HG_DK), F32), unroll=8)
    s_ref[0, 0] = st.T


def _hgrn_prompt(q, k, g, v, batch, seq, chunk):
    spec = pl.BlockSpec((seq, HG_DK), lambda b, h: (b, h))
    return pl.pallas_call(
        functools.partial(_hgrn_prompt_kernel, chunk=chunk),
        grid=(batch, HG_HEADS),
        in_specs=[spec, spec, spec, spec],
        out_specs=[spec, pl.BlockSpec((1, 1, HG_DK, HG_DV), lambda b, h: (b, h, 0, 0))],
        out_shape=[jax.ShapeDtypeStruct((batch * seq, HG_WIDTH), F32),
                   jax.ShapeDtypeStruct((batch, HG_HEADS, HG_DK, HG_DV), F32)],
        compiler_params=_params("arbitrary", "arbitrary"),
        name="hgrn_prompt",
    )(q, k, g, v)


def _hgrn_sample_kernel(q_ref, k_ref, g_ref, v_ref, s0_ref, o_ref, s_ref, *, chunk):
    def state_in(c, prev):
        return s0_ref[c, 0].T

    def state_out(c, st):
        s_ref[c, 0] = st.T

    o_ref[...] = _hgrn_block(q_ref[...], k_ref[...], g_ref[...], v_ref[...], chunk, state_in, state_out)


def _hgrn_sample(q, k, g, v, s0, chunk):
    t = q.shape[0]
    per = LANES // chunk
    spec = pl.BlockSpec((LANES, HG_DK), lambda i, h: (i, h))
    sspec = pl.BlockSpec((per, 1, HG_DK, HG_DV), lambda i, h: (i, h, 0, 0))
    return pl.pallas_call(
        functools.partial(_hgrn_sample_kernel, chunk=chunk),
        grid=(t // LANES, HG_HEADS),
        in_specs=[spec, spec, spec, spec, sspec],
        out_specs=[spec, sspec],
        out_shape=[jax.ShapeDtypeStruct((t, HG_WIDTH), F32),
                   jax.ShapeDtypeStruct(s0.shape, F32)],
        compiler_params=_params("arbitrary", "arbitrary"),
        name="hgrn_sample",
    )(q, k, g, v, s0)


ONES_ROWS = 16


def _fox_prompt_kernel(uq_ref, uk_ref, um_ref, qt_ref, kt_ref, vt_ref, bk_ref, bqt_ref, o_ref,
                       kn_ref, va_ref, rhs_ref, mask_ref, s_ref, p_ref, *, tq, tk, n_units, limits):
    seq = kt_ref.shape[1]
    n_heads = LANES // FOX_DH
    pair = pl.program_id(1)

    for c in range(seq // LANES):
        cs = slice(c * LANES, (c + 1) * LANES)
        kn_ref[cs, 0:LANES] = kt_ref[:, cs].T.astype(BF16)
    kn_ref[:, LANES:2 * LANES] = bk_ref[...]
    for c in range(seq // tk):
        for e in range(n_heads):
            va_ref[c, e, 0:FOX_DH, :] = vt_ref[e * FOX_DH:(e + 1) * FOX_DH, c * tk:(c + 1) * tk].astype(BF16)
            va_ref[c, e, FOX_DH:FOX_DH + ONES_ROWS, :] = jnp.ones((ONES_ROWS, tk), BF16)
    row = lax.broadcasted_iota(jnp.int32, (LANES, tq), 0)
    for i in range(seq // tq):
        qs = slice(i * tq, (i + 1) * tq)
        qt = qt_ref[:, qs]
        bqt = bqt_ref[:, qs]
        for e in range(n_heads):
            rhs_ref[i, e, 0:LANES, :] = jnp.where(row // FOX_DH == e, qt, jnp.zeros_like(qt))
            rhs_ref[i, e, LANES:2 * LANES, :] = jnp.where(row // BIAS_SLOTS == pair * n_heads + e, bqt,
                                                          jnp.zeros_like(bqt))
    p_ref[1] = jnp.zeros(p_ref.shape[1:], BF16)

    key_minus_query = (lax.broadcasted_iota(jnp.int32, (tk, tq), 0)
                       - lax.broadcasted_iota(jnp.int32, (tk, tq), 1))
    for n, limit in enumerate(limits):
        mask_ref[n] = jnp.where(key_minus_query <= limit, 0.0, NEG_INF)

    def scores(u, slot):
        lhs = kn_ref[pl.ds(pl.multiple_of(uk_ref[u] * tk, tk), tk), :]
        for e in range(n_heads):
            s_ref[slot, e] = jnp.dot(lhs, rhs_ref[uq_ref[u], e], preferred_element_type=F32)

    def numerators(u, slot, ms):
        first = uk_ref[u] == 0
        mask = mask_ref[um_ref[u]]
        new_ms, alphas = [], []
        for e in range(n_heads):
            m = jnp.where(first, NEG_INF, ms[e])
            s = s_ref[slot, e] + mask
            m_new = jnp.maximum(m, jnp.max(s, axis=0, keepdims=True))
            p_ref[slot, e] = jnp.exp2(s - m_new).astype(BF16)
            new_ms.append(m_new)
            alphas.append(jnp.exp2(m - m_new))
        return new_ms, alphas

    def values(u, slot, alphas, ls, accs):
        new_ls, new_accs = [], []
        for e in range(n_heads):
            pv = jnp.dot(va_ref[uk_ref[u], e], p_ref[slot, e], preferred_element_type=F32)
            new_ls.append(alphas[e] * ls[e] + pv[FOX_DH:FOX_DH + 1, :])
            new_accs.append(alphas[e] * accs[e] + pv[0:FOX_DH, :])
        o_t = jnp.concatenate([new_accs[e] / new_ls[e] for e in range(n_heads)], axis=0)
        o_ref[pl.ds(pl.multiple_of(uq_ref[u] * tq, tq), tq), :] = o_t.T.astype(BF16)
        return new_ls, new_accs

    def step(u, slot, carry):
        ms, alphas, ls, accs = carry
        new_ls, new_accs = values(u - 1, 1 - slot, alphas, ls, accs)
        scores(u + 1, 1 - slot)
        new_ms, new_alphas = numerators(u, slot, ms)
        return new_ms, new_alphas, new_ls, new_accs

    def body(k, carry):
        carry = step(2 * k + 1, 0, carry)
        return step(2 * k + 2, 1, carry)

    rows = lambda n, v: [jnp.full((n, tq), v, F32) for _ in range(n_heads)]
    scores(1, 0)
    carry = (rows(1, NEG_INF), rows(1, 1.0), rows(1, 1.0), rows(FOX_DH, 0.0))
    _, alphas, ls, accs = lax.fori_loop(0, n_units // 2, body, carry)
    values(n_units, 1, alphas, ls, accs)


def _fox_prompt(fqt, fkt, fvt, bk, bqt, batch, seq, tq, tk):
    assert tk % tq == 0 and seq % tk == 0
    nq = seq // tq
    pairs = FOX_WIDTH // LANES
    n_heads = LANES // FOX_DH
    units = [(i, j) for i in range(nq) for j in range((i * tq) // tk + 1)]
    n_units = len(units)
    assert n_units % 2 == 0
    padded = [units[0]] + units + [units[-1]]
    unit_limits = [min(i * tq - j * tk, tk - 1) for i, j in padded]
    limits = tuple(sorted(set(unit_limits)))
    table = lambda vals: jnp.asarray(vals, jnp.int32)
    uq, uk = table([u[0] for u in padded]), table([u[1] for u in padded])
    um = table([limits.index(v) for v in unit_limits])
    head_pair = lambda b, p, *_: (b, p, 0)
    per_batch = lambda b, p, *_: (b, 0, 0)
    grid_spec = pltpu.PrefetchScalarGridSpec(
        num_scalar_prefetch=3,
        grid=(batch, pairs),
        in_specs=[pl.BlockSpec((None, LANES, seq), head_pair),
                  pl.BlockSpec((None, LANES, seq), head_pair),
                  pl.BlockSpec((None, LANES, seq), head_pair),
                  pl.BlockSpec((None, seq, LANES), per_batch),
                  pl.BlockSpec((None, LANES, seq), per_batch)],
        out_specs=pl.BlockSpec((seq, LANES), lambda b, p, *_: (b, p)),
        scratch_shapes=[pltpu.VMEM((seq, 2 * LANES), BF16),
                        pltpu.VMEM((seq // tk, n_heads, FOX_DH + ONES_ROWS, tk), BF16),
                        pltpu.VMEM((nq, n_heads, 2 * LANES, tq), BF16),
                        pltpu.VMEM((len(limits), tk, tq), F32),
                        pltpu.VMEM((2, n_heads, tk, tq), F32),
                        pltpu.VMEM((2, n_heads, tk, tq), BF16)],
    )
    return pl.pallas_call(
        functools.partial(_fox_prompt_kernel, tq=tq, tk=tk, n_units=n_units, limits=limits),
        grid_spec=grid_spec,
        out_shape=jax.ShapeDtypeStruct((batch * seq, FOX_WIDTH), BF16),
        compiler_params=_params("arbitrary", "arbitrary"),
        name="fox_prompt",
    )(uq, uk, um, fqt, fkt, fvt, bk, bqt)


def _fox_sample_kernel(pt_ref, q_ref, kn_ref, vn_ref, lfn_ref, sfx_ref, *refs, n_pages):
    del pt_ref
    k_refs = refs[:n_pages]
    v_refs = refs[n_pages:2 * n_pages]
    lf_refs = refs[2 * n_pages:3 * n_pages]
    o_ref = refs[3 * n_pages]
    steps = q_ref.shape[1]
    rows = FOX_HEADS * steps
    pad = PAGE_SIZE - steps

    def per_head_rows(x8):
        return jnp.broadcast_to(x8[:, None, :], (FOX_HEADS, steps, x8.shape[1])).reshape(rows, x8.shape[1])

    q = q_ref[0].astype(F32)
    q_rows = jnp.concatenate([q] * FOX_HEADS, axis=0)
    row_head = lax.broadcasted_iota(jnp.int32, (rows, FOX_WIDTH), 0) // steps
    lane_head = lax.broadcasted_iota(jnp.int32, (rows, FOX_WIDTH), 1) // FOX_DH
    head_mask = row_head == lane_head
    qbd = jnp.where(head_mask, q_rows, 0.0).astype(BF16)

    pn = _cumsum_rows(lfn_ref[0], steps)
    pn_rows = jnp.concatenate([pn] * FOX_HEADS, axis=0)
    r_h = lax.broadcasted_iota(jnp.int32, (rows, LANES), 0) // steps
    r_i = lax.broadcasted_iota(jnp.int32, (rows, LANES), 0) % steps
    c_l = lax.broadcasted_iota(jnp.int32, (rows, LANES), 1)
    pcol = jnp.sum(jnp.where(c_l == r_h, pn_rows, 0.0), axis=1, keepdims=True)
    pn_t = jnp.concatenate([pn, jnp.zeros((pad, LANES), F32)], axis=0).T[0:FOX_HEADS, :]

    lf_all = jnp.concatenate([lf_refs[j][0] for j in range(n_pages)], axis=0)
    sfx = _dot_f32_lhs(lf_all, sfx_ref[...])
    carry = jnp.zeros((FOX_HEADS, PAGE_SIZE), F32)
    page_bias = [None] * n_pages
    for j in reversed(range(n_pages)):
        blk = sfx[j * FOX_HEADS:(j + 1) * FOX_HEADS, :]
        page_bias[j] = blk[:, 0:PAGE_SIZE] + carry
        carry = carry + blk[:, PAGE_SIZE:2 * PAGE_SIZE]

    s_tiles = []
    for j in range(n_pages):
        s = jnp.dot(qbd, k_refs[j][0].astype(BF16), preferred_element_type=F32)
        s_tiles.append(s + (per_head_rows(page_bias[j]) + pcol))
    kn = jnp.concatenate([kn_ref[0], jnp.zeros((pad, FOX_WIDTH), F32)], axis=0).astype(BF16)
    s_new = lax.dot_general(qbd, kn, _NT, preferred_element_type=F32) + (pcol - per_head_rows(pn_t))
    s_tiles.append(jnp.where(c_l <= r_i, s_new, NEG_INF))

    m_el = s_tiles[0]
    for s in s_tiles[1:]:
        m_el = jnp.maximum(m_el, s)
    m = jnp.max(m_el, axis=1, keepdims=True)
    p_tiles = [jnp.exp(s - m) for s in s_tiles]
    l_el = p_tiles[0]
    for p in p_tiles[1:]:
        l_el = l_el + p
    l = jnp.sum(l_el, axis=1, keepdims=True)

    vn = jnp.concatenate([vn_ref[0], jnp.zeros((pad, FOX_WIDTH), F32)], axis=0).astype(BF16)
    o = jnp.dot(p_tiles[n_pages].astype(BF16), vn, preferred_element_type=F32)
    for j in range(n_pages):
        o = o + lax.dot_general(p_tiles[j].astype(BF16), v_refs[j][0].astype(BF16), _NT,
                                preferred_element_type=F32)
    o = jnp.where(head_mask, o / l, 0.0)
    o_ref[0] = jnp.sum(o.reshape(FOX_HEADS, steps, FOX_WIDTH), axis=0).astype(BF16)


def _fox_sample(page_table, fq3, kn3, vn3, lfn3, sfx_mat, cache_kt, cache_vt, cache_lft):
    nseq, n_pages = page_table.shape
    steps = fq3.shape[1]
    seq_spec = lambda w: pl.BlockSpec((1, steps, w), lambda n, pt: (n, 0, 0))

    def page_spec(shape, j):
        return pl.BlockSpec((1,) + shape, lambda n, pt: (pt[n * n_pages + j], 0, 0))

    in_specs = [seq_spec(FOX_WIDTH), seq_spec(FOX_WIDTH), seq_spec(FOX_WIDTH), seq_spec(LANES),
                pl.BlockSpec(sfx_mat.shape, lambda n, pt: (0, 0))]
    in_specs += [page_spec((FOX_WIDTH, PAGE_SIZE), j) for j in range(n_pages)]
    in_specs += [page_spec((FOX_WIDTH, PAGE_SIZE), j) for j in range(n_pages)]
    in_specs += [page_spec((FOX_HEADS, PAGE_SIZE), j) for j in range(n_pages)]
    grid_spec = pltpu.PrefetchScalarGridSpec(
        num_scalar_prefetch=1,
        grid=(nseq,),
        in_specs=in_specs,
        out_specs=pl.BlockSpec((1, steps, FOX_WIDTH), lambda n, pt: (n, 0, 0)),
    )
    return pl.pallas_call(
        functools.partial(_fox_sample_kernel, n_pages=n_pages),
        grid_spec=grid_spec,
        out_shape=jax.ShapeDtypeStruct((nseq, steps, FOX_WIDTH), BF16),
        compiler_params=_params("arbitrary"),
        name="fox_sample",
    )(page_table.reshape(-1), fq3, kn3, vn3, lfn3, sfx_mat,
      *([cache_kt] * n_pages), *([cache_vt] * n_pages), *([cache_lft] * n_pages))


def _post_kernel(x_ref, mod_ref, oh_ref, gate_ref, of_ref, mg_ref, onorm_ref, n1post_ref, n2pre_ref,
                 n2post_ref, wbh_ref, wbf_ref, wout_ref, wup_ref, wdn_ref, y_ref, *, ff_chunk):
    g_, r_, d = x_ref.shape
    m = g_ * r_
    mod = mod_ref[...]
    gt1 = mod[:, :, 2 * d:3 * d]
    sh2 = mod[:, :, 3 * d:4 * d]
    sc2 = mod[:, :, 4 * d:5 * d]
    gt2 = mod[:, :, 5 * d:6 * d]

    oh = oh_ref[...]
    parts = [_rms(oh[:, h * HG_DV:(h + 1) * HG_DV], onorm_ref[...]) for h in range(HG_HEADS)]
    ohn = (jnp.concatenate(parts, axis=1) * gate_ref[...].astype(F32)).astype(BF16)
    br_h = jnp.dot(ohn, wbh_ref[...], preferred_element_type=F32)
    br_f = jnp.dot(of_ref[...], wbf_ref[...], preferred_element_type=F32)
    mg = mg_ref[...].astype(F32)
    z = (mg[:, 0:d] * br_h + mg[:, d:2 * d] * br_f).astype(BF16)
    y = jnp.dot(z, wout_ref[...], preferred_element_type=F32).reshape(g_, r_, d)
    x1 = x_ref[...] + gt1 * _rms(y, n1post_ref[...])
    h2 = (_rms(x1, n2pre_ref[...]) * (1.0 + sc2) + sh2).reshape(m, d).astype(BF16)
    u = jnp.zeros((m, d), F32)
    for c in range(wup_ref.shape[1] // ff_chunk):
        cs = slice(c * ff_chunk, (c + 1) * ff_chunk)
        a = jnp.maximum(jnp.dot(h2, wup_ref[:, cs], preferred_element_type=F32), 0.0)
        u = u + jnp.dot((a * a).astype(BF16), wdn_ref[cs, :], preferred_element_type=F32)
    y_ref[...] = x1 + gt2 * _rms(u.reshape(g_, r_, d), n2post_ref[...])


def _post(x3, mod3, oh, gate, of, mg, onorm, n1post, n2pre, n2post, wbh, wbf, wout, wup, wdn,
          groups, rows, ff_chunk=1024):
    nb, r_all, d = x3.shape
    nt = r_all // rows
    steps = (nb // groups) * nt
    m = groups * rows
    xmap = lambda s: (s // nt, s % nt, 0)
    mmap = lambda s: (s // nt, 0, 0)
    tmap = lambda s: (s, 0)
    tok = lambda a: pl.BlockSpec((m, a.shape[1]), tmap)
    consts = [onorm, n1post, n2pre, n2post, wbh, wbf, wout, wup, wdn]
    return pl.pallas_call(
        functools.partial(_post_kernel, ff_chunk=ff_chunk),
        grid=(steps,),
        in_specs=[pl.BlockSpec((groups, rows, d), xmap),
                  pl.BlockSpec((groups, 1, mod3.shape[2]), mmap),
                  tok(oh), tok(gate), tok(of), tok(mg)] + [_resident(c.shape) for c in consts],
        out_specs=pl.BlockSpec((groups, rows, d), xmap),
        out_shape=jax.ShapeDtypeStruct(x3.shape, F32),
        compiler_params=_params("arbitrary"),
        name="post",
    )(x3, mod3, oh, gate, of, mg, *consts)


def _bias_place_matrices():
    shape = (2 * N_PIECES, LANES, LANES)
    s = lax.broadcasted_iota(jnp.int32, shape, 0)
    r = lax.broadcasted_iota(jnp.int32, shape, 1)
    c = lax.broadcasted_iota(jnp.int32, shape, 2)
    return jnp.logical_and(r < FOX_HEADS, c == BIAS_SLOTS * r + s).astype(BF16)


def _page_suffix_matrix():
    r = lax.broadcasted_iota(jnp.int32, (PAGE_SIZE, 2 * PAGE_SIZE), 0)
    c = lax.broadcasted_iota(jnp.int32, (PAGE_SIZE, 2 * PAGE_SIZE), 1)
    return jnp.logical_or(r > c, c >= PAGE_SIZE).astype(BF16)


def kernel(x_prompt, x_sample, c_prompt, c_sample, cache_k, cache_v, cache_logf, state_hgrn, page_table,
           ada_w, ada_b, norm_mix_pre, norm_mix_post, norm_mlp_pre, norm_mlp_post, w_in,
           hgrn_lower_bounds, hgrn_onorm, fox_b_f, w_br_h, w_br_f, w_out, w_mlp_up, w_mlp_down):
    batch, seq, d = x_prompt.shape
    nseq, steps, _ = x_sample.shape
    layer = 0
    n_phys = cache_k.shape[1]

    w_t = jnp.transpose(w_in[layer]).astype(BF16)
    n_a = 4 * HG_WIDTH + 3 * FOX_WIDTH
    wa = w_t[:n_a]
    wff = jnp.pad(w_t[n_a:n_a + FOX_HEADS], ((0, LANES - FOX_HEADS), (0, 0)))
    wmg = w_t[n_a + FOX_HEADS:]
    bf128 = jnp.pad(fox_b_f[layer], (0, LANES - FOX_HEADS)).reshape(1, LANES)
    lbraw = hgrn_lower_bounds
    vec3 = lambda v: v.reshape(1, 1, -1)
    wbh, wbf = w_br_h[layer].astype(BF16), w_br_f[layer].astype(BF16)
    wout = w_out[layer].astype(BF16)
    wup, wdn = w_mlp_up[layer].astype(BF16), w_mlp_down[layer].astype(BF16)
    onorm = hgrn_onorm[layer].reshape(1, HG_DV)

    n_c = batch + nseq
    c_pad = -n_c % (2 * SUBLANES)
    c_all = jnp.concatenate([c_prompt, c_sample, jnp.zeros((c_pad, d), F32)], axis=0)
    mod = _ada(c_all, ada_w[layer], ada_b[layer].reshape(1, -1))
    mod_p = mod[:batch].reshape(batch, 1, -1)
    mod_s = mod[batch:n_c].reshape(nseq, 1, -1)

    tm = 512
    sgroups = tm // steps
    proj_args = (vec3(norm_mix_pre[layer]), lbraw, bf128, wa, wff, wmg)
    (q_p, g_p, k_p, v_p, gate_p, fq_p, fk_p, fv_p, lf8_p, lf128_p, mg_p) = _inproj(
        x_prompt, mod_p, *proj_args, groups=1, rows=tm, fox_transposed=True)
    (q_s, g_s, k_s, v_s, gate_s, fq_s, fk_s, fv_s, lf8_s, lf128_s, mg_s) = _inproj(
        x_sample, mod_s, *proj_args, groups=sgroups, rows=steps, fox_transposed=False)

    chunk_p = HG_CHUNK if seq % HG_CHUNK == 0 else seq
    oh_p, s_p = _hgrn_prompt(q_p, k_p, g_p, v_p, batch, seq, chunk_p)
    oh_s, s_s = _hgrn_sample(q_s, k_s, g_s, v_s, state_hgrn[layer], steps)

    tq, tk = 256, 512
    bias_k, bias_qt = _decay(lf128_p.reshape(batch, seq, LANES), _bias_place_matrices())
    of_p = _fox_prompt(fq_p, fk_p, fv_p, bias_k, bias_qt, batch, seq, tq, tk)

    ckt = jnp.transpose(cache_k[layer], (0, 2, 3, 1)).reshape(n_phys, FOX_WIDTH, PAGE_SIZE)
    cvt = jnp.transpose(cache_v[layer], (0, 2, 3, 1)).reshape(n_phys, FOX_WIDTH, PAGE_SIZE)
    clt = jnp.transpose(cache_logf[layer], (0, 2, 1))
    of_s = _fox_sample(page_table, fq_s.reshape(nseq, steps, FOX_WIDTH), fk_s.reshape(nseq, steps, FOX_WIDTH),
                       fv_s.reshape(nseq, steps, FOX_WIDTH), lf128_s.reshape(nseq, steps, LANES),
                       _page_suffix_matrix(), ckt, cvt, clt).reshape(nseq * steps, FOX_WIDTH)

    post_args = (onorm, vec3(norm_mix_post[layer]), vec3(norm_mlp_pre[layer]), vec3(norm_mlp_post[layer]),
                 wbh, wbf, wout, wup, wdn)
    y_p = _post(x_prompt, mod_p, oh_p, gate_p, of_p, mg_p, *post_args, groups=1, rows=tm)
    y_s = _post(x_sample, mod_s, oh_s, gate_s, of_s, mg_s, *post_args, groups=sgroups, rows=steps)

    k_prompt = fk_p.reshape(1, batch, FOX_HEADS, FOX_DH, seq).transpose(0, 1, 4, 2, 3)
    v_prompt = fv_p.reshape(1, batch, FOX_HEADS, FOX_DH, seq).transpose(0, 1, 4, 2, 3)
    logf_prompt = lf8_p.reshape(1, batch, seq, FOX_HEADS)
    k_sample = fk_s.reshape(1, nseq, steps, FOX_HEADS, FOX_DH)
    v_sample = fv_s.reshape(1, nseq, steps, FOX_HEADS, FOX_DH)
    logf_sample = lf8_s.reshape(1, nseq, steps, FOX_HEADS)
    return (y_p, y_s, k_prompt, v_prompt, logf_prompt, s_p[None], k_sample, v_sample, logf_sample, s_s[None])
```

```python
import functools

import jax
import jax.numpy as jnp
from jax import lax
from jax.experimental import pallas as pl
from jax.experimental.pallas import tpu as pltpu

F32 = jnp.float32
BF16 = jnp.bfloat16

LANES = 128
SUBLANES = 8
VMEM_LIMIT_BYTES = 56 * 1024 * 1024

D_MODEL = 1024
HG_HEADS = 4
HG_DK = 128
HG_DV = 128
HG_CHUNK = 32
FOX_HEADS = 8
FOX_DH = 64
FOX_WIDTH = FOX_HEADS * FOX_DH
HG_WIDTH = HG_HEADS * HG_DV
PAGE_SIZE = 128
N_MOD = 6
RMS_EPS = 1e-6
NEG_INF = float("-inf")
LOG2E = 1.4426950408889634
N_PIECES = 3
BIAS_SLOTS = 8

_NT = (((1,), (1,)), ((), ()))
_TN = (((0,), (0,)), ((), ()))


def _params(*sem):
    return pltpu.CompilerParams(dimension_semantics=sem, vmem_limit_bytes=VMEM_LIMIT_BYTES)


def _resident(shape):
    nd = len(shape)
    return pl.BlockSpec(shape, lambda *_: (0,) * nd, pipeline_mode=pl.Buffered(1))


def _sigmoid_pair(x):
    t = jnp.exp(-jnp.abs(x))
    r = 1.0 / (1.0 + t)
    tr = t * r
    pos = x >= 0
    return jnp.where(pos, r, tr), jnp.where(pos, tr, r)


def _split3(x):
    hi = x.astype(BF16)
    r1 = x - hi.astype(F32)
    mid = r1.astype(BF16)
    lo = (r1 - mid.astype(F32)).astype(BF16)
    return hi, mid, lo


def _dot_f32_lhs(x, w):
    hi, mid, lo = _split3(x)
    d = lambda p: jnp.dot(p, w, preferred_element_type=F32)
    return (d(lo) + d(mid)) + d(hi)


def _cumsum_rows(x, period):
    row = lax.broadcasted_iota(jnp.int32, x.shape, 0) & (period - 1)
    s = 1
    while s < period:
        x = x + jnp.where(row >= s, pltpu.roll(x, s, axis=0), 0.0)
        s *= 2
    return x


def _rms(x, w):
    return x * lax.rsqrt(jnp.mean(x * x, axis=-1, keepdims=True) + RMS_EPS) * w


def _ada_kernel(c_ref, w_ref, b_ref, o_ref):
    c = c_ref[...]
    s, _ = _sigmoid_pair(c)
    a = (c * s).astype(BF16)
    o_ref[...] = jnp.dot(a, w_ref[...].astype(BF16), preferred_element_type=F32) + b_ref[...]


def _ada(c, w, b, tn=1536):
    m, d = c.shape
    n = w.shape[1]
    return pl.pallas_call(
        _ada_kernel,
        grid=(n // tn,),
        in_specs=[pl.BlockSpec((m, d), lambda j: (0, 0)),
                  pl.BlockSpec((d, tn), lambda j: (0, j)),
                  pl.BlockSpec((1, tn), lambda j: (0, j))],
        out_specs=pl.BlockSpec((m, tn), lambda j: (0, j)),
        out_shape=jax.ShapeDtypeStruct((m, n), F32),
        compiler_params=_params("arbitrary"),
        name="ada",
    )(c, w, b)


def _inproj_kernel(x_ref, mod_ref, n1_ref, lbraw_ref, bf_ref, wa_ref, wff_ref, wmg_ref,
                   q_ref, g_ref, k_ref, v_ref, gate_ref, fq_ref, fk_ref, fv_ref,
                   lf8_ref, lf128_ref, mg_ref, *, fox_transposed):
    g_, r_, d = x_ref.shape
    m = g_ * r_
    x = x_ref[...]
    mod = mod_ref[...]
    sh1 = mod[:, :, 0:d]
    sc1 = mod[:, :, d:2 * d]
    h = _rms(x, n1_ref[...] * (1.0 + sc1)) + sh1
    hb = h.reshape(m, d).astype(BF16)

    raw = lbraw_ref[...]
    e = jnp.exp(raw - jnp.max(raw, axis=0, keepdims=True))
    lb = e[0:1, :] / jnp.sum(e, axis=0, keepdims=True)

    w = HG_WIDTH

    def proj(c):
        return lax.dot_general(hb, wa_ref[c * w:(c + 1) * w, :], _NT, preferred_element_type=F32)

    def proj_t(c):
        return lax.dot_general(wa_ref[c * w:(c + 1) * w, :], hb, _NT, preferred_element_type=F32)

    def merge_gate(c):
        mgc = lax.dot_general(hb, wmg_ref[c * w:(c + 1) * w, :], _NT, preferred_element_type=F32)
        mg_ref[:, c * w:(c + 1) * w] = _sigmoid_pair(mgc)[0].astype(BF16)

    def fox(c, ref, scale=None):
        y = proj_t(c) if fox_transposed else proj(c)
        y = y if scale is None else y * scale
        if fox_transposed:
            ref[0] = y.astype(ref.dtype)
        else:
            ref[...] = y.astype(ref.dtype)

    assert wmg_ref.shape[0] == 4 * w
    merge_gate(0)
    q_ref[...] = proj(0) * (HG_DK ** -0.5)
    merge_gate(1)
    v_ref[...] = proj(2).astype(BF16)
    merge_gate(2)
    fox(4, fq_ref, (LOG2E if fox_transposed else 1.0) * FOX_DH ** -0.5)
    merge_gate(3)
    fox(5, fk_ref)
    s_pos, s_neg = _sigmoid_pair(proj(1))
    g_ref[...] = jnp.log(lb + (1.0 - lb) * s_pos)
    k_ref[...] = (1.0 - lb) * s_neg
    fox(6, fv_ref)
    hg = proj(3)
    gate_ref[...] = (hg * _sigmoid_pair(hg)[0]).astype(BF16)
    z = lax.dot_general(hb, wff_ref[...], _NT, preferred_element_type=F32) + bf_ref[...]
    lf = jnp.minimum(z, 0.0) - jnp.log1p(jnp.exp(-jnp.abs(z)))
    lf128_ref[...] = lf
    lf8_ref[...] = lf[:, 0:FOX_HEADS]


def _inproj(x3, mod3, n1, lbraw, bf128, wa, wff, wmg, groups, rows, fox_transposed):
    assert groups == 1 or not fox_transposed
    nb, r_all, d = x3.shape
    nt = r_all // rows
    steps = (nb // groups) * nt
    t = nb * r_all
    m = groups * rows
    xmap = lambda s: (s // nt, s % nt, 0)
    mmap = lambda s: (s // nt, 0, 0)
    omap = lambda s: (s, 0)

    def out(width, dtype):
        return jax.ShapeDtypeStruct((t, width), dtype), pl.BlockSpec((m, width), omap)

    def fox(dtype):
        if not fox_transposed:
            return out(FOX_WIDTH, dtype)
        return (jax.ShapeDtypeStruct((nb, FOX_WIDTH, r_all), dtype),
                pl.BlockSpec((1, FOX_WIDTH, m), lambda s: (s // nt, 0, s % nt)))

    outs = [out(HG_WIDTH, F32), out(HG_WIDTH, F32), out(HG_WIDTH, F32), out(HG_WIDTH, BF16),
            out(HG_WIDTH, BF16), fox(BF16), fox(F32), fox(F32),
            out(FOX_HEADS, F32), out(LANES, F32), out(wmg.shape[0], BF16)]
    return pl.pallas_call(
        functools.partial(_inproj_kernel, fox_transposed=fox_transposed),
        grid=(steps,),
        in_specs=[pl.BlockSpec((groups, rows, d), xmap),
                  pl.BlockSpec((groups, 1, mod3.shape[2]), mmap),
                  _resident(n1.shape), _resident(lbraw.shape), _resident(bf128.shape),
                  _resident(wa.shape), _resident(wff.shape), _resident(wmg.shape)],
        out_specs=[o[1] for o in outs],
        out_shape=[o[0] for o in outs],
        compiler_params=_params("arbitrary"),
        name="inproj",
    )(x3, mod3, n1, lbraw, bf128, wa, wff, wmg)


def _decay_kernel(lf_ref, place_ref, bk_ref, bqt_ref):
    seq = lf_ref.shape[1]
    lane = lax.broadcasted_iota(jnp.int32, (LANES, LANES), 1)
    slot = lane & (BIAS_SLOTS - 1)
    used = lane < FOX_HEADS * BIAS_SLOTS
    ones_k = jnp.where(jnp.logical_and(used, slot < N_PIECES), 1.0, 0.0)
    ones_q = jnp.where(jnp.logical_and(used, jnp.logical_and(slot >= N_PIECES, slot < 2 * N_PIECES)), 1.0, 0.0)
    carry = jnp.zeros((1, LANES), F32)
    for j in range(seq // LANES):
        sl = slice(j * LANES, (j + 1) * LANES)
        cs = _cumsum_rows(lf_ref[0, sl, :], LANES) + carry
        carry = cs[LANES - 1:LANES, :]
        pieces = _split3(cs * LOG2E)

        def place(base):
            return sum(jnp.dot(pieces[j], place_ref[base + j], preferred_element_type=F32)
                       for j in range(N_PIECES))

        bk_ref[0, sl, :] = (ones_k - place(N_PIECES)).astype(BF16)
        bqt_ref[0, :, sl] = (ones_q + place(0)).T.astype(BF16)


def _decay(lf3, place):
    b, seq, _ = lf3.shape
    return pl.pallas_call(
        _decay_kernel,
        grid=(b,),
        in_specs=[pl.BlockSpec((1, seq, LANES), lambda i: (i, 0, 0)), _resident(place.shape)],
        out_specs=[pl.BlockSpec((1, seq, LANES), lambda i: (i, 0, 0)),
                   pl.BlockSpec((1, LANES, seq), lambda i: (i, 0, 0))],
        out_shape=[jax.ShapeDtypeStruct((b, seq, LANES), BF16),
                   jax.ShapeDtypeStruct((b, LANES, seq), BF16)],
        compiler_params=_params("arbitrary"),
        name="decay",
    )(lf3, place)


def _hgrn_decays(q, k, g, chunk):
    rows = q.shape[0]
    n_chunks = rows // chunk
    b = _cumsum_rows(g, chunk)
    qt = (q * jnp.exp(b)).astype(BF16)
    kt = (k * jnp.exp(-b)).astype(BF16)
    b3 = b.reshape(n_chunks, chunk, HG_DK)
    bl3 = b3[:, chunk - 1:chunk, :]
    kd = (k.reshape(n_chunks, chunk, HG_DK) * jnp.exp(bl3 - b3)).reshape(rows, HG_DK).astype(BF16)
    return qt, kt, kd, jnp.exp(bl3)


def _hgrn_products(qt, kt, kd, v, chunk):
    rows = qt.shape[0]
    n_chunks = rows // chunk
    shift = chunk.bit_length() - 1
    a = lax.dot_general(qt, kt, _NT, preferred_element_type=F32)
    v_t = v.astype(F32).T.astype(BF16)
    col_chunk = lax.broadcasted_iota(jnp.int32, v_t.shape, 1) >> shift
    v_blocks = jnp.concatenate([jnp.where(col_chunk == c, v_t, jnp.zeros_like(v_t)) for c in range(n_chunks)],
                               axis=0)
    incr = jnp.dot(v_blocks, kd, preferred_element_type=F32)
    ri = lax.broadcasted_iota(jnp.int32, (rows, rows), 0)
    ci = lax.broadcasted_iota(jnp.int32, (rows, rows), 1)
    keep = (ri - ci).astype(jnp.uint32) <= (ri & (chunk - 1)).astype(jnp.uint32)
    o_intra = jnp.dot(jnp.where(keep, a, 0.0).astype(BF16), v, preferred_element_type=F32)
    return o_intra, incr


def _hgrn_blocks(blocks, chunk, state_in, state_out):
    n_chunks = LANES // chunk
    decays = [_hgrn_decays(q, k, g, chunk) for (q, k, g, _) in blocks]
    products = [_hgrn_products(qt, kt, kd, blk[3], chunk) for (qt, kt, kd, _), blk in zip(decays, blocks)]
    entering = []
    st = None
    for n in range(len(blocks) * n_chunks):
        j, c = divmod(n, n_chunks)
        st = state_in(n, st)
        entering.append(st.astype(BF16))
        st = st * decays[j][3][c] + products[j][1][c * HG_DV:(c + 1) * HG_DV, :]
        state_out(n, st)
    outs = []
    for j, ((qt, _, _, _), (o_intra, _)) in enumerate(zip(decays, products)):
        o_inter = [lax.dot_general(qt[c * chunk:(c + 1) * chunk, :], entering[j * n_chunks + c], _NT,
                                   preferred_element_type=F32) for c in range(n_chunks)]
        outs.append(o_intra + jnp.concatenate(o_inter, axis=0))
    return outs


HGRN_GROUP = 8


def _hgrn_prompt_kernel(q_ref, k_ref, g_ref, v_ref, o_ref, s_ref, *, chunk):
    seq = q_ref.shape[0]
    group_rows = HGRN_GROUP * LANES

    def body(i, st0):
        base = i * group_rows
        row_slices = [pl.ds(pl.multiple_of(base + j * LANES, LANES), LANES) for j in range(HGRN_GROUP)]
        blocks = [(q_ref[rs, :], k_ref[rs, :], g_ref[rs, :], v_ref[rs, :]) for rs in row_slices]
        last = []
        outs = _hgrn_blocks(blocks, chunk, lambda n, prev: st0 if prev is None else prev,
                            lambda n, st: last.append(st))
        for rs, o in zip(row_slices, outs):
            o_ref[rs, :] = o
        return last[-1]

    st = lax.fori_loop(0, seq // group_rows, body, jnp.zeros((HG_DV, HG_DK), F32))
    s_ref[0, 0] = st.T


def _hgrn_prompt(q, k, g, v, batch, seq, chunk):
    spec = pl.BlockSpec((seq, HG_DK), lambda b, h: (b, h))
    return pl.pallas_call(
        functools.partial(_hgrn_prompt_kernel, chunk=chunk),
        grid=(batch, HG_HEADS),
        in_specs=[spec, spec, spec, spec],
        out_specs=[spec, pl.BlockSpec((1, 1, HG_DK, HG_DV), lambda b, h: (b, h, 0, 0))],
        out_shape=[jax.ShapeDtypeStruct((batch * seq, HG_WIDTH), F32),
                   jax.ShapeDtypeStruct((batch, HG_HEADS, HG_DK, HG_DV), F32)],
        compiler_params=_params("arbitrary", "arbitrary"),
        name="hgrn_prompt",
    )(q, k, g, v)


def _hgrn_sample_kernel(q_ref, k_ref, g_ref, v_ref, s0_ref, o_ref, s_ref, *, chunk):
    per = LANES // chunk

    def state_in(n, prev):
        h, c = divmod(n, per)
        return s0_ref[c, h].T

    def state_out(n, st):
        h, c = divmod(n, per)
        s_ref[c, h] = st.T

    heads = [slice(h * HG_DK, (h + 1) * HG_DK) for h in range(HG_HEADS)]
    outs = _hgrn_blocks([(q_ref[:, hs], k_ref[:, hs], g_ref[:, hs], v_ref[:, hs]) for hs in heads],
                        chunk, state_in, state_out)
    for hs, o in zip(heads, outs):
        o_ref[:, hs] = o


def _hgrn_sample(q, k, g, v, s0, chunk):
    t = q.shape[0]
    per = LANES // chunk
    spec = pl.BlockSpec((LANES, HG_WIDTH), lambda i: (i, 0))
    sspec = pl.BlockSpec((per, HG_HEADS, HG_DK, HG_DV), lambda i: (i, 0, 0, 0))
    return pl.pallas_call(
        functools.partial(_hgrn_sample_kernel, chunk=chunk),
        grid=(t // LANES,),
        in_specs=[spec, spec, spec, spec, sspec],
        out_specs=[spec, sspec],
        out_shape=[jax.ShapeDtypeStruct((t, HG_WIDTH), F32),
                   jax.ShapeDtypeStruct(s0.shape, F32)],
        compiler_params=_params("arbitrary"),
        name="hgrn_sample",
    )(q, k, g, v, s0)


ONES_ROWS = 16


def _fox_prompt_kernel(uq_ref, uk_ref, um_ref, qt_ref, kt_ref, vt_ref, bk_ref, bqt_ref, o_ref,
                       kn_ref, va_ref, rhs_ref, mask_ref, s_ref, p_ref, *, tq, tk, n_units, limits):
    seq = kt_ref.shape[1]
    n_heads = LANES // FOX_DH
    pair = pl.program_id(1)

    for c in range(seq // LANES):
        cs = slice(c * LANES, (c + 1) * LANES)
        kn_ref[cs, 0:LANES] = kt_ref[:, cs].T.astype(BF16)
    kn_ref[:, LANES:2 * LANES] = bk_ref[...]
    for c in range(seq // tk):
        for e in range(n_heads):
            va_ref[c, e, 0:FOX_DH, :] = vt_ref[e * FOX_DH:(e + 1) * FOX_DH, c * tk:(c + 1) * tk].astype(BF16)
            va_ref[c, e, FOX_DH:FOX_DH + ONES_ROWS, :] = jnp.ones((ONES_ROWS, tk), BF16)
    row = lax.broadcasted_iota(jnp.int32, (LANES, tq), 0)
    for i in range(seq // tq):
        qs = slice(i * tq, (i + 1) * tq)
        qt = qt_ref[:, qs]
        bqt = bqt_ref[:, qs]
        for e in range(n_heads):
            rhs_ref[i, e, 0:LANES, :] = jnp.where(row // FOX_DH == e, qt, jnp.zeros_like(qt))
            rhs_ref[i, e, LANES:2 * LANES, :] = jnp.where(row // BIAS_SLOTS == pair * n_heads + e, bqt,
                                                          jnp.zeros_like(bqt))
    p_ref[1] = jnp.zeros(p_ref.shape[1:], BF16)

    key_minus_query = (lax.broadcasted_iota(jnp.int32, (tk, tq), 0)
                       - lax.broadcasted_iota(jnp.int32, (tk, tq), 1))
    for n, limit in enumerate(limits):
        mask_ref[n] = jnp.where(key_minus_query <= limit, 0.0, NEG_INF)

    def scores(u, slot):
        lhs = kn_ref[pl.ds(pl.multiple_of(uk_ref[u] * tk, tk), tk), :]
        for e in range(n_heads):
            s_ref[slot, e] = jnp.dot(lhs, rhs_ref[uq_ref[u], e], preferred_element_type=F32)

    def numerators(u, slot, ms):
        first = uk_ref[u] == 0
        mask = mask_ref[um_ref[u]]
        new_ms, alphas = [], []
        for e in range(n_heads):
            m = jnp.where(first, NEG_INF, ms[e])
            s = s_ref[slot, e] + mask
            m_new = jnp.maximum(m, jnp.max(s, axis=0, keepdims=True))
            p_ref[slot, e] = jnp.exp2(s - m_new).astype(BF16)
            new_ms.append(m_new)
            alphas.append(jnp.exp2(m - m_new))
        return new_ms, alphas

    def values(u, slot, alphas, ls, accs):
        new_ls, new_accs = [], []
        for e in range(n_heads):
            pv = jnp.dot(va_ref[uk_ref[u], e], p_ref[slot, e], preferred_element_type=F32)
            new_ls.append(alphas[e] * ls[e] + pv[FOX_DH:FOX_DH + 1, :])
            new_accs.append(alphas[e] * accs[e] + pv[0:FOX_DH, :])
        o_t = jnp.concatenate([new_accs[e] / new_ls[e] for e in range(n_heads)], axis=0)
        o_ref[pl.ds(pl.multiple_of(uq_ref[u] * tq, tq), tq), :] = o_t.T.astype(BF16)
        return new_ls, new_accs

    def step(u, slot, carry):
        ms, alphas, ls, accs = carry
        new_ls, new_accs = values(u - 1, 1 - slot, alphas, ls, accs)
        scores(u + 1, 1 - slot)
        new_ms, new_alphas = numerators(u, slot, ms)
        return new_ms, new_alphas, new_ls, new_accs

    def body(k, carry):
        carry = step(2 * k + 1, 0, carry)
        return step(2 * k + 2, 1, carry)

    rows = lambda n, v: [jnp.full((n, tq), v, F32) for _ in range(n_heads)]
    scores(1, 0)
    carry = (rows(1, NEG_INF), rows(1, 1.0), rows(1, 1.0), rows(FOX_DH, 0.0))
    _, alphas, ls, accs = lax.fori_loop(0, n_units // 2, body, carry)
    values(n_units, 1, alphas, ls, accs)


def _fox_prompt(fqt, fkt, fvt, bk, bqt, batch, seq, tq, tk):
    assert tk % tq == 0 and seq % tk == 0
    nq = seq // tq
    pairs = FOX_WIDTH // LANES
    n_heads = LANES // FOX_DH
    units = [(i, j) for i in range(nq) for j in range((i * tq) // tk + 1)]
    n_units = len(units)
    assert n_units % 2 == 0
    padded = [units[0]] + units + [units[-1]]
    unit_limits = [min(i * tq - j * tk, tk - 1) for i, j in padded]
    limits = tuple(sorted(set(unit_limits)))
    table = lambda vals: jnp.asarray(vals, jnp.int32)
    uq, uk = table([u[0] for u in padded]), table([u[1] for u in padded])
    um = table([limits.index(v) for v in unit_limits])
    head_pair = lambda b, p, *_: (b, p, 0)
    per_batch = lambda b, p, *_: (b, 0, 0)
    grid_spec = pltpu.PrefetchScalarGridSpec(
        num_scalar_prefetch=3,
        grid=(batch, pairs),
        in_specs=[pl.BlockSpec((None, LANES, seq), head_pair),
                  pl.BlockSpec((None, LANES, seq), head_pair),
                  pl.BlockSpec((None, LANES, seq), head_pair),
                  pl.BlockSpec((None, seq, LANES), per_batch),
                  pl.BlockSpec((None, LANES, seq), per_batch)],
        out_specs=pl.BlockSpec((seq, LANES), lambda b, p, *_: (b, p)),
        scratch_shapes=[pltpu.VMEM((seq, 2 * LANES), BF16),
                        pltpu.VMEM((seq // tk, n_heads, FOX_DH + ONES_ROWS, tk), BF16),
                        pltpu.VMEM((nq, n_heads, 2 * LANES, tq), BF16),
                        pltpu.VMEM((len(limits), tk, tq), F32),
                        pltpu.VMEM((2, n_heads, tk, tq), F32),
                        pltpu.VMEM((2, n_heads, tk, tq), BF16)],
    )
    return pl.pallas_call(
        functools.partial(_fox_prompt_kernel, tq=tq, tk=tk, n_units=n_units, limits=limits),
        grid_spec=grid_spec,
        out_shape=jax.ShapeDtypeStruct((batch * seq, FOX_WIDTH), BF16),
        compiler_params=_params("arbitrary", "arbitrary"),
        name="fox_prompt",
    )(uq, uk, um, fqt, fkt, fvt, bk, bqt)


def _fox_sample_kernel(pt_ref, q_ref, kn_ref, vn_ref, lfn_ref, sfx_ref, ck_hbm, cv_hbm, cl_hbm, o_ref,
                       k_buf, v_buf, lf_buf, sems, *, n_pages):
    n = pl.program_id(0)
    n_seq = pl.num_programs(0)
    slot = lax.rem(n, 2)

    def page_copies(seq, to_slot):
        copies = []
        for j in range(n_pages):
            page = pt_ref[seq * n_pages + j]
            copies.append(pltpu.make_async_copy(ck_hbm.at[page], k_buf.at[to_slot, j], sems.at[0, to_slot]))
            copies.append(pltpu.make_async_copy(cv_hbm.at[page], v_buf.at[to_slot, j], sems.at[0, to_slot]))
            copies.append(pltpu.make_async_copy(cl_hbm.at[page], lf_buf.at[to_slot, j], sems.at[1, to_slot]))
        return copies

    @pl.when(n == 0)
    def _():
        for c in page_copies(0, 0):
            c.start()

    ahead = jnp.minimum(n + 1, n_seq - 1)
    for c in page_copies(ahead, 1 - slot):
        c.start()
    for c in page_copies(n, slot):
        c.wait()
    k_refs = [k_buf.at[slot, j] for j in range(n_pages)]
    v_refs = [v_buf.at[slot, j] for j in range(n_pages)]
    lf_refs = [lf_buf.at[slot, j] for j in range(n_pages)]
    steps = q_ref.shape[1]
    rows = FOX_HEADS * steps
    pad = PAGE_SIZE - steps

    def per_head_rows(x8):
        return jnp.broadcast_to(x8[:, None, :], (FOX_HEADS, steps, x8.shape[1])).reshape(rows, x8.shape[1])

    q = q_ref[0].astype(F32)
    q_rows = jnp.concatenate([q] * FOX_HEADS, axis=0)
    row_head = lax.broadcasted_iota(jnp.int32, (rows, FOX_WIDTH), 0) // steps
    lane_head = lax.broadcasted_iota(jnp.int32, (rows, FOX_WIDTH), 1) // FOX_DH
    head_mask = row_head == lane_head
    qbd = jnp.where(head_mask, q_rows, 0.0).astype(BF16)

    pn = _cumsum_rows(lfn_ref[0], steps)
    pn_rows = jnp.concatenate([pn] * FOX_HEADS, axis=0)
    r_h = lax.broadcasted_iota(jnp.int32, (rows, LANES), 0) // steps
    r_i = lax.broadcasted_iota(jnp.int32, (rows, LANES), 0) % steps
    c_l = lax.broadcasted_iota(jnp.int32, (rows, LANES), 1)
    pcol = jnp.sum(jnp.where(c_l == r_h, pn_rows, 0.0), axis=1, keepdims=True)
    pn_t = jnp.concatenate([pn, jnp.zeros((pad, LANES), F32)], axis=0).T[0:FOX_HEADS, :]

    lf_all = jnp.concatenate([lf_refs[j][...] for j in range(n_pages)], axis=0)
    sfx = _dot_f32_lhs(lf_all, sfx_ref[...])
    carry = jnp.zeros((FOX_HEADS, PAGE_SIZE), F32)
    page_bias = [None] * n_pages
    for j in reversed(range(n_pages)):
        blk = sfx[j * FOX_HEADS:(j + 1) * FOX_HEADS, :]
        page_bias[j] = blk[:, 0:PAGE_SIZE] + carry
        carry = carry + blk[:, PAGE_SIZE:2 * PAGE_SIZE]

    s_tiles = []
    for j in range(n_pages):
        s = jnp.dot(qbd, k_refs[j][...].astype(BF16), preferred_element_type=F32)
        s_tiles.append(s + (per_head_rows(page_bias[j]) + pcol))
    kn = jnp.concatenate([kn_ref[0], jnp.zeros((pad, FOX_WIDTH), F32)], axis=0).astype(BF16)
    s_new = lax.dot_general(qbd, kn, _NT, preferred_element_type=F32) + (pcol - per_head_rows(pn_t))
    s_tiles.append(jnp.where(c_l <= r_i, s_new, NEG_INF))

    m_el = s_tiles[0]
    for s in s_tiles[1:]:
        m_el = jnp.maximum(m_el, s)
    m = jnp.max(m_el, axis=1, keepdims=True)
    p_tiles = [jnp.exp(s - m) for s in s_tiles]
    l_el = p_tiles[0]
    for p in p_tiles[1:]:
        l_el = l_el + p
    l = jnp.sum(l_el, axis=1, keepdims=True)

    vn = jnp.concatenate([vn_ref[0], jnp.zeros((pad, FOX_WIDTH), F32)], axis=0).astype(BF16)
    o = jnp.dot(p_tiles[n_pages].astype(BF16), vn, preferred_element_type=F32)
    for j in range(n_pages):
        o = o + lax.dot_general(p_tiles[j].astype(BF16), v_refs[j][...].astype(BF16), _NT,
                                preferred_element_type=F32)
    o = jnp.where(head_mask, o / l, 0.0)
    o_ref[0] = jnp.sum(o.reshape(FOX_HEADS, steps, FOX_WIDTH), axis=0).astype(BF16)

    @pl.when(n == n_seq - 1)
    def _():
        for c in page_copies(ahead, 1 - slot):
            c.wait()


def _fox_sample(page_table, fq3, kn3, vn3, lfn3, sfx_mat, cache_kt, cache_vt, cache_lft):
    nseq, n_pages = page_table.shape
    steps = fq3.shape[1]
    seq_spec = lambda w: pl.BlockSpec((1, steps, w), lambda n, pt: (n, 0, 0))
    in_hbm = pl.BlockSpec(memory_space=pl.ANY)
    grid_spec = pltpu.PrefetchScalarGridSpec(
        num_scalar_prefetch=1,
        grid=(nseq,),
        in_specs=[seq_spec(FOX_WIDTH), seq_spec(FOX_WIDTH), seq_spec(FOX_WIDTH), seq_spec(LANES),
                  pl.BlockSpec(sfx_mat.shape, lambda n, pt: (0, 0)), in_hbm, in_hbm, in_hbm],
        out_specs=pl.BlockSpec((1, steps, FOX_WIDTH), lambda n, pt: (n, 0, 0)),
        scratch_shapes=[pltpu.VMEM((2, n_pages, FOX_WIDTH, PAGE_SIZE), cache_kt.dtype),
                        pltpu.VMEM((2, n_pages, FOX_WIDTH, PAGE_SIZE), cache_vt.dtype),
                        pltpu.VMEM((2, n_pages, FOX_HEADS, PAGE_SIZE), cache_lft.dtype),
                        pltpu.SemaphoreType.DMA((2, 2))],
    )
    return pl.pallas_call(
        functools.partial(_fox_sample_kernel, n_pages=n_pages),
        grid_spec=grid_spec,
        out_shape=jax.ShapeDtypeStruct((nseq, steps, FOX_WIDTH), BF16),
        compiler_params=_params("arbitrary"),
        name="fox_sample",
    )(page_table.reshape(-1), fq3, kn3, vn3, lfn3, sfx_mat, cache_kt, cache_vt, cache_lft)


def _post_kernel(x_ref, mod_ref, oh_ref, gate_ref, of_ref, mg_ref, onorm_ref, n1post_ref, n2pre_ref,
                 n2post_ref, wbh_ref, wbf_ref, wout_ref, wup_ref, wdn_ref, y_ref, *, ff_chunk):
    g_, r_, d = x_ref.shape
    m = g_ * r_
    mod = mod_ref[...]
    gt1 = mod[:, :, 2 * d:3 * d]
    sh2 = mod[:, :, 3 * d:4 * d]
    sc2 = mod[:, :, 4 * d:5 * d]
    gt2 = mod[:, :, 5 * d:6 * d]

    oh = oh_ref[...]
    parts = [_rms(oh[:, h * HG_DV:(h + 1) * HG_DV], onorm_ref[...]) for h in range(HG_HEADS)]
    ohn = (jnp.concatenate(parts, axis=1) * gate_ref[...].astype(F32)).astype(BF16)
    br_h = jnp.dot(ohn, wbh_ref[...], preferred_element_type=F32)
    br_f = jnp.dot(of_ref[...], wbf_ref[...], preferred_element_type=F32)
    mg = mg_ref[...].astype(F32)
    z = (mg[:, 0:d] * br_h + mg[:, d:2 * d] * br_f).astype(BF16)
    y = jnp.dot(z, wout_ref[...], preferred_element_type=F32).reshape(g_, r_, d)
    x1 = x_ref[...] + _rms(y, gt1 * n1post_ref[...])
    h2 = (_rms(x1, n2pre_ref[...] * (1.0 + sc2)) + sh2).reshape(m, d).astype(BF16)
    u = jnp.zeros((m, d), F32)
    for c in range(wup_ref.shape[1] // ff_chunk):
        cs = slice(c * ff_chunk, (c + 1) * ff_chunk)
        a = jnp.maximum(jnp.dot(h2, wup_ref[:, cs], preferred_element_type=F32), 0.0)
        u = u + jnp.dot((a * a).astype(BF16), wdn_ref[cs, :], preferred_element_type=F32)
    y_ref[...] = x1 + _rms(u.reshape(g_, r_, d), gt2 * n2post_ref[...])


def _post(x3, mod3, oh, gate, of, mg, onorm, n1post, n2pre, n2post, wbh, wbf, wout, wup, wdn,
          groups, rows, ff_chunk=1024):
    nb, r_all, d = x3.shape
    nt = r_all // rows
    steps = (nb // groups) * nt
    m = groups * rows
    xmap = lambda s: (s // nt, s % nt, 0)
    mmap = lambda s: (s // nt, 0, 0)
    tmap = lambda s: (s, 0)
    tok = lambda a: pl.BlockSpec((m, a.shape[1]), tmap)
    consts = [onorm, n1post, n2pre, n2post, wbh, wbf, wout, wup, wdn]
    return pl.pallas_call(
        functools.partial(_post_kernel, ff_chunk=ff_chunk),
        grid=(steps,),
        in_specs=[pl.BlockSpec((groups, rows, d), xmap),
                  pl.BlockSpec((groups, 1, mod3.shape[2]), mmap),
                  tok(oh), tok(gate), tok(of), tok(mg)] + [_resident(c.shape) for c in consts],
        out_specs=pl.BlockSpec((groups, rows, d), xmap),
        out_shape=jax.ShapeDtypeStruct(x3.shape, F32),
        compiler_params=_params("arbitrary"),
        name="post",
    )(x3, mod3, oh, gate, of, mg, *consts)


def _bias_place_matrices():
    shape = (2 * N_PIECES, LANES, LANES)
    s = lax.broadcasted_iota(jnp.int32, shape, 0)
    r = lax.broadcasted_iota(jnp.int32, shape, 1)
    c = lax.broadcasted_iota(jnp.int32, shape, 2)
    return jnp.logical_and(r < FOX_HEADS, c == BIAS_SLOTS * r + s).astype(BF16)


def _page_suffix_matrix():
    r = lax.broadcasted_iota(jnp.int32, (PAGE_SIZE, 2 * PAGE_SIZE), 0)
    c = lax.broadcasted_iota(jnp.int32, (PAGE_SIZE, 2 * PAGE_SIZE), 1)
    return jnp.logical_or(r > c, c >= PAGE_SIZE).astype(BF16)


def kernel(x_prompt, x_sample, c_prompt, c_sample, cache_k, cache_v, cache_logf, state_hgrn, page_table,
           ada_w, ada_b, norm_mix_pre, norm_mix_post, norm_mlp_pre, norm_mlp_post, w_in,
           hgrn_lower_bounds, hgrn_onorm, fox_b_f, w_br_h, w_br_f, w_out, w_mlp_up, w_mlp_down):
    batch, seq, d = x_prompt.shape
    nseq, steps, _ = x_sample.shape
    layer = 0
    n_phys = cache_k.shape[1]

    w_t = jnp.transpose(w_in[layer]).astype(BF16)
    n_a = 4 * HG_WIDTH + 3 * FOX_WIDTH
    wa = w_t[:n_a]
    wff = jnp.pad(w_t[n_a:n_a + FOX_HEADS], ((0, LANES - FOX_HEADS), (0, 0)))
    wmg = w_t[n_a + FOX_HEADS:]
    bf128 = jnp.pad(fox_b_f[layer], (0, LANES - FOX_HEADS)).reshape(1, LANES)
    lbraw = hgrn_lower_bounds
    vec3 = lambda v: v.reshape(1, 1, -1)
    wbh, wbf = w_br_h[layer].astype(BF16), w_br_f[layer].astype(BF16)
    wout = w_out[layer].astype(BF16)
    wup, wdn = w_mlp_up[layer].astype(BF16), w_mlp_down[layer].astype(BF16)
    onorm = hgrn_onorm[layer].reshape(1, HG_DV)

    n_c = batch + nseq
    c_pad = -n_c % (2 * SUBLANES)
    c_all = jnp.concatenate([c_prompt, c_sample, jnp.zeros((c_pad, d), F32)], axis=0)
    mod = _ada(c_all, ada_w[layer], ada_b[layer].reshape(1, -1))
    mod_p = mod[:batch].reshape(batch, 1, -1)
    mod_s = mod[batch:n_c].reshape(nseq, 1, -1)

    tm = 512
    sgroups = tm // steps
    proj_args = (vec3(norm_mix_pre[layer]), lbraw, bf128, wa, wff, wmg)
    (q_p, g_p, k_p, v_p, gate_p, fq_p, fk_p, fv_p, lf8_p, lf128_p, mg_p) = _inproj(
        x_prompt, mod_p, *proj_args, groups=1, rows=tm, fox_transposed=True)
    (q_s, g_s, k_s, v_s, gate_s, fq_s, fk_s, fv_s, lf8_s, lf128_s, mg_s) = _inproj(
        x_sample, mod_s, *proj_args, groups=sgroups, rows=steps, fox_transposed=False)

    chunk_p = HG_CHUNK if seq % HG_CHUNK == 0 else seq
    oh_p, s_p = _hgrn_prompt(q_p, k_p, g_p, v_p, batch, seq, chunk_p)
    oh_s, s_s = _hgrn_sample(q_s, k_s, g_s, v_s, state_hgrn[layer], steps)

    tq, tk = 256, 512
    bias_k, bias_qt = _decay(lf128_p.reshape(batch, seq, LANES), _bias_place_matrices())
    of_p = _fox_prompt(fq_p, fk_p, fv_p, bias_k, bias_qt, batch, seq, tq, tk)

    ckt = jnp.transpose(cache_k[layer], (0, 2, 3, 1)).reshape(n_phys, FOX_WIDTH, PAGE_SIZE)
    cvt = jnp.transpose(cache_v[layer], (0, 2, 3, 1)).reshape(n_phys, FOX_WIDTH, PAGE_SIZE)
    clt = jnp.transpose(cache_logf[layer], (0, 2, 1))
    of_s = _fox_sample(page_table, fq_s.reshape(nseq, steps, FOX_WIDTH), fk_s.reshape(nseq, steps, FOX_WIDTH),
                       fv_s.reshape(nseq, steps, FOX_WIDTH), lf128_s.reshape(nseq, steps, LANES),
                       _page_suffix_matrix(), ckt, cvt, clt).reshape(nseq * steps, FOX_WIDTH)

    post_args = (onorm, vec3(norm_mix_post[layer]), vec3(norm_mlp_pre[layer]), vec3(norm_mlp_post[layer]),
                 wbh, wbf, wout, wup, wdn)
    y_p = _post(x_prompt, mod_p, oh_p, gate_p, of_p, mg_p, *post_args, groups=1, rows=tm)
    y_s = _post(x_sample, mod_s, oh_s, gate_s, of_s, mg_s, *post_args, groups=sgroups, rows=steps)

    k_prompt = fk_p.reshape(1, batch, FOX_HEADS, FOX_DH, seq).transpose(0, 1, 4, 2, 3)
    v_prompt = fv_p.reshape(1, batch, FOX_HEADS, FOX_DH, seq).transpose(0, 1, 4, 2, 3)
    logf_prompt = lf8_p.reshape(1, batch, seq, FOX_HEADS)
    k_sample = fk_s.reshape(1, nseq, steps, FOX_HEADS, FOX_DH)
    v_sample = fv_s.reshape(1, nseq, steps, FOX_HEADS, FOX_DH)
    logf_sample = lf8_s.reshape(1, nseq, steps, FOX_HEADS)
    return (y_p, y_s, k_prompt, v_prompt, logf_prompt, s_p[None], k_sample, v_sample, logf_sample, s_s[None])
```

```python
import functools

import jax
import jax.numpy as jnp
from jax import lax
from jax.experimental import pallas as pl
from jax.experimental.pallas import tpu as pltpu

F32 = jnp.float32
BF16 = jnp.bfloat16

LANES = 128
SUBLANES = 8
VMEM_LIMIT_BYTES = 56 * 1024 * 1024

D_MODEL = 1024
HG_HEADS = 4
HG_DK = 128
HG_DV = 128
HG_CHUNK = 32
FOX_HEADS = 8
FOX_DH = 64
FOX_WIDTH = FOX_HEADS * FOX_DH
HG_WIDTH = HG_HEADS * HG_DV
PAGE_SIZE = 128
N_MOD = 6
RMS_EPS = 1e-6
NEG_INF = float("-inf")
LOG2E = 1.4426950408889634
N_PIECES = 3
BIAS_SLOTS = 8

_NT = (((1,), (1,)), ((), ()))
_TN = (((0,), (0,)), ((), ()))


def _params(*sem):
    return pltpu.CompilerParams(dimension_semantics=sem, vmem_limit_bytes=VMEM_LIMIT_BYTES)


def _resident(shape):
    nd = len(shape)
    return pl.BlockSpec(shape, lambda *_: (0,) * nd, pipeline_mode=pl.Buffered(1))


def _sigmoid_pair(x):
    t = jnp.exp(-jnp.abs(x))
    r = 1.0 / (1.0 + t)
    tr = t * r
    pos = x >= 0
    return jnp.where(pos, r, tr), jnp.where(pos, tr, r)


def _split3(x):
    hi = x.astype(BF16)
    r1 = x - hi.astype(F32)
    mid = r1.astype(BF16)
    lo = (r1 - mid.astype(F32)).astype(BF16)
    return hi, mid, lo


def _dot_f32_lhs(x, w):
    hi, mid, lo = _split3(x)
    d = lambda p: jnp.dot(p, w, preferred_element_type=F32)
    return (d(lo) + d(mid)) + d(hi)


def _cumsum_rows(x, period):
    row = lax.broadcasted_iota(jnp.int32, x.shape, 0) & (period - 1)
    s = 1
    while s < period:
        x = x + jnp.where(row >= s, pltpu.roll(x, s, axis=0), 0.0)
        s *= 2
    return x


def _rms(x, w):
    return x * lax.rsqrt(jnp.mean(x * x, axis=-1, keepdims=True) + RMS_EPS) * w


def _ada_kernel(c_ref, w_ref, b_ref, o_ref):
    c = c_ref[...]
    s, _ = _sigmoid_pair(c)
    a = (c * s).astype(BF16)
    o_ref[...] = jnp.dot(a, w_ref[...].astype(BF16), preferred_element_type=F32) + b_ref[...]


def _ada(c, w, b, tn=1536):
    m, d = c.shape
    n = w.shape[1]
    return pl.pallas_call(
        _ada_kernel,
        grid=(n // tn,),
        in_specs=[pl.BlockSpec((m, d), lambda j: (0, 0)),
                  pl.BlockSpec((d, tn), lambda j: (0, j)),
                  pl.BlockSpec((1, tn), lambda j: (0, j))],
        out_specs=pl.BlockSpec((m, tn), lambda j: (0, j)),
        out_shape=jax.ShapeDtypeStruct((m, n), F32),
        compiler_params=_params("arbitrary"),
        name="ada",
    )(c, w, b)


def _inproj_kernel(x_ref, mod_ref, n1_ref, lbraw_ref, bf_ref, wa_ref, wff_ref, wmg_ref,
                   q_ref, g_ref, k_ref, v_ref, gate_ref, fq_ref, fk_ref, fv_ref,
                   lf8_ref, lf128_ref, mg_ref, *, fox_transposed):
    g_, r_, d = x_ref.shape
    m = g_ * r_
    x = x_ref[...]
    mod = mod_ref[...]
    sh1 = mod[:, :, 0:d]
    sc1 = mod[:, :, d:2 * d]
    h = _rms(x, n1_ref[...] * (1.0 + sc1)) + sh1
    hb = h.reshape(m, d).astype(BF16)

    raw = lbraw_ref[...]
    e = jnp.exp(raw - jnp.max(raw, axis=0, keepdims=True))
    lb = e[0:1, :] / jnp.sum(e, axis=0, keepdims=True)

    w = HG_WIDTH

    def proj(c):
        return lax.dot_general(hb, wa_ref[c * w:(c + 1) * w, :], _NT, preferred_element_type=F32)

    def proj_t(c):
        return lax.dot_general(wa_ref[c * w:(c + 1) * w, :], hb, _NT, preferred_element_type=F32)

    def merge_gate(c):
        mgc = lax.dot_general(hb, wmg_ref[c * w:(c + 1) * w, :], _NT, preferred_element_type=F32)
        mg_ref[:, c * w:(c + 1) * w] = _sigmoid_pair(mgc)[0].astype(BF16)

    def fox(c, ref, scale=None):
        y = proj_t(c) if fox_transposed else proj(c)
        y = y if scale is None else y * scale
        if fox_transposed:
            ref[0] = y.astype(ref.dtype)
        else:
            ref[...] = y.astype(ref.dtype)

    assert wmg_ref.shape[0] == 4 * w
    merge_gate(0)
    q_ref[...] = proj(0) * (HG_DK ** -0.5)
    merge_gate(1)
    v_ref[...] = proj(2).astype(BF16)
    merge_gate(2)
    fox(4, fq_ref, (LOG2E if fox_transposed else 1.0) * FOX_DH ** -0.5)
    merge_gate(3)
    fox(5, fk_ref)
    s_pos, s_neg = _sigmoid_pair(proj(1))
    g_ref[...] = jnp.log(lb + (1.0 - lb) * s_pos)
    k_ref[...] = (1.0 - lb) * s_neg
    fox(6, fv_ref)
    hg = proj(3)
    gate_ref[...] = (hg * _sigmoid_pair(hg)[0]).astype(BF16)
    z = lax.dot_general(hb, wff_ref[...], _NT, preferred_element_type=F32) + bf_ref[...]
    lf = jnp.minimum(z, 0.0) - jnp.log1p(jnp.exp(-jnp.abs(z)))
    lf128_ref[...] = lf
    lf8_ref[...] = lf[:, 0:FOX_HEADS]


def _inproj(x3, mod3, n1, lbraw, bf128, wa, wff, wmg, groups, rows, fox_transposed):
    assert groups == 1 or not fox_transposed
    nb, r_all, d = x3.shape
    nt = r_all // rows
    steps = (nb // groups) * nt
    t = nb * r_all
    m = groups * rows
    xmap = lambda s: (s // nt, s % nt, 0)
    mmap = lambda s: (s // nt, 0, 0)
    omap = lambda s: (s, 0)

    def out(width, dtype):
        return jax.ShapeDtypeStruct((t, width), dtype), pl.BlockSpec((m, width), omap)

    def fox(dtype):
        if not fox_transposed:
            return out(FOX_WIDTH, dtype)
        return (jax.ShapeDtypeStruct((nb, FOX_WIDTH, r_all), dtype),
                pl.BlockSpec((1, FOX_WIDTH, m), lambda s: (s // nt, 0, s % nt)))

    outs = [out(HG_WIDTH, F32), out(HG_WIDTH, F32), out(HG_WIDTH, F32), out(HG_WIDTH, BF16),
            out(HG_WIDTH, BF16), fox(BF16), fox(F32), fox(F32),
            out(FOX_HEADS, F32), out(LANES, F32), out(wmg.shape[0], BF16)]
    return pl.pallas_call(
        functools.partial(_inproj_kernel, fox_transposed=fox_transposed),
        grid=(steps,),
        in_specs=[pl.BlockSpec((groups, rows, d), xmap),
                  pl.BlockSpec((groups, 1, mod3.shape[2]), mmap),
                  _resident(n1.shape), _resident(lbraw.shape), _resident(bf128.shape),
                  _resident(wa.shape), _resident(wff.shape), _resident(wmg.shape)],
        out_specs=[o[1] for o in outs],
        out_shape=[o[0] for o in outs],
        compiler_params=_params("arbitrary"),
        name="inproj",
    )(x3, mod3, n1, lbraw, bf128, wa, wff, wmg)


def _decay_kernel(lf_ref, place_ref, bk_ref, bqt_ref):
    seq = lf_ref.shape[1]
    lane = lax.broadcasted_iota(jnp.int32, (LANES, LANES), 1)
    slot = lane & (BIAS_SLOTS - 1)
    used = lane < FOX_HEADS * BIAS_SLOTS
    ones_k = jnp.where(jnp.logical_and(used, slot < N_PIECES), 1.0, 0.0)
    ones_q = jnp.where(jnp.logical_and(used, jnp.logical_and(slot >= N_PIECES, slot < 2 * N_PIECES)), 1.0, 0.0)
    carry = jnp.zeros((1, LANES), F32)
    for j in range(seq // LANES):
        sl = slice(j * LANES, (j + 1) * LANES)
        cs = _cumsum_rows(lf_ref[0, sl, :], LANES) + carry
        carry = cs[LANES - 1:LANES, :]
        pieces = _split3(cs * LOG2E)

        def place(base):
            return sum(jnp.dot(pieces[j], place_ref[base + j], preferred_element_type=F32)
                       for j in range(N_PIECES))

        bk_ref[0, sl, :] = (ones_k - place(N_PIECES)).astype(BF16)
        bqt_ref[0, :, sl] = (ones_q + place(0)).T.astype(BF16)


def _decay(lf3, place):
    b, seq, _ = lf3.shape
    return pl.pallas_call(
        _decay_kernel,
        grid=(b,),
        in_specs=[pl.BlockSpec((1, seq, LANES), lambda i: (i, 0, 0)), _resident(place.shape)],
        out_specs=[pl.BlockSpec((1, seq, LANES), lambda i: (i, 0, 0)),
                   pl.BlockSpec((1, LANES, seq), lambda i: (i, 0, 0))],
        out_shape=[jax.ShapeDtypeStruct((b, seq, LANES), BF16),
                   jax.ShapeDtypeStruct((b, LANES, seq), BF16)],
        compiler_params=_params("arbitrary"),
        name="decay",
    )(lf3, place)


def _hgrn_decays(q, k, g, chunk):
    rows = q.shape[0]
    n_chunks = rows // chunk
    b = _cumsum_rows(g, chunk)
    qt = (q * jnp.exp(b)).astype(BF16)
    kt = (k * jnp.exp(-b)).astype(BF16)
    b3 = b.reshape(n_chunks, chunk, HG_DK)
    bl3 = b3[:, chunk - 1:chunk, :]
    kd = (k.reshape(n_chunks, chunk, HG_DK) * jnp.exp(bl3 - b3)).reshape(rows, HG_DK).astype(BF16)
    return qt, kt, kd, jnp.exp(bl3)


def _hgrn_products(qt, kt, kd, v, chunk):
    rows = qt.shape[0]
    n_chunks = rows // chunk
    shift = chunk.bit_length() - 1
    a = lax.dot_general(qt, kt, _NT, preferred_element_type=F32)
    v_t = v.astype(F32).T.astype(BF16)
    col_chunk = lax.broadcasted_iota(jnp.int32, v_t.shape, 1) >> shift
    v_blocks = jnp.concatenate([jnp.where(col_chunk == c, v_t, jnp.zeros_like(v_t)) for c in range(n_chunks)],
                               axis=0)
    incr = jnp.dot(v_blocks, kd, preferred_element_type=F32)
    ri = lax.broadcasted_iota(jnp.int32, (rows, rows), 0)
    ci = lax.broadcasted_iota(jnp.int32, (rows, rows), 1)
    keep = (ri - ci).astype(jnp.uint32) <= (ri & (chunk - 1)).astype(jnp.uint32)
    o_intra = jnp.dot(jnp.where(keep, a, 0.0).astype(BF16), v, preferred_element_type=F32)
    return o_intra, incr


def _hgrn_blocks(blocks, chunk, state_in, state_out):
    n_chunks = LANES // chunk
    decays = [_hgrn_decays(q, k, g, chunk) for (q, k, g, _) in blocks]
    products = [_hgrn_products(qt, kt, kd, blk[3], chunk) for (qt, kt, kd, _), blk in zip(decays, blocks)]
    entering = []
    st = None
    for n in range(len(blocks) * n_chunks):
        j, c = divmod(n, n_chunks)
        st = state_in(n, st)
        entering.append(st.astype(BF16))
        st = st * decays[j][3][c] + products[j][1][c * HG_DV:(c + 1) * HG_DV, :]
        state_out(n, st)
    outs = []
    for j, ((qt, _, _, _), (o_intra, _)) in enumerate(zip(decays, products)):
        o_inter = [lax.dot_general(qt[c * chunk:(c + 1) * chunk, :], entering[j * n_chunks + c], _NT,
                                   preferred_element_type=F32) for c in range(n_chunks)]
        outs.append(o_intra + jnp.concatenate(o_inter, axis=0))
    return outs


HGRN_GROUP = 8


def _hgrn_prompt_kernel(q_ref, k_ref, g_ref, v_ref, o_ref, s_ref, *, chunk):
    seq = q_ref.shape[0]
    group_rows = HGRN_GROUP * LANES

    def body(i, st0):
        base = i * group_rows
        row_slices = [pl.ds(pl.multiple_of(base + j * LANES, LANES), LANES) for j in range(HGRN_GROUP)]
        blocks = [(q_ref[rs, :], k_ref[rs, :], g_ref[rs, :], v_ref[rs, :]) for rs in row_slices]
        last = []
        outs = _hgrn_blocks(blocks, chunk, lambda n, prev: st0 if prev is None else prev,
                            lambda n, st: last.append(st))
        for rs, o in zip(row_slices, outs):
            o_ref[rs, :] = o
        return last[-1]

    st = lax.fori_loop(0, seq // group_rows, body, jnp.zeros((HG_DV, HG_DK), F32))
    s_ref[0, 0] = st.T


def _hgrn_prompt(q, k, g, v, batch, seq, chunk):
    spec = pl.BlockSpec((seq, HG_DK), lambda b, h: (b, h))
    return pl.pallas_call(
        functools.partial(_hgrn_prompt_kernel, chunk=chunk),
        grid=(batch, HG_HEADS),
        in_specs=[spec, spec, spec, spec],
        out_specs=[spec, pl.BlockSpec((1, 1, HG_DK, HG_DV), lambda b, h: (b, h, 0, 0))],
        out_shape=[jax.ShapeDtypeStruct((batch * seq, HG_WIDTH), F32),
                   jax.ShapeDtypeStruct((batch, HG_HEADS, HG_DK, HG_DV), F32)],
        compiler_params=_params("arbitrary", "arbitrary"),
        name="hgrn_prompt",
    )(q, k, g, v)


def _hgrn_sample_kernel(q_ref, k_ref, g_ref, v_ref, s0_ref, o_ref, s_ref, *, chunk):
    per = LANES // chunk

    def state_in(n, prev):
        h, c = divmod(n, per)
        return s0_ref[c, h].T

    def state_out(n, st):
        h, c = divmod(n, per)
        s_ref[c, h] = st.T

    heads = [slice(h * HG_DK, (h + 1) * HG_DK) for h in range(HG_HEADS)]
    outs = _hgrn_blocks([(q_ref[:, hs], k_ref[:, hs], g_ref[:, hs], v_ref[:, hs]) for hs in heads],
                        chunk, state_in, state_out)
    for hs, o in zip(heads, outs):
        o_ref[:, hs] = o


def _hgrn_sample(q, k, g, v, s0, chunk):
    t = q.shape[0]
    per = LANES // chunk
    spec = pl.BlockSpec((LANES, HG_WIDTH), lambda i: (i, 0))
    sspec = pl.BlockSpec((per, HG_HEADS, HG_DK, HG_DV), lambda i: (i, 0, 0, 0))
    return pl.pallas_call(
        functools.partial(_hgrn_sample_kernel, chunk=chunk),
        grid=(t // LANES,),
        in_specs=[spec, spec, spec, spec, sspec],
        out_specs=[spec, sspec],
        out_shape=[jax.ShapeDtypeStruct((t, HG_WIDTH), F32),
                   jax.ShapeDtypeStruct(s0.shape, F32)],
        compiler_params=_params("arbitrary"),
        name="hgrn_sample",
    )(q, k, g, v, s0)


ONES_ROWS = 16


def _prompt_attention(uq_ref, uk_ref, um_ref, qt_ref, kt_ref, vt_ref, bk_ref, bqt_ref, o_ref,
                      kn_ref, va_ref, rhs_ref, mask_ref, s_ref, p_ref, *, pair, tq, tk, n_units, limits):
    seq = kt_ref.shape[1]
    n_heads = LANES // FOX_DH

    def prepare():
        for c in range(seq // LANES):
            cs = slice(c * LANES, (c + 1) * LANES)
            kn_ref[cs, 0:LANES] = kt_ref[:, cs].T.astype(BF16)
        kn_ref[:, LANES:2 * LANES] = bk_ref[...]
        for c in range(seq // tk):
            for e in range(n_heads):
                va_ref[c, e, 0:FOX_DH, :] = vt_ref[e * FOX_DH:(e + 1) * FOX_DH, c * tk:(c + 1) * tk].astype(BF16)
                va_ref[c, e, FOX_DH:FOX_DH + ONES_ROWS, :] = jnp.ones((ONES_ROWS, tk), BF16)
        row = lax.broadcasted_iota(jnp.int32, (LANES, tq), 0)
        for i in range(seq // tq):
            qs = slice(i * tq, (i + 1) * tq)
            qt = qt_ref[:, qs]
            bqt = bqt_ref[:, qs]
            for e in range(n_heads):
                rhs_ref[i, e, 0:LANES, :] = jnp.where(row // FOX_DH == e, qt, jnp.zeros_like(qt))
                rhs_ref[i, e, LANES:2 * LANES, :] = jnp.where(row // BIAS_SLOTS == pair * n_heads + e, bqt,
                                                              jnp.zeros_like(bqt))
        p_ref[1] = jnp.zeros(p_ref.shape[1:], BF16)
        key_minus_query = (lax.broadcasted_iota(jnp.int32, (tk, tq), 0)
                           - lax.broadcasted_iota(jnp.int32, (tk, tq), 1))
        for n, limit in enumerate(limits):
            mask_ref[n] = jnp.where(key_minus_query <= limit, 0.0, NEG_INF)
        rows = lambda n, v: [jnp.full((n, tq), v, F32) for _ in range(n_heads)]
        scores(1, 0)
        return rows(1, NEG_INF), rows(1, 1.0), rows(1, 1.0), rows(FOX_DH, 0.0)

    def scores(u, slot):
        lhs = kn_ref[pl.ds(pl.multiple_of(uk_ref[u] * tk, tk), tk), :]
        for e in range(n_heads):
            s_ref[slot, e] = jnp.dot(lhs, rhs_ref[uq_ref[u], e], preferred_element_type=F32)

    def numerators(u, slot, ms):
        first = uk_ref[u] == 0
        mask = mask_ref[um_ref[u]]
        new_ms, alphas = [], []
        for e in range(n_heads):
            m = jnp.where(first, NEG_INF, ms[e])
            s = s_ref[slot, e] + mask
            m_new = jnp.maximum(m, jnp.max(s, axis=0, keepdims=True))
            p_ref[slot, e] = jnp.exp2(s - m_new).astype(BF16)
            new_ms.append(m_new)
            alphas.append(jnp.exp2(m - m_new))
        return new_ms, alphas

    def values(u, slot, alphas, ls, accs):
        new_ls, new_accs = [], []
        for e in range(n_heads):
            pv = jnp.dot(va_ref[uk_ref[u], e], p_ref[slot, e], preferred_element_type=F32)
            new_ls.append(alphas[e] * ls[e] + pv[FOX_DH:FOX_DH + 1, :])
            new_accs.append(alphas[e] * accs[e] + pv[0:FOX_DH, :])
        o_t = jnp.concatenate([new_accs[e] / new_ls[e] for e in range(n_heads)], axis=0)
        o_ref[pl.ds(pl.multiple_of(uq_ref[u] * tq, tq), tq), :] = o_t.T.astype(BF16)
        return new_ls, new_accs

    def step(u, slot, carry):
        ms, alphas, ls, accs = carry
        new_ls, new_accs = values(u - 1, 1 - slot, alphas, ls, accs)
        scores(u + 1, 1 - slot)
        new_ms, new_alphas = numerators(u, slot, ms)
        return new_ms, new_alphas, new_ls, new_accs

    def body(k, carry):
        carry = step(2 * k + 1, 0, carry)
        return step(2 * k + 2, 1, carry)

    def run(lo, hi, state):
        return lax.fori_loop(lo, hi, body, state)

    def finish(state):
        _, alphas, ls, accs = state
        values(n_units, 1, alphas, ls, accs)

    return prepare, run, finish


def _sample_attend(q, k_new, v_new, lf_new, sfx_ref, k_refs, v_refs, lf_refs):
    n_pages = len(k_refs)
    steps = q.shape[0]
    rows = FOX_HEADS * steps
    pad = PAGE_SIZE - steps

    def per_head_rows(x8):
        return jnp.broadcast_to(x8[:, None, :], (FOX_HEADS, steps, x8.shape[1])).reshape(rows, x8.shape[1])

    q = q.astype(F32)
    q_rows = jnp.concatenate([q] * FOX_HEADS, axis=0)
    row_head = lax.broadcasted_iota(jnp.int32, (rows, FOX_WIDTH), 0) // steps
    lane_head = lax.broadcasted_iota(jnp.int32, (rows, FOX_WIDTH), 1) // FOX_DH
    head_mask = row_head == lane_head
    qbd = jnp.where(head_mask, q_rows, 0.0).astype(BF16)

    pn = _cumsum_rows(lf_new, steps)
    pn_rows = jnp.concatenate([pn] * FOX_HEADS, axis=0)
    r_h = lax.broadcasted_iota(jnp.int32, (rows, LANES), 0) // steps
    r_i = lax.broadcasted_iota(jnp.int32, (rows, LANES), 0) % steps
    c_l = lax.broadcasted_iota(jnp.int32, (rows, LANES), 1)
    pcol = jnp.sum(jnp.where(c_l == r_h, pn_rows, 0.0), axis=1, keepdims=True)
    pn_t = jnp.concatenate([pn, jnp.zeros((pad, LANES), F32)], axis=0).T[0:FOX_HEADS, :]

    lf_all = jnp.concatenate([lf_refs[j][...] for j in range(n_pages)], axis=0)
    sfx = _dot_f32_lhs(lf_all, sfx_ref[...])
    carry = jnp.zeros((FOX_HEADS, PAGE_SIZE), F32)
    page_bias = [None] * n_pages
    for j in reversed(range(n_pages)):
        blk = sfx[j * FOX_HEADS:(j + 1) * FOX_HEADS, :]
        page_bias[j] = blk[:, 0:PAGE_SIZE] + carry
        carry = carry + blk[:, PAGE_SIZE:2 * PAGE_SIZE]

    s_tiles = []
    for j in range(n_pages):
        s = jnp.dot(qbd, k_refs[j][...].astype(BF16), preferred_element_type=F32)
        s_tiles.append(s + (per_head_rows(page_bias[j]) + pcol))
    kn = jnp.concatenate([k_new, jnp.zeros((pad, FOX_WIDTH), F32)], axis=0).astype(BF16)
    s_new = lax.dot_general(qbd, kn, _NT, preferred_element_type=F32) + (pcol - per_head_rows(pn_t))
    s_tiles.append(jnp.where(c_l <= r_i, s_new, NEG_INF))

    m_el = s_tiles[0]
    for s in s_tiles[1:]:
        m_el = jnp.maximum(m_el, s)
    m = jnp.max(m_el, axis=1, keepdims=True)
    p_tiles = [jnp.exp(s - m) for s in s_tiles]
    l_el = p_tiles[0]
    for p in p_tiles[1:]:
        l_el = l_el + p
    l = jnp.sum(l_el, axis=1, keepdims=True)

    vn = jnp.concatenate([v_new, jnp.zeros((pad, FOX_WIDTH), F32)], axis=0).astype(BF16)
    o = jnp.dot(p_tiles[n_pages].astype(BF16), vn, preferred_element_type=F32)
    for j in range(n_pages):
        o = o + lax.dot_general(p_tiles[j].astype(BF16), v_refs[j][...].astype(BF16), _NT,
                                preferred_element_type=F32)
    o = jnp.where(head_mask, o / l, 0.0)
    return jnp.sum(o.reshape(FOX_HEADS, steps, FOX_WIDTH), axis=0)


def _fox_kernel(uq_ref, uk_ref, um_ref, pt_ref,
                qt_ref, kt_ref, vt_ref, bk_ref, bqt_ref,
                sq_ref, sk_ref, sv_ref, slf_ref, sfx_ref, ck_hbm, cv_hbm, cl_hbm,
                o_ref, so_ref,
                kn_ref, va_ref, rhs_ref, mask_ref, s_ref, p_ref, k_buf, v_buf, lf_buf, sems,
                *, tq, tk, n_units, limits, n_pages):
    step = pl.program_id(0) * pl.num_programs(1) + pl.program_id(1)
    n_steps = pl.num_programs(0) * pl.num_programs(1)
    per_step = sq_ref.shape[0]
    last_seq = n_steps * per_step - 1

    def page_copies(seq, slot):
        copies = []
        for j in range(n_pages):
            page = pt_ref[seq * n_pages + j]
            copies.append(pltpu.make_async_copy(ck_hbm.at[page], k_buf.at[slot, j], sems.at[0, slot]))
            copies.append(pltpu.make_async_copy(cv_hbm.at[page], v_buf.at[slot, j], sems.at[0, slot]))
            copies.append(pltpu.make_async_copy(cl_hbm.at[page], lf_buf.at[slot, j], sems.at[1, slot]))
        return copies

    @pl.when(step == 0)
    def _():
        for c in page_copies(0, 0):
            c.start()

    prepare, run, finish = _prompt_attention(
        uq_ref, uk_ref, um_ref, qt_ref, kt_ref, vt_ref, bk_ref, bqt_ref, o_ref,
        kn_ref, va_ref, rhs_ref, mask_ref, s_ref, p_ref,
        pair=pl.program_id(1), tq=tq, tk=tk, n_units=n_units, limits=limits)
    state = prepare()
    n_pairs = n_units // 2
    bounds = [(j * n_pairs) // per_step for j in range(per_step + 1)]
    for j in range(per_step):
        seq = step * per_step + j
        slot = j % 2
        ahead = jnp.minimum(seq + 1, last_seq)
        for c in page_copies(ahead, 1 - slot):
            c.start()
        for c in page_copies(seq, slot):
            c.wait()
        so_ref[j] = _sample_attend(
            sq_ref[j], sk_ref[j], sv_ref[j], slf_ref[j], sfx_ref,
            [k_buf.at[slot, n] for n in range(n_pages)], [v_buf.at[slot, n] for n in range(n_pages)],
            [lf_buf.at[slot, n] for n in range(n_pages)]).astype(BF16)
        state = run(bounds[j], bounds[j + 1], state)
    finish(state)

    @pl.when(step == n_steps - 1)
    def _():
        for c in page_copies(last_seq, per_step % 2):
            c.wait()


def _fox(fqt, fkt, fvt, bk, bqt, page_table, sq3, sk3, sv3, slf3, sfx_mat, cache_kt, cache_vt, cache_lft,
         tq, tk):
    batch, _, seq = fqt.shape
    nseq, n_pages = page_table.shape
    steps = sq3.shape[1]
    assert tk % tq == 0 and seq % tk == 0
    nq = seq // tq
    pairs = FOX_WIDTH // LANES
    n_heads = LANES // FOX_DH
    per_step = nseq // (batch * pairs)
    assert per_step * batch * pairs == nseq and per_step % 2 == 0
    units = [(i, j) for i in range(nq) for j in range((i * tq) // tk + 1)]
    n_units = len(units)
    assert n_units % 2 == 0
    padded = [units[0]] + units + [units[-1]]
    unit_limits = [min(i * tq - j * tk, tk - 1) for i, j in padded]
    limits = tuple(sorted(set(unit_limits)))
    table = lambda vals: jnp.asarray(vals, jnp.int32)
    uq, uk = table([u[0] for u in padded]), table([u[1] for u in padded])
    um = table([limits.index(v) for v in unit_limits])
    head_pair = lambda b, p, *_: (b, p, 0)
    per_batch = lambda b, p, *_: (b, 0, 0)
    sample = lambda w: pl.BlockSpec((per_step, steps, w), lambda b, p, *_: (b * pairs + p, 0, 0))
    in_hbm = pl.BlockSpec(memory_space=pl.ANY)
    grid_spec = pltpu.PrefetchScalarGridSpec(
        num_scalar_prefetch=4,
        grid=(batch, pairs),
        in_specs=[pl.BlockSpec((None, LANES, seq), head_pair),
                  pl.BlockSpec((None, LANES, seq), head_pair),
                  pl.BlockSpec((None, LANES, seq), head_pair),
                  pl.BlockSpec((None, seq, LANES), per_batch),
                  pl.BlockSpec((None, LANES, seq), per_batch),
                  sample(FOX_WIDTH), sample(FOX_WIDTH), sample(FOX_WIDTH), sample(LANES),
                  pl.BlockSpec(sfx_mat.shape, lambda b, p, *_: (0, 0)), in_hbm, in_hbm, in_hbm],
        out_specs=[pl.BlockSpec((seq, LANES), lambda b, p, *_: (b, p)), sample(FOX_WIDTH)],
        scratch_shapes=[pltpu.VMEM((seq, 2 * LANES), BF16),
                        pltpu.VMEM((seq // tk, n_heads, FOX_DH + ONES_ROWS, tk), BF16),
                        pltpu.VMEM((nq, n_heads, 2 * LANES, tq), BF16),
                        pltpu.VMEM((len(limits), tk, tq), F32),
                        pltpu.VMEM((2, n_heads, tk, tq), F32),
                        pltpu.VMEM((2, n_heads, tk, tq), BF16),
                        pltpu.VMEM((2, n_pages, FOX_WIDTH, PAGE_SIZE), cache_kt.dtype),
                        pltpu.VMEM((2, n_pages, FOX_WIDTH, PAGE_SIZE), cache_vt.dtype),
                        pltpu.VMEM((2, n_pages, FOX_HEADS, PAGE_SIZE), cache_lft.dtype),
                        pltpu.SemaphoreType.DMA((2, 2))],
    )
    return pl.pallas_call(
        functools.partial(_fox_kernel, tq=tq, tk=tk, n_units=n_units, limits=limits, n_pages=n_pages),
        grid_spec=grid_spec,
        out_shape=[jax.ShapeDtypeStruct((batch * seq, FOX_WIDTH), BF16),
                   jax.ShapeDtypeStruct((nseq, steps, FOX_WIDTH), BF16)],
        compiler_params=_params("arbitrary", "arbitrary"),
        name="fox",
    )(uq, uk, um, page_table.reshape(-1), fqt, fkt, fvt, bk, bqt, sq3, sk3, sv3, slf3, sfx_mat,
      cache_kt, cache_vt, cache_lft)


def _post_kernel(x_ref, mod_ref, oh_ref, gate_ref, of_ref, mg_ref, onorm_ref, n1post_ref, n2pre_ref,
                 n2post_ref, wbh_ref, wbf_ref, wout_ref, wup_ref, wdn_ref, y_ref, *, ff_chunk):
    g_, r_, d = x_ref.shape
    m = g_ * r_
    mod = mod_ref[...]
    gt1 = mod[:, :, 2 * d:3 * d]
    sh2 = mod[:, :, 3 * d:4 * d]
    sc2 = mod[:, :, 4 * d:5 * d]
    gt2 = mod[:, :, 5 * d:6 * d]

    oh = oh_ref[...]
    parts = [_rms(oh[:, h * HG_DV:(h + 1) * HG_DV], onorm_ref[...]) for h in range(HG_HEADS)]
    ohn = (jnp.concatenate(parts, axis=1) * gate_ref[...].astype(F32)).astype(BF16)
    br_h = jnp.dot(ohn, wbh_ref[...], preferred_element_type=F32)
    br_f = jnp.dot(of_ref[...], wbf_ref[...], preferred_element_type=F32)
    mg = mg_ref[...].astype(F32)
    z = (mg[:, 0:d] * br_h + mg[:, d:2 * d] * br_f).astype(BF16)
    y = jnp.dot(z, wout_ref[...], preferred_element_type=F32).reshape(g_, r_, d)
    x1 = x_ref[...] + _rms(y, gt1 * n1post_ref[...])
    h2 = (_rms(x1, n2pre_ref[...] * (1.0 + sc2)) + sh2).reshape(m, d).astype(BF16)
    u = jnp.zeros((m, d), F32)
    for c in range(wup_ref.shape[1] // ff_chunk):
        cs = slice(c * ff_chunk, (c + 1) * ff_chunk)
        a = jnp.maximum(jnp.dot(h2, wup_ref[:, cs], preferred_element_type=F32), 0.0)
        u = u + jnp.dot((a * a).astype(BF16), wdn_ref[cs, :], preferred_element_type=F32)
    y_ref[...] = x1 + _rms(u.reshape(g_, r_, d), gt2 * n2post_ref[...])


def _post(x3, mod3, oh, gate, of, mg, onorm, n1post, n2pre, n2post, wbh, wbf, wout, wup, wdn,
          groups, rows, ff_chunk=1024):
    nb, r_all, d = x3.shape
    nt = r_all // rows
    steps = (nb // groups) * nt
    m = groups * rows
    xmap = lambda s: (s // nt, s % nt, 0)
    mmap = lambda s: (s // nt, 0, 0)
    tmap = lambda s: (s, 0)
    tok = lambda a: pl.BlockSpec((m, a.shape[1]), tmap)
    consts = [onorm, n1post, n2pre, n2post, wbh, wbf, wout, wup, wdn]
    return pl.pallas_call(
        functools.partial(_post_kernel, ff_chunk=ff_chunk),
        grid=(steps,),
        in_specs=[pl.BlockSpec((groups, rows, d), xmap),
                  pl.BlockSpec((groups, 1, mod3.shape[2]), mmap),
                  tok(oh), tok(gate), tok(of), tok(mg)] + [_resident(c.shape) for c in consts],
        out_specs=pl.BlockSpec((groups, rows, d), xmap),
        out_shape=jax.ShapeDtypeStruct(x3.shape, F32),
        compiler_params=_params("arbitrary"),
        name="post",
    )(x3, mod3, oh, gate, of, mg, *consts)


def _bias_place_matrices():
    shape = (2 * N_PIECES, LANES, LANES)
    s = lax.broadcasted_iota(jnp.int32, shape, 0)
    r = lax.broadcasted_iota(jnp.int32, shape, 1)
    c = lax.broadcasted_iota(jnp.int32, shape, 2)
    return jnp.logical_and(r < FOX_HEADS, c == BIAS_SLOTS * r + s).astype(BF16)


def _page_suffix_matrix():
    r = lax.broadcasted_iota(jnp.int32, (PAGE_SIZE, 2 * PAGE_SIZE), 0)
    c = lax.broadcasted_iota(jnp.int32, (PAGE_SIZE, 2 * PAGE_SIZE), 1)
    return jnp.logical_or(r > c, c >= PAGE_SIZE).astype(BF16)


def kernel(x_prompt, x_sample, c_prompt, c_sample, cache_k, cache_v, cache_logf, state_hgrn, page_table,
           ada_w, ada_b, norm_mix_pre, norm_mix_post, norm_mlp_pre, norm_mlp_post, w_in,
           hgrn_lower_bounds, hgrn_onorm, fox_b_f, w_br_h, w_br_f, w_out, w_mlp_up, w_mlp_down):
    batch, seq, d = x_prompt.shape
    nseq, steps, _ = x_sample.shape
    layer = 0
    n_phys = cache_k.shape[1]

    w_t = jnp.transpose(w_in[layer]).astype(BF16)
    n_a = 4 * HG_WIDTH + 3 * FOX_WIDTH
    wa = w_t[:n_a]
    wff = jnp.pad(w_t[n_a:n_a + FOX_HEADS], ((0, LANES - FOX_HEADS), (0, 0)))
    wmg = w_t[n_a + FOX_HEADS:]
    bf128 = jnp.pad(fox_b_f[layer], (0, LANES - FOX_HEADS)).reshape(1, LANES)
    lbraw = hgrn_lower_bounds
    vec3 = lambda v: v.reshape(1, 1, -1)
    wbh, wbf = w_br_h[layer].astype(BF16), w_br_f[layer].astype(BF16)
    wout = w_out[layer].astype(BF16)
    wup, wdn = w_mlp_up[layer].astype(BF16), w_mlp_down[layer].astype(BF16)
    onorm = hgrn_onorm[layer].reshape(1, HG_DV)

    n_c = batch + nseq
    c_pad = -n_c % (2 * SUBLANES)
    c_all = jnp.concatenate([c_prompt, c_sample, jnp.zeros((c_pad, d), F32)], axis=0)
    mod = _ada(c_all, ada_w[layer], ada_b[layer].reshape(1, -1))
    mod_p = mod[:batch].reshape(batch, 1, -1)
    mod_s = mod[batch:n_c].reshape(nseq, 1, -1)

    tm = 512
    sgroups = tm // steps
    proj_args = (vec3(norm_mix_pre[layer]), lbraw, bf128, wa, wff, wmg)
    (q_p, g_p, k_p, v_p, gate_p, fq_p, fk_p, fv_p, lf8_p, lf128_p, mg_p) = _inproj(
        x_prompt, mod_p, *proj_args, groups=1, rows=tm, fox_transposed=True)
    (q_s, g_s, k_s, v_s, gate_s, fq_s, fk_s, fv_s, lf8_s, lf128_s, mg_s) = _inproj(
        x_sample, mod_s, *proj_args, groups=sgroups, rows=steps, fox_transposed=False)

    chunk_p = HG_CHUNK if seq % HG_CHUNK == 0 else seq
    oh_p, s_p = _hgrn_prompt(q_p, k_p, g_p, v_p, batch, seq, chunk_p)
    oh_s, s_s = _hgrn_sample(q_s, k_s, g_s, v_s, state_hgrn[layer], steps)

    tq, tk = 256, 512
    bias_k, bias_qt = _decay(lf128_p.reshape(batch, seq, LANES), _bias_place_matrices())
    ckt = jnp.transpose(cache_k[layer], (0, 2, 3, 1)).reshape(n_phys, FOX_WIDTH, PAGE_SIZE)
    cvt = jnp.transpose(cache_v[layer], (0, 2, 3, 1)).reshape(n_phys, FOX_WIDTH, PAGE_SIZE)
    clt = jnp.transpose(cache_logf[layer], (0, 2, 1))
    per_seq = lambda a: a.reshape(nseq, steps, a.shape[-1])
    of_p, of_s = _fox(fq_p, fk_p, fv_p, bias_k, bias_qt, page_table, per_seq(fq_s), per_seq(fk_s), per_seq(fv_s),
                      per_seq(lf128_s), _page_suffix_matrix(), ckt, cvt, clt, tq, tk)
    of_s = of_s.reshape(nseq * steps, FOX_WIDTH)

    post_args = (onorm, vec3(norm_mix_post[layer]), vec3(norm_mlp_pre[layer]), vec3(norm_mlp_post[layer]),
                 wbh, wbf, wout, wup, wdn)
    y_p = _post(x_prompt, mod_p, oh_p, gate_p, of_p, mg_p, *post_args, groups=1, rows=tm)
    y_s = _post(x_sample, mod_s, oh_s, gate_s, of_s, mg_s, *post_args, groups=sgroups, rows=steps)

    k_prompt = fk_p.reshape(1, batch, FOX_HEADS, FOX_DH, seq).transpose(0, 1, 4, 2, 3)
    v_prompt = fv_p.reshape(1, batch, FOX_HEADS, FOX_DH, seq).transpose(0, 1, 4, 2, 3)
    logf_prompt = lf8_p.reshape(1, batch, seq, FOX_HEADS)
    k_sample = fk_s.reshape(1, nseq, steps, FOX_HEADS, FOX_DH)
    v_sample = fv_s.reshape(1, nseq, steps, FOX_HEADS, FOX_DH)
    logf_sample = lf8_s.reshape(1, nseq, steps, FOX_HEADS)
    return (y_p, y_s, k_prompt, v_prompt, logf_prompt, s_p[None], k_sample, v_sample, logf_sample, s_s[None])
```

```python
import functools

import jax
import jax.numpy as jnp
from jax import lax
from jax.experimental import pallas as pl
from jax.experimental.pallas import tpu as pltpu

F32 = jnp.float32
BF16 = jnp.bfloat16

LANES = 128
SUBLANES = 8
VMEM_LIMIT_BYTES = 56 * 1024 * 1024

D_MODEL = 1024
HG_HEADS = 4
HG_DK = 128
HG_DV = 128
HG_CHUNK = 32
FOX_HEADS = 8
FOX_DH = 64
FOX_WIDTH = FOX_HEADS * FOX_DH
HG_WIDTH = HG_HEADS * HG_DV
PAGE_SIZE = 128
N_MOD = 6
RMS_EPS = 1e-6
NEG_INF = float("-inf")
LOG2E = 1.4426950408889634
N_PIECES = 3
BIAS_SLOTS = 8

_NT = (((1,), (1,)), ((), ()))
_TN = (((0,), (0,)), ((), ()))


def _params(*sem):
    return pltpu.CompilerParams(dimension_semantics=sem, vmem_limit_bytes=VMEM_LIMIT_BYTES)


def _resident(shape):
    nd = len(shape)
    return pl.BlockSpec(shape, lambda *_: (0,) * nd, pipeline_mode=pl.Buffered(1))


def _sigmoid_pair(x):
    t = jnp.exp(-jnp.abs(x))
    r = 1.0 / (1.0 + t)
    tr = t * r
    pos = x >= 0
    return jnp.where(pos, r, tr), jnp.where(pos, tr, r)


def _split3(x):
    hi = x.astype(BF16)
    r1 = x - hi.astype(F32)
    mid = r1.astype(BF16)
    lo = (r1 - mid.astype(F32)).astype(BF16)
    return hi, mid, lo


def _dot_f32_lhs(x, w):
    hi, mid, lo = _split3(x)
    d = lambda p: jnp.dot(p, w, preferred_element_type=F32)
    return (d(lo) + d(mid)) + d(hi)


def _cumsum_rows(x, period):
    row = lax.broadcasted_iota(jnp.int32, x.shape, 0) & (period - 1)
    s = 1
    while s < period:
        x = x + jnp.where(row >= s, pltpu.roll(x, s, axis=0), 0.0)
        s *= 2
    return x


def _rms(x, w):
    return x * lax.rsqrt(jnp.mean(x * x, axis=-1, keepdims=True) + RMS_EPS) * w


def _ada_kernel(c_ref, w_ref, b_ref, o_ref):
    c = c_ref[...]
    s, _ = _sigmoid_pair(c)
    a = (c * s).astype(BF16)
    o_ref[...] = jnp.dot(a, w_ref[...].astype(BF16), preferred_element_type=F32) + b_ref[...]


def _ada(c, w, b, tn=1536):
    m, d = c.shape
    n = w.shape[1]
    return pl.pallas_call(
        _ada_kernel,
        grid=(n // tn,),
        in_specs=[pl.BlockSpec((m, d), lambda j: (0, 0)),
                  pl.BlockSpec((d, tn), lambda j: (0, j)),
                  pl.BlockSpec((1, tn), lambda j: (0, j))],
        out_specs=pl.BlockSpec((m, tn), lambda j: (0, j)),
        out_shape=jax.ShapeDtypeStruct((m, n), F32),
        compiler_params=_params("arbitrary"),
        name="ada",
    )(c, w, b)


def _inproj_kernel(x_ref, mod_ref, n1_ref, lbraw_ref, bf_ref, wa_ref, wff_ref, wmg_ref,
                   q_ref, g_ref, k_ref, v_ref, gate_ref, fq_ref, fk_ref, fv_ref,
                   lf8_ref, lf128_ref, mg_ref, *, fox_transposed):
    g_, r_, d = x_ref.shape
    m = g_ * r_
    x = x_ref[...]
    mod = mod_ref[...]
    sh1 = mod[:, :, 0:d]
    sc1 = mod[:, :, d:2 * d]
    h = _rms(x, n1_ref[...] * (1.0 + sc1)) + sh1
    hb = h.reshape(m, d).astype(BF16)

    raw = lbraw_ref[...]
    e = jnp.exp(raw - jnp.max(raw, axis=0, keepdims=True))
    lb = e[0:1, :] / jnp.sum(e, axis=0, keepdims=True)

    w = HG_WIDTH

    def proj(c):
        return lax.dot_general(hb, wa_ref[c * w:(c + 1) * w, :], _NT, preferred_element_type=F32)

    def proj_t(c):
        return lax.dot_general(wa_ref[c * w:(c + 1) * w, :], hb, _NT, preferred_element_type=F32)

    def merge_gate(c):
        mgc = lax.dot_general(hb, wmg_ref[c * w:(c + 1) * w, :], _NT, preferred_element_type=F32)
        mg_ref[:, c * w:(c + 1) * w] = _sigmoid_pair(mgc)[0].astype(BF16)

    def fox(c, ref, scale=None):
        y = proj_t(c) if fox_transposed else proj(c)
        y = y if scale is None else y * scale
        if fox_transposed:
            ref[0] = y.astype(ref.dtype)
        else:
            ref[...] = y.astype(ref.dtype)

    assert wmg_ref.shape[0] == 4 * w
    merge_gate(0)
    q_ref[...] = proj(0) * (HG_DK ** -0.5)
    merge_gate(1)
    v_ref[...] = proj(2).astype(BF16)
    merge_gate(2)
    fox(4, fq_ref, (LOG2E if fox_transposed else 1.0) * FOX_DH ** -0.5)
    merge_gate(3)
    fox(5, fk_ref)
    s_pos, s_neg = _sigmoid_pair(proj(1))
    g_ref[...] = jnp.log(lb + (1.0 - lb) * s_pos)
    k_ref[...] = (1.0 - lb) * s_neg
    fox(6, fv_ref)
    hg = proj(3)
    gate_ref[...] = (hg * _sigmoid_pair(hg)[0]).astype(BF16)
    z = lax.dot_general(hb, wff_ref[...], _NT, preferred_element_type=F32) + bf_ref[...]
    lf = jnp.minimum(z, 0.0) - jnp.log1p(jnp.exp(-jnp.abs(z)))
    lf128_ref[...] = lf
    lf8_ref[...] = lf[:, 0:FOX_HEADS]


def _inproj(x3, mod3, n1, lbraw, bf128, wa, wff, wmg, groups, rows, fox_transposed):
    assert groups == 1 or not fox_transposed
    nb, r_all, d = x3.shape
    nt = r_all // rows
    steps = (nb // groups) * nt
    t = nb * r_all
    m = groups * rows
    xmap = lambda s: (s // nt, s % nt, 0)
    mmap = lambda s: (s // nt, 0, 0)
    omap = lambda s: (s, 0)

    def out(width, dtype):
        return jax.ShapeDtypeStruct((t, width), dtype), pl.BlockSpec((m, width), omap)

    def fox(dtype):
        if not fox_transposed:
            return out(FOX_WIDTH, dtype)
        return (jax.ShapeDtypeStruct((nb, FOX_WIDTH, r_all), dtype),
                pl.BlockSpec((1, FOX_WIDTH, m), lambda s: (s // nt, 0, s % nt)))

    outs = [out(HG_WIDTH, F32), out(HG_WIDTH, F32), out(HG_WIDTH, F32), out(HG_WIDTH, BF16),
            out(HG_WIDTH, BF16), fox(BF16), fox(F32), fox(F32),
            out(FOX_HEADS, F32), out(LANES, F32), out(wmg.shape[0], BF16)]
    return pl.pallas_call(
        functools.partial(_inproj_kernel, fox_transposed=fox_transposed),
        grid=(steps,),
        in_specs=[pl.BlockSpec((groups, rows, d), xmap),
                  pl.BlockSpec((groups, 1, mod3.shape[2]), mmap),
                  _resident(n1.shape), _resident(lbraw.shape), _resident(bf128.shape),
                  _resident(wa.shape), _resident(wff.shape), _resident(wmg.shape)],
        out_specs=[o[1] for o in outs],
        out_shape=[o[0] for o in outs],
        compiler_params=_params("arbitrary"),
        name="inproj",
    )(x3, mod3, n1, lbraw, bf128, wa, wff, wmg)


def _decay_kernel(lf_ref, place_ref, bk_ref, bqt_ref):
    seq = lf_ref.shape[1]
    lane = lax.broadcasted_iota(jnp.int32, (LANES, LANES), 1)
    slot = lane & (BIAS_SLOTS - 1)
    used = lane < FOX_HEADS * BIAS_SLOTS
    ones_k = jnp.where(jnp.logical_and(used, slot < N_PIECES), 1.0, 0.0)
    ones_q = jnp.where(jnp.logical_and(used, jnp.logical_and(slot >= N_PIECES, slot < 2 * N_PIECES)), 1.0, 0.0)
    carry = jnp.zeros((1, LANES), F32)
    for j in range(seq // LANES):
        sl = slice(j * LANES, (j + 1) * LANES)
        cs = _cumsum_rows(lf_ref[0, sl, :], LANES) + carry
        carry = cs[LANES - 1:LANES, :]
        pieces = _split3(cs * LOG2E)

        def place(base):
            return sum(jnp.dot(pieces[j], place_ref[base + j], preferred_element_type=F32)
                       for j in range(N_PIECES))

        bk_ref[0, sl, :] = (ones_k - place(N_PIECES)).astype(BF16)
        bqt_ref[0, :, sl] = (ones_q + place(0)).T.astype(BF16)


def _decay(lf3, place):
    b, seq, _ = lf3.shape
    return pl.pallas_call(
        _decay_kernel,
        grid=(b,),
        in_specs=[pl.BlockSpec((1, seq, LANES), lambda i: (i, 0, 0)), _resident(place.shape)],
        out_specs=[pl.BlockSpec((1, seq, LANES), lambda i: (i, 0, 0)),
                   pl.BlockSpec((1, LANES, seq), lambda i: (i, 0, 0))],
        out_shape=[jax.ShapeDtypeStruct((b, seq, LANES), BF16),
                   jax.ShapeDtypeStruct((b, LANES, seq), BF16)],
        compiler_params=_params("arbitrary"),
        name="decay",
    )(lf3, place)


def _hgrn_decays(q, k, g, chunk):
    rows = q.shape[0]
    n_chunks = rows // chunk
    b = _cumsum_rows(g, chunk)
    qt = (q * jnp.exp(b)).astype(BF16)
    kt = (k * jnp.exp(-b)).astype(BF16)
    b3 = b.reshape(n_chunks, chunk, HG_DK)
    bl3 = b3[:, chunk - 1:chunk, :]
    kd = (k.reshape(n_chunks, chunk, HG_DK) * jnp.exp(bl3 - b3)).reshape(rows, HG_DK).astype(BF16)
    return qt, kt, kd, jnp.exp(bl3)


def _hgrn_products(qt, kt, kd, v, chunk):
    rows = qt.shape[0]
    n_chunks = rows // chunk
    shift = chunk.bit_length() - 1
    a = lax.dot_general(qt, kt, _NT, preferred_element_type=F32)
    v_t = v.astype(F32).T.astype(BF16)
    col_chunk = lax.broadcasted_iota(jnp.int32, v_t.shape, 1) >> shift
    v_blocks = jnp.concatenate([jnp.where(col_chunk == c, v_t, jnp.zeros_like(v_t)) for c in range(n_chunks)],
                               axis=0)
    incr = jnp.dot(v_blocks, kd, preferred_element_type=F32)
    ri = lax.broadcasted_iota(jnp.int32, (rows, rows), 0)
    ci = lax.broadcasted_iota(jnp.int32, (rows, rows), 1)
    keep = (ri - ci).astype(jnp.uint32) <= (ri & (chunk - 1)).astype(jnp.uint32)
    o_intra = jnp.dot(jnp.where(keep, a, 0.0).astype(BF16), v, preferred_element_type=F32)
    return o_intra, incr


def _hgrn_blocks(blocks, chunk, state_in, state_out):
    n_chunks = LANES // chunk
    decays = [_hgrn_decays(q, k, g, chunk) for (q, k, g, _) in blocks]
    products = [_hgrn_products(qt, kt, kd, blk[3], chunk) for (qt, kt, kd, _), blk in zip(decays, blocks)]
    entering = []
    st = None
    for n in range(len(blocks) * n_chunks):
        j, c = divmod(n, n_chunks)
        st = state_in(n, st)
        entering.append(st.astype(BF16))
        st = st * decays[j][3][c] + products[j][1][c * HG_DV:(c + 1) * HG_DV, :]
        state_out(n, st)
    outs = []
    for j, ((qt, _, _, _), (o_intra, _)) in enumerate(zip(decays, products)):
        o_inter = [lax.dot_general(qt[c * chunk:(c + 1) * chunk, :], entering[j * n_chunks + c], _NT,
                                   preferred_element_type=F32) for c in range(n_chunks)]
        outs.append(o_intra + jnp.concatenate(o_inter, axis=0))
    return outs


HGRN_GROUP = 8


def _hgrn_prompt_kernel(q_ref, k_ref, g_ref, v_ref, o_ref, s_ref, *, chunk):
    seq = q_ref.shape[0]
    group_rows = HGRN_GROUP * LANES

    def body(i, st0):
        base = i * group_rows
        row_slices = [pl.ds(pl.multiple_of(base + j * LANES, LANES), LANES) for j in range(HGRN_GROUP)]
        blocks = [(q_ref[rs, :], k_ref[rs, :], g_ref[rs, :], v_ref[rs, :]) for rs in row_slices]
        last = []
        outs = _hgrn_blocks(blocks, chunk, lambda n, prev: st0 if prev is None else prev,
                            lambda n, st: last.append(st))
        for rs, o in zip(row_slices, outs):
            o_ref[rs, :] = o
        return last[-1]

    st = lax.fori_loop(0, seq // group_rows, body, jnp.zeros((HG_DV, HG_DK), F32))
    s_ref[0, 0] = st.T


def _hgrn_prompt(q, k, g, v, batch, seq, chunk):
    spec = pl.BlockSpec((seq, HG_DK), lambda b, h: (b, h))
    return pl.pallas_call(
        functools.partial(_hgrn_prompt_kernel, chunk=chunk),
        grid=(batch, HG_HEADS),
        in_specs=[spec, spec, spec, spec],
        out_specs=[spec, pl.BlockSpec((1, 1, HG_DK, HG_DV), lambda b, h: (b, h, 0, 0))],
        out_shape=[jax.ShapeDtypeStruct((batch * seq, HG_WIDTH), F32),
                   jax.ShapeDtypeStruct((batch, HG_HEADS, HG_DK, HG_DV), F32)],
        compiler_params=_params("arbitrary", "arbitrary"),
        name="hgrn_prompt",
    )(q, k, g, v)


def _hgrn_sample_kernel(q_ref, k_ref, g_ref, v_ref, s0_ref, o_ref, s_ref, *, chunk):
    per = LANES // chunk

    def state_in(n, prev):
        h, c = divmod(n, per)
        return s0_ref[c, h].T

    def state_out(n, st):
        h, c = divmod(n, per)
        s_ref[c, h] = st.T

    heads = [slice(h * HG_DK, (h + 1) * HG_DK) for h in range(HG_HEADS)]
    outs = _hgrn_blocks([(q_ref[:, hs], k_ref[:, hs], g_ref[:, hs], v_ref[:, hs]) for hs in heads],
                        chunk, state_in, state_out)
    for hs, o in zip(heads, outs):
        o_ref[:, hs] = o


def _hgrn_sample(q, k, g, v, s0, chunk):
    t = q.shape[0]
    per = LANES // chunk
    spec = pl.BlockSpec((LANES, HG_WIDTH), lambda i: (i, 0))
    sspec = pl.BlockSpec((per, HG_HEADS, HG_DK, HG_DV), lambda i: (i, 0, 0, 0))
    return pl.pallas_call(
        functools.partial(_hgrn_sample_kernel, chunk=chunk),
        grid=(t // LANES,),
        in_specs=[spec, spec, spec, spec, sspec],
        out_specs=[spec, sspec],
        out_shape=[jax.ShapeDtypeStruct((t, HG_WIDTH), F32),
                   jax.ShapeDtypeStruct(s0.shape, F32)],
        compiler_params=_params("arbitrary"),
        name="hgrn_sample",
    )(q, k, g, v, s0)


ONES_ROWS = 16
PAGE_PREFETCH_DEPTH = 2


def _prompt_attention(uq_ref, uk_ref, um_ref, qt_ref, kt_ref, vt_ref, bk_ref, bqt_ref, o_ref,
                      kn_ref, va_ref, rhs_ref, mask_ref, s_ref, p_ref, *, pair, tq, tk, n_units, limits):
    seq = kt_ref.shape[1]
    n_heads = LANES // FOX_DH

    def prepare():
        for c in range(seq // LANES):
            cs = slice(c * LANES, (c + 1) * LANES)
            kn_ref[cs, 0:LANES] = kt_ref[:, cs].T.astype(BF16)
        kn_ref[:, LANES:2 * LANES] = bk_ref[...]
        for c in range(seq // tk):
            for e in range(n_heads):
                va_ref[c, e, 0:FOX_DH, :] = vt_ref[e * FOX_DH:(e + 1) * FOX_DH, c * tk:(c + 1) * tk].astype(BF16)
                va_ref[c, e, FOX_DH:FOX_DH + ONES_ROWS, :] = jnp.ones((ONES_ROWS, tk), BF16)
        row = lax.broadcasted_iota(jnp.int32, (LANES, tq), 0)
        for i in range(seq // tq):
            qs = slice(i * tq, (i + 1) * tq)
            qt = qt_ref[:, qs]
            bqt = bqt_ref[:, qs]
            for e in range(n_heads):
                rhs_ref[i, e, 0:LANES, :] = jnp.where(row // FOX_DH == e, qt, jnp.zeros_like(qt))
                rhs_ref[i, e, LANES:2 * LANES, :] = jnp.where(row // BIAS_SLOTS == pair * n_heads + e, bqt,
                                                              jnp.zeros_like(bqt))
        p_ref[1] = jnp.zeros(p_ref.shape[1:], BF16)
        key_minus_query = (lax.broadcasted_iota(jnp.int32, (tk, tq), 0)
                           - lax.broadcasted_iota(jnp.int32, (tk, tq), 1))
        for n, limit in enumerate(limits):
            mask_ref[n] = jnp.where(key_minus_query <= limit, 0.0, NEG_INF)
        rows = lambda n, v: [jnp.full((n, tq), v, F32) for _ in range(n_heads)]
        scores(1, 0)
        return rows(1, NEG_INF), rows(1, 1.0), rows(1, 1.0), rows(FOX_DH, 0.0)

    def scores(u, slot):
        lhs = kn_ref[pl.ds(pl.multiple_of(uk_ref[u] * tk, tk), tk), :]
        for e in range(n_heads):
            s_ref[slot, e] = jnp.dot(lhs, rhs_ref[uq_ref[u], e], preferred_element_type=F32)

    def numerators(u, slot, ms):
        first = uk_ref[u] == 0
        mask = mask_ref[um_ref[u]]
        new_ms, alphas = [], []
        for e in range(n_heads):
            m = jnp.where(first, NEG_INF, ms[e])
            s = s_ref[slot, e] + mask
            m_new = jnp.maximum(m, jnp.max(s, axis=0, keepdims=True))
            p_ref[slot, e] = jnp.exp2(s - m_new).astype(BF16)
            new_ms.append(m_new)
            alphas.append(jnp.exp2(m - m_new))
        return new_ms, alphas

    def values(u, slot, alphas, ls, accs):
        new_ls, new_accs = [], []
        for e in range(n_heads):
            pv = jnp.dot(va_ref[uk_ref[u], e], p_ref[slot, e], preferred_element_type=F32)
            new_ls.append(alphas[e] * ls[e] + pv[FOX_DH:FOX_DH + 1, :])
            new_accs.append(alphas[e] * accs[e] + pv[0:FOX_DH, :])
        o_t = jnp.concatenate([new_accs[e] / new_ls[e] for e in range(n_heads)], axis=0)
        o_ref[pl.ds(pl.multiple_of(uq_ref[u] * tq, tq), tq), :] = o_t.T.astype(BF16)
        return new_ls, new_accs

    def step(u, slot, carry):
        ms, alphas, ls, accs = carry
        new_ls, new_accs = values(u - 1, 1 - slot, alphas, ls, accs)
        scores(u + 1, 1 - slot)
        new_ms, new_alphas = numerators(u, slot, ms)
        return new_ms, new_alphas, new_ls, new_accs

    def body(k, carry):
        carry = step(2 * k + 1, 0, carry)
        return step(2 * k + 2, 1, carry)

    def run(lo, hi, state):
        for k in range(lo, hi):
            state = body(k, state)
        return state

    def finish(state):
        _, alphas, ls, accs = state
        values(n_units, 1, alphas, ls, accs)

    return prepare, run, finish


def _sample_attend(q, k_new, v_new, lf_new, sfx_ref, k_refs, v_refs, lf_refs):
    n_pages = len(k_refs)
    steps = q.shape[0]
    rows = FOX_HEADS * steps
    pad = PAGE_SIZE - steps

    def per_head_rows(x8):
        return jnp.broadcast_to(x8[:, None, :], (FOX_HEADS, steps, x8.shape[1])).reshape(rows, x8.shape[1])

    q = q.astype(F32)
    q_rows = jnp.concatenate([q] * FOX_HEADS, axis=0)
    row_head = lax.broadcasted_iota(jnp.int32, (rows, FOX_WIDTH), 0) // steps
    lane_head = lax.broadcasted_iota(jnp.int32, (rows, FOX_WIDTH), 1) // FOX_DH
    head_mask = row_head == lane_head
    qbd = jnp.where(head_mask, q_rows, 0.0).astype(BF16)

    pn = _cumsum_rows(lf_new, steps)
    pn_rows = jnp.concatenate([pn] * FOX_HEADS, axis=0)
    r_h = lax.broadcasted_iota(jnp.int32, (rows, LANES), 0) // steps
    r_i = lax.broadcasted_iota(jnp.int32, (rows, LANES), 0) % steps
    c_l = lax.broadcasted_iota(jnp.int32, (rows, LANES), 1)
    pcol = jnp.sum(jnp.where(c_l == r_h, pn_rows, 0.0), axis=1, keepdims=True)
    pn_t = jnp.concatenate([pn, jnp.zeros((pad, LANES), F32)], axis=0).T[0:FOX_HEADS, :]

    lf_all = jnp.concatenate([lf_refs[j][...] for j in range(n_pages)], axis=0)
    sfx = _dot_f32_lhs(lf_all, sfx_ref[...])
    carry = jnp.zeros((FOX_HEADS, PAGE_SIZE), F32)
    page_bias = [None] * n_pages
    for j in reversed(range(n_pages)):
        blk = sfx[j * FOX_HEADS:(j + 1) * FOX_HEADS, :]
        page_bias[j] = blk[:, 0:PAGE_SIZE] + carry
        carry = carry + blk[:, PAGE_SIZE:2 * PAGE_SIZE]

    s_tiles = []
    for j in range(n_pages):
        s = jnp.dot(qbd, k_refs[j][...].astype(BF16), preferred_element_type=F32)
        s_tiles.append(s + (per_head_rows(page_bias[j]) + pcol))
    kn = jnp.concatenate([k_new, jnp.zeros((pad, FOX_WIDTH), F32)], axis=0).astype(BF16)
    s_new = lax.dot_general(qbd, kn, _NT, preferred_element_type=F32) + (pcol - per_head_rows(pn_t))
    s_tiles.append(jnp.where(c_l <= r_i, s_new, NEG_INF))

    m_el = s_tiles[0]
    for s in s_tiles[1:]:
        m_el = jnp.maximum(m_el, s)
    m = jnp.max(m_el, axis=1, keepdims=True)
    p_tiles = [jnp.exp(s - m) for s in s_tiles]
    l_el = p_tiles[0]
    for p in p_tiles[1:]:
        l_el = l_el + p
    l = jnp.sum(l_el, axis=1, keepdims=True)

    vn = jnp.concatenate([v_new, jnp.zeros((pad, FOX_WIDTH), F32)], axis=0).astype(BF16)
    o = jnp.dot(p_tiles[n_pages].astype(BF16), vn, preferred_element_type=F32)
    for j in range(n_pages):
        o = o + lax.dot_general(p_tiles[j].astype(BF16), v_refs[j][...].astype(BF16), _NT,
                                preferred_element_type=F32)
    o = jnp.where(head_mask, o / l, 0.0)
    return jnp.sum(o.reshape(FOX_HEADS, steps, FOX_WIDTH), axis=0)


def _fox_kernel(uq_ref, uk_ref, um_ref, pt_ref,
                qt_ref, kt_ref, vt_ref, bk_ref, bqt_ref,
                sq_ref, sk_ref, sv_ref, slf_ref, sfx_ref, ck_hbm, cv_hbm, cl_hbm,
                o_ref, so_ref,
                kn_ref, va_ref, rhs_ref, mask_ref, s_ref, p_ref, k_buf, v_buf, lf_buf, sems,
                *, tq, tk, n_units, limits, n_pages):
    step = pl.program_id(0) * pl.num_programs(1) + pl.program_id(1)
    n_steps = pl.num_programs(0) * pl.num_programs(1)
    per_step = sq_ref.shape[0]
    depth = PAGE_PREFETCH_DEPTH
    assert 0 < depth < per_step
    last_seq = n_steps * per_step - 1

    def page_copies(seq, slot):
        copies = []
        for j in range(n_pages):
            page = pt_ref[seq * n_pages + j]
            copies.append(pltpu.make_async_copy(ck_hbm.at[page], k_buf.at[slot, j], sems.at[0, slot]))
            copies.append(pltpu.make_async_copy(cv_hbm.at[page], v_buf.at[slot, j], sems.at[0, slot]))
            copies.append(pltpu.make_async_copy(cl_hbm.at[page], lf_buf.at[slot, j], sems.at[1, slot]))
        return copies

    @pl.when(step == 0)
    def _():
        for d in range(depth):
            for c in page_copies(d, d):
                c.start()

    prepare, run, finish = _prompt_attention(
        uq_ref, uk_ref, um_ref, qt_ref, kt_ref, vt_ref, bk_ref, bqt_ref, o_ref,
        kn_ref, va_ref, rhs_ref, mask_ref, s_ref, p_ref,
        pair=pl.program_id(1), tq=tq, tk=tk, n_units=n_units, limits=limits)
    state = prepare()
    n_pairs = n_units // 2
    bounds = [(j * n_pairs) // per_step for j in range(per_step + 1)]
    for j in range(per_step):
        seq = step * per_step + j
        slot = j
        ahead = jnp.minimum(seq + depth, last_seq)
        for c in page_copies(ahead, (j + depth) % per_step):
            c.start()
        for c in page_copies(seq, slot):
            c.wait()
        so_ref[j] = _sample_attend(
            sq_ref[j], sk_ref[j], sv_ref[j], slf_ref[j], sfx_ref,
            [k_buf.at[slot, n] for n in range(n_pages)], [v_buf.at[slot, n] for n in range(n_pages)],
            [lf_buf.at[slot, n] for n in range(n_pages)]).astype(BF16)
        state = run(bounds[j], bounds[j + 1], state)
    finish(state)

    @pl.when(step == n_steps - 1)
    def _():
        for d in range(depth):
            for c in page_copies(last_seq, d):
                c.wait()


def _fox(fqt, fkt, fvt, bk, bqt, page_table, sq3, sk3, sv3, slf3, sfx_mat, cache_kt, cache_vt, cache_lft,
         tq, tk):
    batch, _, seq = fqt.shape
    nseq, n_pages = page_table.shape
    steps = sq3.shape[1]
    assert tk % tq == 0 and seq % tk == 0
    nq = seq // tq
    pairs = FOX_WIDTH // LANES
    n_heads = LANES // FOX_DH
    per_step = nseq // (batch * pairs)
    assert per_step * batch * pairs == nseq and per_step > PAGE_PREFETCH_DEPTH
    units = [(i, j) for i in range(nq) for j in range((i * tq) // tk + 1)]
    n_units = len(units)
    assert n_units % 2 == 0
    padded = [units[0]] + units + [units[-1]]
    unit_limits = [min(i * tq - j * tk, tk - 1) for i, j in padded]
    limits = tuple(sorted(set(unit_limits)))
    table = lambda vals: jnp.asarray(vals, jnp.int32)
    uq, uk = table([u[0] for u in padded]), table([u[1] for u in padded])
    um = table([limits.index(v) for v in unit_limits])
    head_pair = lambda b, p, *_: (b, p, 0)
    per_batch = lambda b, p, *_: (b, 0, 0)
    sample = lambda w: pl.BlockSpec((per_step, steps, w), lambda b, p, *_: (b * pairs + p, 0, 0))
    in_hbm = pl.BlockSpec(memory_space=pl.ANY)
    grid_spec = pltpu.PrefetchScalarGridSpec(
        num_scalar_prefetch=4,
        grid=(batch, pairs),
        in_specs=[pl.BlockSpec((None, LANES, seq), head_pair),
                  pl.BlockSpec((None, LANES, seq), head_pair),
                  pl.BlockSpec((None, LANES, seq), head_pair),
                  pl.BlockSpec((None, seq, LANES), per_batch),
                  pl.BlockSpec((None, LANES, seq), per_batch),
                  sample(FOX_WIDTH), sample(FOX_WIDTH), sample(FOX_WIDTH), sample(LANES),
                  pl.BlockSpec(sfx_mat.shape, lambda b, p, *_: (0, 0)), in_hbm, in_hbm, in_hbm],
        out_specs=[pl.BlockSpec((seq, LANES), lambda b, p, *_: (b, p)), sample(FOX_WIDTH)],
        scratch_shapes=[pltpu.VMEM((seq, 2 * LANES), BF16),
                        pltpu.VMEM((seq // tk, n_heads, FOX_DH + ONES_ROWS, tk), BF16),
                        pltpu.VMEM((nq, n_heads, 2 * LANES, tq), BF16),
                        pltpu.VMEM((len(limits), tk, tq), F32),
                        pltpu.VMEM((2, n_heads, tk, tq), F32),
                        pltpu.VMEM((2, n_heads, tk, tq), BF16),
                        pltpu.VMEM((per_step, n_pages, FOX_WIDTH, PAGE_SIZE), cache_kt.dtype),
                        pltpu.VMEM((per_step, n_pages, FOX_WIDTH, PAGE_SIZE), cache_vt.dtype),
                        pltpu.VMEM((per_step, n_pages, FOX_HEADS, PAGE_SIZE), cache_lft.dtype),
                        pltpu.SemaphoreType.DMA((2, per_step))],
    )
    return pl.pallas_call(
        functools.partial(_fox_kernel, tq=tq, tk=tk, n_units=n_units, limits=limits, n_pages=n_pages),
        grid_spec=grid_spec,
        out_shape=[jax.ShapeDtypeStruct((batch * seq, FOX_WIDTH), BF16),
                   jax.ShapeDtypeStruct((nseq, steps, FOX_WIDTH), BF16)],
        compiler_params=_params("arbitrary", "arbitrary"),
        name="fox",
    )(uq, uk, um, page_table.reshape(-1), fqt, fkt, fvt, bk, bqt, sq3, sk3, sv3, slf3, sfx_mat,
      cache_kt, cache_vt, cache_lft)


def _post_kernel(x_ref, mod_ref, oh_ref, gate_ref, of_ref, mg_ref, onorm_ref, n1post_ref, n2pre_ref,
                 n2post_ref, wbh_ref, wbf_ref, wout_ref, wup_ref, wdn_ref, y_ref, *, ff_chunk):
    g_, r_, d = x_ref.shape
    m = g_ * r_
    mod = mod_ref[...]
    gt1 = mod[:, :, 2 * d:3 * d]
    sh2 = mod[:, :, 3 * d:4 * d]
    sc2 = mod[:, :, 4 * d:5 * d]
    gt2 = mod[:, :, 5 * d:6 * d]

    oh = oh_ref[...]
    parts = [_rms(oh[:, h * HG_DV:(h + 1) * HG_DV], onorm_ref[...]) for h in range(HG_HEADS)]
    ohn = (jnp.concatenate(parts, axis=1) * gate_ref[...].astype(F32)).astype(BF16)
    br_h = jnp.dot(ohn, wbh_ref[...], preferred_element_type=F32)
    br_f = jnp.dot(of_ref[...], wbf_ref[...], preferred_element_type=F32)
    mg = mg_ref[...].astype(F32)
    z = (mg[:, 0:d] * br_h + mg[:, d:2 * d] * br_f).astype(BF16)
    y = jnp.dot(z, wout_ref[...], preferred_element_type=F32).reshape(g_, r_, d)
    x1 = x_ref[...] + _rms(y, gt1 * n1post_ref[...])
    h2 = (_rms(x1, n2pre_ref[...] * (1.0 + sc2)) + sh2).reshape(m, d).astype(BF16)
    u = jnp.zeros((m, d), F32)
    for c in range(wup_ref.shape[1] // ff_chunk):
        cs = slice(c * ff_chunk, (c + 1) * ff_chunk)
        a = jnp.maximum(jnp.dot(h2, wup_ref[:, cs], preferred_element_type=F32), 0.0)
        u = u + jnp.dot((a * a).astype(BF16), wdn_ref[cs, :], preferred_element_type=F32)
    y_ref[...] = x1 + _rms(u.reshape(g_, r_, d), gt2 * n2post_ref[...])


def _post(x3, mod3, oh, gate, of, mg, onorm, n1post, n2pre, n2post, wbh, wbf, wout, wup, wdn,
          groups, rows, ff_chunk=1024):
    nb, r_all, d = x3.shape
    nt = r_all // rows
    steps = (nb // groups) * nt
    m = groups * rows
    xmap = lambda s: (s // nt, s % nt, 0)
    mmap = lambda s: (s // nt, 0, 0)
    tmap = lambda s: (s, 0)
    tok = lambda a: pl.BlockSpec((m, a.shape[1]), tmap)
    consts = [onorm, n1post, n2pre, n2post, wbh, wbf, wout, wup, wdn]
    return pl.pallas_call(
        functools.partial(_post_kernel, ff_chunk=ff_chunk),
        grid=(steps,),
        in_specs=[pl.BlockSpec((groups, rows, d), xmap),
                  pl.BlockSpec((groups, 1, mod3.shape[2]), mmap),
                  tok(oh), tok(gate), tok(of), tok(mg)] + [_resident(c.shape) for c in consts],
        out_specs=pl.BlockSpec((groups, rows, d), xmap),
        out_shape=jax.ShapeDtypeStruct(x3.shape, F32),
        compiler_params=_params("arbitrary"),
        name="post",
    )(x3, mod3, oh, gate, of, mg, *consts)


def _bias_place_matrices():
    shape = (2 * N_PIECES, LANES, LANES)
    s = lax.broadcasted_iota(jnp.int32, shape, 0)
    r = lax.broadcasted_iota(jnp.int32, shape, 1)
    c = lax.broadcasted_iota(jnp.int32, shape, 2)
    return jnp.logical_and(r < FOX_HEADS, c == BIAS_SLOTS * r + s).astype(BF16)


def _page_suffix_matrix():
    r = lax.broadcasted_iota(jnp.int32, (PAGE_SIZE, 2 * PAGE_SIZE), 0)
    c = lax.broadcasted_iota(jnp.int32, (PAGE_SIZE, 2 * PAGE_SIZE), 1)
    return jnp.logical_or(r > c, c >= PAGE_SIZE).astype(BF16)


def kernel(x_prompt, x_sample, c_prompt, c_sample, cache_k, cache_v, cache_logf, state_hgrn, page_table,
           ada_w, ada_b, norm_mix_pre, norm_mix_post, norm_mlp_pre, norm_mlp_post, w_in,
           hgrn_lower_bounds, hgrn_onorm, fox_b_f, w_br_h, w_br_f, w_out, w_mlp_up, w_mlp_down):
    batch, seq, d = x_prompt.shape
    nseq, steps, _ = x_sample.shape
    layer = 0
    n_phys = cache_k.shape[1]

    w_t = jnp.transpose(w_in[layer]).astype(BF16)
    n_a = 4 * HG_WIDTH + 3 * FOX_WIDTH
    wa = w_t[:n_a]
    wff = jnp.pad(w_t[n_a:n_a + FOX_HEADS], ((0, LANES - FOX_HEADS), (0, 0)))
    wmg = w_t[n_a + FOX_HEADS:]
    bf128 = jnp.pad(fox_b_f[layer], (0, LANES - FOX_HEADS)).reshape(1, LANES)
    lbraw = hgrn_lower_bounds
    vec3 = lambda v: v.reshape(1, 1, -1)
    wbh, wbf = w_br_h[layer].astype(BF16), w_br_f[layer].astype(BF16)
    wout = w_out[layer].astype(BF16)
    wup, wdn = w_mlp_up[layer].astype(BF16), w_mlp_down[layer].astype(BF16)
    onorm = hgrn_onorm[layer].reshape(1, HG_DV)

    n_c = batch + nseq
    c_pad = -n_c % (2 * SUBLANES)
    c_all = jnp.concatenate([c_prompt, c_sample, jnp.zeros((c_pad, d), F32)], axis=0)
    mod = _ada(c_all, ada_w[layer], ada_b[layer].reshape(1, -1))
    mod_p = mod[:batch].reshape(batch, 1, -1)
    mod_s = mod[batch:n_c].reshape(nseq, 1, -1)

    tm = 512
    sgroups = tm // steps
    proj_args = (vec3(norm_mix_pre[layer]), lbraw, bf128, wa, wff, wmg)
    (q_p, g_p, k_p, v_p, gate_p, fq_p, fk_p, fv_p, lf8_p, lf128_p, mg_p) = _inproj(
        x_prompt, mod_p, *proj_args, groups=1, rows=tm, fox_transposed=True)
    (q_s, g_s, k_s, v_s, gate_s, fq_s, fk_s, fv_s, lf8_s, lf128_s, mg_s) = _inproj(
        x_sample, mod_s, *proj_args, groups=sgroups, rows=steps, fox_transposed=False)

    chunk_p = HG_CHUNK if seq % HG_CHUNK == 0 else seq
    oh_p, s_p = _hgrn_prompt(q_p, k_p, g_p, v_p, batch, seq, chunk_p)
    oh_s, s_s = _hgrn_sample(q_s, k_s, g_s, v_s, state_hgrn[layer], steps)

    tq, tk = 256, 512
    bias_k, bias_qt = _decay(lf128_p.reshape(batch, seq, LANES), _bias_place_matrices())
    ckt = jnp.transpose(cache_k[layer], (0, 2, 3, 1)).reshape(n_phys, FOX_WIDTH, PAGE_SIZE)
    cvt = jnp.transpose(cache_v[layer], (0, 2, 3, 1)).reshape(n_phys, FOX_WIDTH, PAGE_SIZE)
    clt = jnp.transpose(cache_logf[layer], (0, 2, 1))
    per_seq = lambda a: a.reshape(nseq, steps, a.shape[-1])
    of_p, of_s = _fox(fq_p, fk_p, fv_p, bias_k, bias_qt, page_table, per_seq(fq_s), per_seq(fk_s), per_seq(fv_s),
                      per_seq(lf128_s), _page_suffix_matrix(), ckt, cvt, clt, tq, tk)
    of_s = of_s.reshape(nseq * steps, FOX_WIDTH)

    post_args = (onorm, vec3(norm_mix_post[layer]), vec3(norm_mlp_pre[layer]), vec3(norm_mlp_post[layer]),
                 wbh, wbf, wout, wup, wdn)
    y_p = _post(x_prompt, mod_p, oh_p, gate_p, of_p, mg_p, *post_args, groups=1, rows=tm)
    y_s = _post(x_sample, mod_s, oh_s, gate_s, of_s, mg_s, *post_args, groups=sgroups, rows=steps)

    k_prompt = fk_p.reshape(1, batch, FOX_HEADS, FOX_DH, seq).transpose(0, 1, 4, 2, 3)
    v_prompt = fv_p.reshape(1, batch, FOX_HEADS, FOX_DH, seq).transpose(0, 1, 4, 2, 3)
    logf_prompt = lf8_p.reshape(1, batch, seq, FOX_HEADS)
    k_sample = fk_s.reshape(1, nseq, steps, FOX_HEADS, FOX_DH)
    v_sample = fv_s.reshape(1, nseq, steps, FOX_HEADS, FOX_DH)
    logf_sample = lf8_s.reshape(1, nseq, steps, FOX_HEADS)
    return (y_p, y_s, k_prompt, v_prompt, logf_prompt, s_p[None], k_sample, v_sample, logf_sample, s_s[None])
```

```python
import functools

import jax
import jax.numpy as jnp
from jax import lax
from jax.experimental import pallas as pl
from jax.experimental.pallas import tpu as pltpu

F32 = jnp.float32
BF16 = jnp.bfloat16

LANES = 128
SUBLANES = 8
VMEM_LIMIT_BYTES = 56 * 1024 * 1024

D_MODEL = 1024
HG_HEADS = 4
HG_DK = 128
HG_DV = 128
HG_CHUNK = 32
FOX_HEADS = 8
FOX_DH = 64
FOX_WIDTH = FOX_HEADS * FOX_DH
HG_WIDTH = HG_HEADS * HG_DV
PAGE_SIZE = 128
N_MOD = 6
RMS_EPS = 1e-6
NEG_INF = float("-inf")
LOG2E = 1.4426950408889634
N_PIECES = 3
BIAS_SLOTS = 8

_NT = (((1,), (1,)), ((), ()))
_TN = (((0,), (0,)), ((), ()))


def _params(*sem):
    return pltpu.CompilerParams(dimension_semantics=sem, vmem_limit_bytes=VMEM_LIMIT_BYTES)


def _resident(shape):
    nd = len(shape)
    return pl.BlockSpec(shape, lambda *_: (0,) * nd, pipeline_mode=pl.Buffered(1))


def _sigmoid_pair(x):
    t = jnp.exp(-jnp.abs(x))
    r = 1.0 / (1.0 + t)
    tr = t * r
    pos = x >= 0
    return jnp.where(pos, r, tr), jnp.where(pos, tr, r)


def _split3(x):
    hi = x.astype(BF16)
    r1 = x - hi.astype(F32)
    mid = r1.astype(BF16)
    lo = (r1 - mid.astype(F32)).astype(BF16)
    return hi, mid, lo


def _dot_f32_lhs(x, w):
    hi, mid, lo = _split3(x)
    d = lambda p: jnp.dot(p, w, preferred_element_type=F32)
    return (d(lo) + d(mid)) + d(hi)


def _cumsum_rows(x, period):
    row = lax.broadcasted_iota(jnp.int32, x.shape, 0) & (period - 1)
    s = 1
    while s < period:
        x = x + jnp.where(row >= s, pltpu.roll(x, s, axis=0), 0.0)
        s *= 2
    return x


def _rms(x, w):
    return x * lax.rsqrt(jnp.mean(x * x, axis=-1, keepdims=True) + RMS_EPS) * w


def _ada_kernel(c_ref, w_ref, b_ref, o_ref):
    c = c_ref[...]
    s, _ = _sigmoid_pair(c)
    a = (c * s).astype(BF16)
    o_ref[...] = jnp.dot(a, w_ref[...].astype(BF16), preferred_element_type=F32) + b_ref[...]


def _ada(c, w, b, tn=1536):
    m, d = c.shape
    n = w.shape[1]
    return pl.pallas_call(
        _ada_kernel,
        grid=(n // tn,),
        in_specs=[pl.BlockSpec((m, d), lambda j: (0, 0)),
                  pl.BlockSpec((d, tn), lambda j: (0, j)),
                  pl.BlockSpec((1, tn), lambda j: (0, j))],
        out_specs=pl.BlockSpec((m, tn), lambda j: (0, j)),
        out_shape=jax.ShapeDtypeStruct((m, n), F32),
        compiler_params=_params("arbitrary"),
        name="ada",
    )(c, w, b)


def _inproj_kernel(x_ref, mod_ref, n1_ref, lbraw_ref, bf_ref, wa_ref, wff_ref, wmg_ref,
                   q_ref, g_ref, k_ref, v_ref, gate_ref, fq_ref, fk_ref, fv_ref,
                   lf8_ref, lf128_ref, mg_ref, *, fox_transposed):
    g_, r_, d = x_ref.shape
    m = g_ * r_
    x = x_ref[...]
    mod = mod_ref[...]
    sh1 = mod[:, :, 0:d]
    sc1 = mod[:, :, d:2 * d]
    h = _rms(x, n1_ref[...] * (1.0 + sc1)) + sh1
    hb = h.reshape(m, d).astype(BF16)

    raw = lbraw_ref[...]
    e = jnp.exp(raw - jnp.max(raw, axis=0, keepdims=True))
    lb = e[0:1, :] / jnp.sum(e, axis=0, keepdims=True)

    w = HG_WIDTH

    def proj(c):
        return lax.dot_general(hb, wa_ref[c * w:(c + 1) * w, :], _NT, preferred_element_type=F32)

    def proj_t(c):
        return lax.dot_general(wa_ref[c * w:(c + 1) * w, :], hb, _NT, preferred_element_type=F32)

    def merge_gate(c):
        mgc = lax.dot_general(hb, wmg_ref[c * w:(c + 1) * w, :], _NT, preferred_element_type=F32)
        mg_ref[:, c * w:(c + 1) * w] = _sigmoid_pair(mgc)[0].astype(BF16)

    def fox(c, ref, scale=None):
        y = proj_t(c) if fox_transposed else proj(c)
        y = y if scale is None else y * scale
        if fox_transposed:
            ref[0] = y.astype(ref.dtype)
        else:
            ref[...] = y.astype(ref.dtype)

    assert wmg_ref.shape[0] == 4 * w
    merge_gate(0)
    q_ref[...] = proj(0) * (HG_DK ** -0.5)
    merge_gate(1)
    v_ref[...] = proj(2).astype(BF16)
    merge_gate(2)
    fox(4, fq_ref, (LOG2E if fox_transposed else 1.0) * FOX_DH ** -0.5)
    merge_gate(3)
    fox(5, fk_ref)
    s_pos, s_neg = _sigmoid_pair(proj(1))
    g_ref[...] = jnp.log(lb + (1.0 - lb) * s_pos)
    k_ref[...] = (1.0 - lb) * s_neg
    fox(6, fv_ref)
    hg = proj(3)
    gate_ref[...] = (hg * _sigmoid_pair(hg)[0]).astype(BF16)
    z = lax.dot_general(hb, wff_ref[...], _NT, preferred_element_type=F32) + bf_ref[...]
    lf = jnp.minimum(z, 0.0) - jnp.log1p(jnp.exp(-jnp.abs(z)))
    lf128_ref[...] = lf
    lf8_ref[...] = lf[:, 0:FOX_HEADS]


def _inproj(x3, mod3, n1, lbraw, bf128, wa, wff, wmg, groups, rows, fox_transposed):
    assert groups == 1 or not fox_transposed
    nb, r_all, d = x3.shape
    nt = r_all // rows
    steps = (nb // groups) * nt
    t = nb * r_all
    m = groups * rows
    xmap = lambda s: (s // nt, s % nt, 0)
    mmap = lambda s: (s // nt, 0, 0)
    omap = lambda s: (s, 0)

    def out(width, dtype):
        return jax.ShapeDtypeStruct((t, width), dtype), pl.BlockSpec((m, width), omap)

    def fox(dtype):
        if not fox_transposed:
            return out(FOX_WIDTH, dtype)
        return (jax.ShapeDtypeStruct((nb, FOX_WIDTH, r_all), dtype),
                pl.BlockSpec((1, FOX_WIDTH, m), lambda s: (s // nt, 0, s % nt)))

    outs = [out(HG_WIDTH, F32), out(HG_WIDTH, F32), out(HG_WIDTH, F32), out(HG_WIDTH, BF16),
            out(HG_WIDTH, BF16), fox(BF16), fox(F32), fox(F32),
            out(FOX_HEADS, F32), out(LANES, F32), out(wmg.shape[0], BF16)]
    return pl.pallas_call(
        functools.partial(_inproj_kernel, fox_transposed=fox_transposed),
        grid=(steps,),
        in_specs=[pl.BlockSpec((groups, rows, d), xmap),
                  pl.BlockSpec((groups, 1, mod3.shape[2]), mmap),
                  _resident(n1.shape), _resident(lbraw.shape), _resident(bf128.shape),
                  _resident(wa.shape), _resident(wff.shape), _resident(wmg.shape)],
        out_specs=[o[1] for o in outs],
        out_shape=[o[0] for o in outs],
        compiler_params=_params("arbitrary"),
        name="inproj",
    )(x3, mod3, n1, lbraw, bf128, wa, wff, wmg)


def _decay_kernel(lf_ref, place_ref, bk_ref, bqt_ref):
    seq = lf_ref.shape[1]
    lane = lax.broadcasted_iota(jnp.int32, (LANES, LANES), 1)
    slot = lane & (BIAS_SLOTS - 1)
    used = lane < FOX_HEADS * BIAS_SLOTS
    ones_k = jnp.where(jnp.logical_and(used, slot < N_PIECES), 1.0, 0.0)
    ones_q = jnp.where(jnp.logical_and(used, jnp.logical_and(slot >= N_PIECES, slot < 2 * N_PIECES)), 1.0, 0.0)
    carry = jnp.zeros((1, LANES), F32)
    for j in range(seq // LANES):
        sl = slice(j * LANES, (j + 1) * LANES)
        cs = _cumsum_rows(lf_ref[0, sl, :], LANES) + carry
        carry = cs[LANES - 1:LANES, :]
        pieces = _split3(cs * LOG2E)

        def place(base):
            return sum(jnp.dot(pieces[j], place_ref[base + j], preferred_element_type=F32)
                       for j in range(N_PIECES))

        bk_ref[0, sl, :] = (ones_k - place(N_PIECES)).astype(BF16)
        bqt_ref[0, :, sl] = (ones_q + place(0)).T.astype(BF16)


def _decay(lf3, place):
    b, seq, _ = lf3.shape
    return pl.pallas_call(
        _decay_kernel,
        grid=(b,),
        in_specs=[pl.BlockSpec((1, seq, LANES), lambda i: (i, 0, 0)), _resident(place.shape)],
        out_specs=[pl.BlockSpec((1, seq, LANES), lambda i: (i, 0, 0)),
                   pl.BlockSpec((1, LANES, seq), lambda i: (i, 0, 0))],
        out_shape=[jax.ShapeDtypeStruct((b, seq, LANES), BF16),
                   jax.ShapeDtypeStruct((b, LANES, seq), BF16)],
        compiler_params=_params("arbitrary"),
        name="decay",
    )(lf3, place)


def _hgrn_decays(q, k, g, chunk):
    rows = q.shape[0]
    n_chunks = rows // chunk
    b = _cumsum_rows(g, chunk)
    qt = (q * jnp.exp(b)).astype(BF16)
    kt = (k * jnp.exp(-b)).astype(BF16)
    b3 = b.reshape(n_chunks, chunk, HG_DK)
    bl3 = b3[:, chunk - 1:chunk, :]
    kd = (k.reshape(n_chunks, chunk, HG_DK) * jnp.exp(bl3 - b3)).reshape(rows, HG_DK).astype(BF16)
    return qt, kt, kd, jnp.exp(bl3)


def _hgrn_products(qt, kt, kd, v, chunk):
    rows = qt.shape[0]
    n_chunks = rows // chunk
    shift = chunk.bit_length() - 1
    a = lax.dot_general(qt, kt, _NT, preferred_element_type=F32)
    v_t = v.astype(F32).T.astype(BF16)
    col_chunk = lax.broadcasted_iota(jnp.int32, v_t.shape, 1) >> shift
    v_blocks = jnp.concatenate([jnp.where(col_chunk == c, v_t, jnp.zeros_like(v_t)) for c in range(n_chunks)],
                               axis=0)
    incr = jnp.dot(v_blocks, kd, preferred_element_type=F32)
    ri = lax.broadcasted_iota(jnp.int32, (rows, rows), 0)
    ci = lax.broadcasted_iota(jnp.int32, (rows, rows), 1)
    keep = (ri - ci).astype(jnp.uint32) <= (ri & (chunk - 1)).astype(jnp.uint32)
    o_intra = jnp.dot(jnp.where(keep, a, 0.0).astype(BF16), v, preferred_element_type=F32)
    return o_intra, incr


def _hgrn_blocks(blocks, chunk, state_in, state_out):
    n_chunks = LANES // chunk
    decays = [_hgrn_decays(q, k, g, chunk) for (q, k, g, _) in blocks]
    products = [_hgrn_products(qt, kt, kd, blk[3], chunk) for (qt, kt, kd, _), blk in zip(decays, blocks)]
    entering = []
    st = None
    for n in range(len(blocks) * n_chunks):
        j, c = divmod(n, n_chunks)
        st = state_in(n, st)
        entering.append(st.astype(BF16))
        st = st * decays[j][3][c] + products[j][1][c * HG_DV:(c + 1) * HG_DV, :]
        state_out(n, st)
    outs = []
    for j, ((qt, _, _, _), (o_intra, _)) in enumerate(zip(decays, products)):
        o_inter = [lax.dot_general(qt[c * chunk:(c + 1) * chunk, :], entering[j * n_chunks + c], _NT,
                                   preferred_element_type=F32) for c in range(n_chunks)]
        outs.append(o_intra + jnp.concatenate(o_inter, axis=0))
    return outs


HGRN_GROUP = 8


def _hgrn_prompt_kernel(q_ref, k_ref, g_ref, v_ref, o_ref, s_ref, *, chunk):
    seq = q_ref.shape[0]
    group_rows = HGRN_GROUP * LANES

    def body(i, st0):
        base = i * group_rows
        row_slices = [pl.ds(pl.multiple_of(base + j * LANES, LANES), LANES) for j in range(HGRN_GROUP)]
        blocks = [(q_ref[rs, :], k_ref[rs, :], g_ref[rs, :], v_ref[rs, :]) for rs in row_slices]
        last = []
        outs = _hgrn_blocks(blocks, chunk, lambda n, prev: st0 if prev is None else prev,
                            lambda n, st: last.append(st))
        for rs, o in zip(row_slices, outs):
            o_ref[rs, :] = o
        return last[-1]

    st = lax.fori_loop(0, seq // group_rows, body, jnp.zeros((HG_DV, HG_DK), F32))
    s_ref[0, 0] = st.T


def _hgrn_prompt(q, k, g, v, batch, seq, chunk):
    spec = pl.BlockSpec((seq, HG_DK), lambda b, h: (b, h))
    return pl.pallas_call(
        functools.partial(_hgrn_prompt_kernel, chunk=chunk),
        grid=(batch, HG_HEADS),
        in_specs=[spec, spec, spec, spec],
        out_specs=[spec, pl.BlockSpec((1, 1, HG_DK, HG_DV), lambda b, h: (b, h, 0, 0))],
        out_shape=[jax.ShapeDtypeStruct((batch * seq, HG_WIDTH), F32),
                   jax.ShapeDtypeStruct((batch, HG_HEADS, HG_DK, HG_DV), F32)],
        compiler_params=_params("arbitrary", "arbitrary"),
        name="hgrn_prompt",
    )(q, k, g, v)


def _hgrn_sample_kernel(q_ref, k_ref, g_ref, v_ref, s0_ref, o_ref, s_ref, *, chunk):
    per = LANES // chunk

    def state_in(n, prev):
        h, c = divmod(n, per)
        return s0_ref[c, h].T

    def state_out(n, st):
        h, c = divmod(n, per)
        s_ref[c, h] = st.T

    heads = [slice(h * HG_DK, (h + 1) * HG_DK) for h in range(HG_HEADS)]
    outs = _hgrn_blocks([(q_ref[:, hs], k_ref[:, hs], g_ref[:, hs], v_ref[:, hs]) for hs in heads],
                        chunk, state_in, state_out)
    for hs, o in zip(heads, outs):
        o_ref[:, hs] = o


def _hgrn_sample(q, k, g, v, s0, chunk):
    t = q.shape[0]
    per = LANES // chunk
    spec = pl.BlockSpec((LANES, HG_WIDTH), lambda i: (i, 0))
    sspec = pl.BlockSpec((per, HG_HEADS, HG_DK, HG_DV), lambda i: (i, 0, 0, 0))
    return pl.pallas_call(
        functools.partial(_hgrn_sample_kernel, chunk=chunk),
        grid=(t // LANES,),
        in_specs=[spec, spec, spec, spec, sspec],
        out_specs=[spec, sspec],
        out_shape=[jax.ShapeDtypeStruct((t, HG_WIDTH), F32),
                   jax.ShapeDtypeStruct(s0.shape, F32)],
        compiler_params=_params("arbitrary"),
        name="hgrn_sample",
    )(q, k, g, v, s0)


ONES_ROWS = 16
PAGE_PREFETCH_DEPTH = 2


def _prompt_attention(uq_ref, uk_ref, um_ref, qt_ref, kt_ref, vt_ref, bk_ref, bqt_ref, o_ref,
                      kn_ref, va_ref, rhs_ref, mask_ref, s_ref, p_ref, *, pair, tq, tk, n_units, limits):
    seq = kt_ref.shape[1]
    n_heads = LANES // FOX_DH

    def prepare():
        for c in range(seq // LANES):
            cs = slice(c * LANES, (c + 1) * LANES)
            kn_ref[cs, 0:LANES] = kt_ref[:, cs].T.astype(BF16)
        kn_ref[:, LANES:2 * LANES] = bk_ref[...]
        for c in range(seq // tk):
            for e in range(n_heads):
                va_ref[c, e, 0:FOX_DH, :] = vt_ref[e * FOX_DH:(e + 1) * FOX_DH, c * tk:(c + 1) * tk].astype(BF16)
                va_ref[c, e, FOX_DH:FOX_DH + ONES_ROWS, :] = jnp.ones((ONES_ROWS, tk), BF16)
        row = lax.broadcasted_iota(jnp.int32, (LANES, tq), 0)
        for i in range(seq // tq):
            qs = slice(i * tq, (i + 1) * tq)
            qt = qt_ref[:, qs]
            bqt = bqt_ref[:, qs]
            for e in range(n_heads):
                rhs_ref[i, e, 0:LANES, :] = jnp.where(row // FOX_DH == e, qt, jnp.zeros_like(qt))
                rhs_ref[i, e, LANES:2 * LANES, :] = jnp.where(row // BIAS_SLOTS == pair * n_heads + e, bqt,
                                                              jnp.zeros_like(bqt))
        p_ref[1] = jnp.zeros(p_ref.shape[1:], BF16)
        key_minus_query = (lax.broadcasted_iota(jnp.int32, (tk, tq), 0)
                           - lax.broadcasted_iota(jnp.int32, (tk, tq), 1))
        for n, limit in enumerate(limits):
            mask_ref[n] = jnp.where(key_minus_query <= limit, 0.0, NEG_INF)
        rows = lambda n, v: [jnp.full((n, tq), v, F32) for _ in range(n_heads)]
        scores(1, 0)
        return rows(1, NEG_INF), rows(1, 1.0), rows(1, 1.0), rows(FOX_DH, 0.0)

    def scores(u, slot):
        lhs = kn_ref[pl.ds(pl.multiple_of(uk_ref[u] * tk, tk), tk), :]
        for e in range(n_heads):
            s_ref[slot, e] = jnp.dot(lhs, rhs_ref[uq_ref[u], e], preferred_element_type=F32)

    def numerators(u, slot, ms):
        first = uk_ref[u] == 0
        mask = mask_ref[um_ref[u]]
        new_ms, alphas = [], []
        for e in range(n_heads):
            m = jnp.where(first, NEG_INF, ms[e])
            s = s_ref[slot, e] + mask
            m_new = jnp.maximum(m, jnp.max(s, axis=0, keepdims=True))
            p_ref[slot, e] = jnp.exp2(s - m_new).astype(BF16)
            new_ms.append(m_new)
            alphas.append(jnp.exp2(m - m_new))
        return new_ms, alphas

    def values(u, slot, alphas, ls, accs):
        new_ls, new_accs = [], []
        for e in range(n_heads):
            pv = jnp.dot(va_ref[uk_ref[u], e], p_ref[slot, e], preferred_element_type=F32)
            new_ls.append(alphas[e] * ls[e] + pv[FOX_DH:FOX_DH + 1, :])
            new_accs.append(alphas[e] * accs[e] + pv[0:FOX_DH, :])
        o_t = jnp.concatenate([new_accs[e] / new_ls[e] for e in range(n_heads)], axis=0)
        o_ref[pl.ds(pl.multiple_of(uq_ref[u] * tq, tq), tq), :] = o_t.T.astype(BF16)
        return new_ls, new_accs

    def step(u, slot, carry):
        ms, alphas, ls, accs = carry
        new_ls, new_accs = values(u - 1, 1 - slot, alphas, ls, accs)
        scores(u + 1, 1 - slot)
        new_ms, new_alphas = numerators(u, slot, ms)
        return new_ms, new_alphas, new_ls, new_accs

    def body(k, carry):
        carry = step(2 * k + 1, 0, carry)
        return step(2 * k + 2, 1, carry)

    def run(lo, hi, state):
        for k in range(lo, hi):
            state = body(k, state)
        return state

    def finish(state):
        _, alphas, ls, accs = state
        values(n_units, 1, alphas, ls, accs)

    return prepare, run, finish


def _sample_attend(q, k_new, v_new, lf_new, sfx_ref, k_refs, v_refs, lf_refs):
    n_pages = len(k_refs)
    steps = q.shape[0]
    rows = FOX_HEADS * steps
    pad = PAGE_SIZE - steps

    def per_head_rows(x8):
        return jnp.broadcast_to(x8[:, None, :], (FOX_HEADS, steps, x8.shape[1])).reshape(rows, x8.shape[1])

    q = q.astype(F32)
    q_rows = jnp.concatenate([q] * FOX_HEADS, axis=0)
    row_head = lax.broadcasted_iota(jnp.int32, (rows, FOX_WIDTH), 0) // steps
    lane_head = lax.broadcasted_iota(jnp.int32, (rows, FOX_WIDTH), 1) // FOX_DH
    head_mask = row_head == lane_head
    qbd = jnp.where(head_mask, q_rows, 0.0).astype(BF16)

    pn = _cumsum_rows(lf_new, steps)
    pn_rows = jnp.concatenate([pn] * FOX_HEADS, axis=0)
    r_h = lax.broadcasted_iota(jnp.int32, (rows, LANES), 0) // steps
    r_i = lax.broadcasted_iota(jnp.int32, (rows, LANES), 0) % steps
    c_l = lax.broadcasted_iota(jnp.int32, (rows, LANES), 1)
    pcol = jnp.sum(jnp.where(c_l == r_h, pn_rows, 0.0), axis=1, keepdims=True)
    pn_t = jnp.concatenate([pn, jnp.zeros((pad, LANES), F32)], axis=0).T[0:FOX_HEADS, :]

    lf_all = jnp.concatenate([lf_refs[j][...] for j in range(n_pages)], axis=0)
    sfx = _dot_f32_lhs(lf_all, sfx_ref[...])
    carry = jnp.zeros((FOX_HEADS, PAGE_SIZE), F32)
    page_bias = [None] * n_pages
    for j in reversed(range(n_pages)):
        blk = sfx[j * FOX_HEADS:(j + 1) * FOX_HEADS, :]
        page_bias[j] = blk[:, 0:PAGE_SIZE] + carry
        carry = carry + blk[:, PAGE_SIZE:2 * PAGE_SIZE]

    s_tiles = []
    for j in range(n_pages):
        s = jnp.dot(qbd, k_refs[j][...].astype(BF16), preferred_element_type=F32)
        s_tiles.append(s + (per_head_rows(page_bias[j]) + pcol))
    kn = jnp.concatenate([k_new, jnp.zeros((pad, FOX_WIDTH), F32)], axis=0).astype(BF16)
    s_new = lax.dot_general(qbd, kn, _NT, preferred_element_type=F32) + (pcol - per_head_rows(pn_t))
    s_tiles.append(jnp.where(c_l <= r_i, s_new, NEG_INF))

    m_el = s_tiles[0]
    for s in s_tiles[1:]:
        m_el = jnp.maximum(m_el, s)
    m = jnp.max(m_el, axis=1, keepdims=True)
    p_tiles = [jnp.exp(s - m) for s in s_tiles]
    l_el = p_tiles[0]
    for p in p_tiles[1:]:
        l_el = l_el + p
    l = jnp.sum(l_el, axis=1, keepdims=True)

    vn = jnp.concatenate([v_new, jnp.zeros((pad, FOX_WIDTH), F32)], axis=0).astype(BF16)
    o = jnp.dot(p_tiles[n_pages].astype(BF16), vn, preferred_element_type=F32)
    for j in range(n_pages):
        o = o + lax.dot_general(p_tiles[j].astype(BF16), v_refs[j][...].astype(BF16), _NT,
                                preferred_element_type=F32)
    o = jnp.where(head_mask, o / l, 0.0)
    return jnp.sum(o.reshape(FOX_HEADS, steps, FOX_WIDTH), axis=0)


def _fox_kernel(uq_ref, uk_ref, um_ref, pt_ref,
                qt_ref, kt_ref, vt_ref, bk_ref, bqt_ref,
                sq_ref, sk_ref, sv_ref, slf_ref, sfx_ref, ck_hbm, cv_hbm, cl_hbm,
                o_ref, so_ref,
                kn_ref, va_ref, rhs_ref, mask_ref, s_ref, p_ref, k_buf, v_buf, lf_buf, sems,
                *, tq, tk, n_units, limits, n_pages):
    step = pl.program_id(0) * pl.num_programs(1) + pl.program_id(1)
    n_steps = pl.num_programs(0) * pl.num_programs(1)
    per_step = sq_ref.shape[0]
    depth = PAGE_PREFETCH_DEPTH
    assert 0 < depth < per_step
    last_seq = n_steps * per_step - 1

    def page_copies(seq, slot):
        copies = []
        for j in range(n_pages):
            page = pt_ref[seq * n_pages + j]
            copies.append(pltpu.make_async_copy(ck_hbm.at[page], k_buf.at[slot, j], sems.at[0, slot]))
            copies.append(pltpu.make_async_copy(cv_hbm.at[page], v_buf.at[slot, j], sems.at[0, slot]))
            copies.append(pltpu.make_async_copy(cl_hbm.at[page], lf_buf.at[slot, j], sems.at[1, slot]))
        return copies

    @pl.when(step == 0)
    def _():
        for d in range(depth):
            for c in page_copies(d, d):
                c.start()

    prepare, run, finish = _prompt_attention(
        uq_ref, uk_ref, um_ref, qt_ref, kt_ref, vt_ref, bk_ref, bqt_ref, o_ref,
        kn_ref, va_ref, rhs_ref, mask_ref, s_ref, p_ref,
        pair=pl.program_id(1), tq=tq, tk=tk, n_units=n_units, limits=limits)
    state = prepare()
    n_pairs = n_units // 2
    bounds = [(j * n_pairs) // per_step for j in range(per_step + 1)]
    for j in range(per_step):
        seq = step * per_step + j
        slot = j
        ahead = jnp.minimum(seq + depth, last_seq)
        for i, c in enumerate(page_copies(ahead, (j + depth) % per_step)):
            c.start(priority=i % 2)
        for c in page_copies(seq, slot):
            c.wait()
        so_ref[j] = _sample_attend(
            sq_ref[j], sk_ref[j], sv_ref[j], slf_ref[j], sfx_ref,
            [k_buf.at[slot, n] for n in range(n_pages)], [v_buf.at[slot, n] for n in range(n_pages)],
            [lf_buf.at[slot, n] for n in range(n_pages)]).astype(BF16)
        state = run(bounds[j], bounds[j + 1], state)
    finish(state)

    @pl.when(step == n_steps - 1)
    def _():
        for d in range(depth):
            for c in page_copies(last_seq, d):
                c.wait()


def _fox(fqt, fkt, fvt, bk, bqt, page_table, sq3, sk3, sv3, slf3, sfx_mat, cache_kt, cache_vt, cache_lft,
         tq, tk):
    batch, _, seq = fqt.shape
    nseq, n_pages = page_table.shape
    steps = sq3.shape[1]
    assert tk % tq == 0 and seq % tk == 0
    nq = seq // tq
    pairs = FOX_WIDTH // LANES
    n_heads = LANES // FOX_DH
    per_step = nseq // (batch * pairs)
    assert per_step * batch * pairs == nseq and per_step > PAGE_PREFETCH_DEPTH
    units = [(i, j) for i in range(nq) for j in range((i * tq) // tk + 1)]
    n_units = len(units)
    assert n_units % 2 == 0
    padded = [units[0]] + units + [units[-1]]
    unit_limits = [min(i * tq - j * tk, tk - 1) for i, j in padded]
    limits = tuple(sorted(set(unit_limits)))
    table = lambda vals: jnp.asarray(vals, jnp.int32)
    uq, uk = table([u[0] for u in padded]), table([u[1] for u in padded])
    um = table([limits.index(v) for v in unit_limits])
    head_pair = lambda b, p, *_: (b, p, 0)
    per_batch = lambda b, p, *_: (b, 0, 0)
    sample = lambda w: pl.BlockSpec((per_step, steps, w), lambda b, p, *_: (b * pairs + p, 0, 0))
    in_hbm = pl.BlockSpec(memory_space=pl.ANY)
    grid_spec = pltpu.PrefetchScalarGridSpec(
        num_scalar_prefetch=4,
        grid=(batch, pairs),
        in_specs=[pl.BlockSpec((None, LANES, seq), head_pair),
                  pl.BlockSpec((None, LANES, seq), head_pair),
                  pl.BlockSpec((None, LANES, seq), head_pair),
                  pl.BlockSpec((None, seq, LANES), per_batch),
                  pl.BlockSpec((None, LANES, seq), per_batch),
                  sample(FOX_WIDTH), sample(FOX_WIDTH), sample(FOX_WIDTH), sample(LANES),
                  pl.BlockSpec(sfx_mat.shape, lambda b, p, *_: (0, 0)), in_hbm, in_hbm, in_hbm],
        out_specs=[pl.BlockSpec((seq, LANES), lambda b, p, *_: (b, p)), sample(FOX_WIDTH)],
        scratch_shapes=[pltpu.VMEM((seq, 2 * LANES), BF16),
                        pltpu.VMEM((seq // tk, n_heads, FOX_DH + ONES_ROWS, tk), BF16),
                        pltpu.VMEM((nq, n_heads, 2 * LANES, tq), BF16),
                        pltpu.VMEM((len(limits), tk, tq), F32),
                        pltpu.VMEM((2, n_heads, tk, tq), F32),
                        pltpu.VMEM((2, n_heads, tk, tq), BF16),
                        pltpu.VMEM((per_step, n_pages, FOX_WIDTH, PAGE_SIZE), cache_kt.dtype),
                        pltpu.VMEM((per_step, n_pages, FOX_WIDTH, PAGE_SIZE), cache_vt.dtype),
                        pltpu.VMEM((per_step, n_pages, FOX_HEADS, PAGE_SIZE), cache_lft.dtype),
                        pltpu.SemaphoreType.DMA((2, per_step))],
    )
    return pl.pallas_call(
        functools.partial(_fox_kernel, tq=tq, tk=tk, n_units=n_units, limits=limits, n_pages=n_pages),
        grid_spec=grid_spec,
        out_shape=[jax.ShapeDtypeStruct((batch * seq, FOX_WIDTH), BF16),
                   jax.ShapeDtypeStruct((nseq, steps, FOX_WIDTH), BF16)],
        compiler_params=_params("arbitrary", "arbitrary"),
        name="fox",
    )(uq, uk, um, page_table.reshape(-1), fqt, fkt, fvt, bk, bqt, sq3, sk3, sv3, slf3, sfx_mat,
      cache_kt, cache_vt, cache_lft)


def _post_kernel(x_ref, mod_ref, oh_ref, gate_ref, of_ref, mg_ref, onorm_ref, n1post_ref, n2pre_ref,
                 n2post_ref, wbh_ref, wbf_ref, wout_ref, wup_ref, wdn_ref, y_ref, *, ff_chunk):
    g_, r_, d = x_ref.shape
    m = g_ * r_
    mod = mod_ref[...]
    gt1 = mod[:, :, 2 * d:3 * d]
    sh2 = mod[:, :, 3 * d:4 * d]
    sc2 = mod[:, :, 4 * d:5 * d]
    gt2 = mod[:, :, 5 * d:6 * d]

    oh = oh_ref[...]
    parts = [_rms(oh[:, h * HG_DV:(h + 1) * HG_DV], onorm_ref[...]) for h in range(HG_HEADS)]
    ohn = (jnp.concatenate(parts, axis=1) * gate_ref[...].astype(F32)).astype(BF16)
    br_h = jnp.dot(ohn, wbh_ref[...], preferred_element_type=F32)
    br_f = jnp.dot(of_ref[...], wbf_ref[...], preferred_element_type=F32)
    mg = mg_ref[...].astype(F32)
    z = (mg[:, 0:d] * br_h + mg[:, d:2 * d] * br_f).astype(BF16)
    y = jnp.dot(z, wout_ref[...], preferred_element_type=F32).reshape(g_, r_, d)
    x1 = x_ref[...] + _rms(y, gt1 * n1post_ref[...])
    h2 = (_rms(x1, n2pre_ref[...] * (1.0 + sc2)) + sh2).reshape(m, d).astype(BF16)
    u = jnp.zeros((m, d), F32)
    for c in range(wup_ref.shape[1] // ff_chunk):
        cs = slice(c * ff_chunk, (c + 1) * ff_chunk)
        a = jnp.maximum(jnp.dot(h2, wup_ref[:, cs], preferred_element_type=F32), 0.0)
        u = u + jnp.dot((a * a).astype(BF16), wdn_ref[cs, :], preferred_element_type=F32)
    y_ref[...] = x1 + _rms(u.reshape(g_, r_, d), gt2 * n2post_ref[...])


def _post(x3, mod3, oh, gate, of, mg, onorm, n1post, n2pre, n2post, wbh, wbf, wout, wup, wdn,
          groups, rows, ff_chunk=1024):
    nb, r_all, d = x3.shape
    nt = r_all // rows
    steps = (nb // groups) * nt
    m = groups * rows
    xmap = lambda s: (s // nt, s % nt, 0)
    mmap = lambda s: (s // nt, 0, 0)
    tmap = lambda s: (s, 0)
    tok = lambda a: pl.BlockSpec((m, a.shape[1]), tmap)
    consts = [onorm, n1post, n2pre, n2post, wbh, wbf, wout, wup, wdn]
    return pl.pallas_call(
        functools.partial(_post_kernel, ff_chunk=ff_chunk),
        grid=(steps,),
        in_specs=[pl.BlockSpec((groups, rows, d), xmap),
                  pl.BlockSpec((groups, 1, mod3.shape[2]), mmap),
                  tok(oh), tok(gate), tok(of), tok(mg)] + [_resident(c.shape) for c in consts],
        out_specs=pl.BlockSpec((groups, rows, d), xmap),
        out_shape=jax.ShapeDtypeStruct(x3.shape, F32),
        compiler_params=_params("arbitrary"),
        name="post",
    )(x3, mod3, oh, gate, of, mg, *consts)


def _bias_place_matrices():
    shape = (2 * N_PIECES, LANES, LANES)
    s = lax.broadcasted_iota(jnp.int32, shape, 0)
    r = lax.broadcasted_iota(jnp.int32, shape, 1)
    c = lax.broadcasted_iota(jnp.int32, shape, 2)
    return jnp.logical_and(r < FOX_HEADS, c == BIAS_SLOTS * r + s).astype(BF16)


def _page_suffix_matrix():
    r = lax.broadcasted_iota(jnp.int32, (PAGE_SIZE, 2 * PAGE_SIZE), 0)
    c = lax.broadcasted_iota(jnp.int32, (PAGE_SIZE, 2 * PAGE_SIZE), 1)
    return jnp.logical_or(r > c, c >= PAGE_SIZE).astype(BF16)


def kernel(x_prompt, x_sample, c_prompt, c_sample, cache_k, cache_v, cache_logf, state_hgrn, page_table,
           ada_w, ada_b, norm_mix_pre, norm_mix_post, norm_mlp_pre, norm_mlp_post, w_in,
           hgrn_lower_bounds, hgrn_onorm, fox_b_f, w_br_h, w_br_f, w_out, w_mlp_up, w_mlp_down):
    batch, seq, d = x_prompt.shape
    nseq, steps, _ = x_sample.shape
    layer = 0
    n_phys = cache_k.shape[1]

    w_t = jnp.transpose(w_in[layer]).astype(BF16)
    n_a = 4 * HG_WIDTH + 3 * FOX_WIDTH
    wa = w_t[:n_a]
    wff = jnp.pad(w_t[n_a:n_a + FOX_HEADS], ((0, LANES - FOX_HEADS), (0, 0)))
    wmg = w_t[n_a + FOX_HEADS:]
    bf128 = jnp.pad(fox_b_f[layer], (0, LANES - FOX_HEADS)).reshape(1, LANES)
    lbraw = hgrn_lower_bounds
    vec3 = lambda v: v.reshape(1, 1, -1)
    wbh, wbf = w_br_h[layer].astype(BF16), w_br_f[layer].astype(BF16)
    wout = w_out[layer].astype(BF16)
    wup, wdn = w_mlp_up[layer].astype(BF16), w_mlp_down[layer].astype(BF16)
    onorm = hgrn_onorm[layer].reshape(1, HG_DV)

    n_c = batch + nseq
    c_pad = -n_c % (2 * SUBLANES)
    c_all = jnp.concatenate([c_prompt, c_sample, jnp.zeros((c_pad, d), F32)], axis=0)
    mod = _ada(c_all, ada_w[layer], ada_b[layer].reshape(1, -1))
    mod_p = mod[:batch].reshape(batch, 1, -1)
    mod_s = mod[batch:n_c].reshape(nseq, 1, -1)

    tm = 512
    sgroups = tm // steps
    proj_args = (vec3(norm_mix_pre[layer]), lbraw, bf128, wa, wff, wmg)
    (q_p, g_p, k_p, v_p, gate_p, fq_p, fk_p, fv_p, lf8_p, lf128_p, mg_p) = _inproj(
        x_prompt, mod_p, *proj_args, groups=1, rows=tm, fox_transposed=True)
    (q_s, g_s, k_s, v_s, gate_s, fq_s, fk_s, fv_s, lf8_s, lf128_s, mg_s) = _inproj(
        x_sample, mod_s, *proj_args, groups=sgroups, rows=steps, fox_transposed=False)

    chunk_p = HG_CHUNK if seq % HG_CHUNK == 0 else seq
    oh_p, s_p = _hgrn_prompt(q_p, k_p, g_p, v_p, batch, seq, chunk_p)
    oh_s, s_s = _hgrn_sample(q_s, k_s, g_s, v_s, state_hgrn[layer], steps)

    tq, tk = 256, 512
    bias_k, bias_qt = _decay(lf128_p.reshape(batch, seq, LANES), _bias_place_matrices())
    ckt = jnp.transpose(cache_k[layer], (0, 2, 3, 1)).reshape(n_phys, FOX_WIDTH, PAGE_SIZE)
    cvt = jnp.transpose(cache_v[layer], (0, 2, 3, 1)).reshape(n_phys, FOX_WIDTH, PAGE_SIZE)
    clt = jnp.transpose(cache_logf[layer], (0, 2, 1))
    per_seq = lambda a: a.reshape(nseq, steps, a.shape[-1])
    of_p, of_s = _fox(fq_p, fk_p, fv_p, bias_k, bias_qt, page_table, per_seq(fq_s), per_seq(fk_s), per_seq(fv_s),
                      per_seq(lf128_s), _page_suffix_matrix(), ckt, cvt, clt, tq, tk)
    of_s = of_s.reshape(nseq * steps, FOX_WIDTH)

    post_args = (onorm, vec3(norm_mix_post[layer]), vec3(norm_mlp_pre[layer]), vec3(norm_mlp_post[layer]),
                 wbh, wbf, wout, wup, wdn)
    y_p = _post(x_prompt, mod_p, oh_p, gate_p, of_p, mg_p, *post_args, groups=1, rows=tm)
    y_s = _post(x_sample, mod_s, oh_s, gate_s, of_s, mg_s, *post_args, groups=sgroups, rows=steps)

    k_prompt = fk_p.reshape(1, batch, FOX_HEADS, FOX_DH, seq).transpose(0, 1, 4, 2, 3)
    v_prompt = fv_p.reshape(1, batch, FOX_HEADS, FOX_DH, seq).transpose(0, 1, 4, 2, 3)
    logf_prompt = lf8_p.reshape(1, batch, seq, FOX_HEADS)
    k_sample = fk_s.reshape(1, nseq, steps, FOX_HEADS, FOX_DH)
    v_sample = fv_s.reshape(1, nseq, steps, FOX_HEADS, FOX_DH)
    logf_sample = lf8_s.reshape(1, nseq, steps, FOX_HEADS)
    return (y_p, y_s, k_prompt, v_prompt, logf_prompt, s_p[None], k_sample, v_sample, logf_sample, s_s[None])
```

```python
import functools

import jax
import jax.numpy as jnp
from jax import lax
from jax.experimental import pallas as pl
from jax.experimental.pallas import tpu as pltpu

F32 = jnp.float32
BF16 = jnp.bfloat16

LANES = 128
SUBLANES = 8
VMEM_LIMIT_BYTES = 56 * 1024 * 1024

D_MODEL = 1024
HG_HEADS = 4
HG_DK = 128
HG_DV = 128
HG_CHUNK = 32
FOX_HEADS = 8
FOX_DH = 64
FOX_WIDTH = FOX_HEADS * FOX_DH
HG_WIDTH = HG_HEADS * HG_DV
PAGE_SIZE = 128
N_MOD = 6
RMS_EPS = 1e-6
NEG_INF = float("-inf")
LOG2E = 1.4426950408889634
N_PIECES = 3
BIAS_SLOTS = 8

_NT = (((1,), (1,)), ((), ()))
_TN = (((0,), (0,)), ((), ()))


def _params(*sem):
    return pltpu.CompilerParams(dimension_semantics=sem, vmem_limit_bytes=VMEM_LIMIT_BYTES)


def _resident(shape):
    nd = len(shape)
    return pl.BlockSpec(shape, lambda *_: (0,) * nd, pipeline_mode=pl.Buffered(1))


def _sigmoid_pair(x):
    t = jnp.exp(-jnp.abs(x))
    r = 1.0 / (1.0 + t)
    tr = t * r
    pos = x >= 0
    return jnp.where(pos, r, tr), jnp.where(pos, tr, r)


def _split3(x):
    hi = x.astype(BF16)
    r1 = x - hi.astype(F32)
    mid = r1.astype(BF16)
    lo = (r1 - mid.astype(F32)).astype(BF16)
    return hi, mid, lo


def _dot_f32_lhs(x, w):
    hi, mid, lo = _split3(x)
    d = lambda p: jnp.dot(p, w, preferred_element_type=F32)
    return (d(lo) + d(mid)) + d(hi)


def _cumsum_rows(x, period):
    row = lax.broadcasted_iota(jnp.int32, x.shape, 0) & (period - 1)
    s = 1
    while s < period:
        x = x + jnp.where(row >= s, pltpu.roll(x, s, axis=0), 0.0)
        s *= 2
    return x


def _rms(x, w):
    return x * lax.rsqrt(jnp.mean(x * x, axis=-1, keepdims=True) + RMS_EPS) * w


def _ada_kernel(c_ref, w_ref, b_ref, o_ref):
    c = c_ref[...]
    s, _ = _sigmoid_pair(c)
    a = (c * s).astype(BF16)
    o_ref[...] = jnp.dot(a, w_ref[...].astype(BF16), preferred_element_type=F32) + b_ref[...]


def _ada(c, w, b, tn=1536):
    m, d = c.shape
    n = w.shape[1]
    return pl.pallas_call(
        _ada_kernel,
        grid=(n // tn,),
        in_specs=[pl.BlockSpec((m, d), lambda j: (0, 0)),
                  pl.BlockSpec((d, tn), lambda j: (0, j)),
                  pl.BlockSpec((1, tn), lambda j: (0, j))],
        out_specs=pl.BlockSpec((m, tn), lambda j: (0, j)),
        out_shape=jax.ShapeDtypeStruct((m, n), F32),
        compiler_params=_params("arbitrary"),
        name="ada",
    )(c, w, b)


def _inproj_kernel(x_ref, mod_ref, n1_ref, lbraw_ref, bf_ref, wa_ref, wff_ref, wmg_ref,
                   q_ref, g_ref, k_ref, v_ref, gate_ref, fq_ref, fk_ref, fv_ref,
                   lf8_ref, lf128_ref, mg_ref, *, fox_transposed):
    g_, r_, d = x_ref.shape
    m = g_ * r_
    x = x_ref[...]
    mod = mod_ref[...]
    sh1 = mod[:, :, 0:d]
    sc1 = mod[:, :, d:2 * d]
    h = _rms(x, n1_ref[...] * (1.0 + sc1)) + sh1
    hb = h.reshape(m, d).astype(BF16)

    raw = lbraw_ref[...]
    e = jnp.exp(raw - jnp.max(raw, axis=0, keepdims=True))
    lb = e[0:1, :] / jnp.sum(e, axis=0, keepdims=True)

    w = HG_WIDTH

    def proj(c):
        return lax.dot_general(hb, wa_ref[c * w:(c + 1) * w, :], _NT, preferred_element_type=F32)

    def proj_t(c):
        return lax.dot_general(wa_ref[c * w:(c + 1) * w, :], hb, _NT, preferred_element_type=F32)

    def merge_gate(c):
        mgc = lax.dot_general(hb, wmg_ref[c * w:(c + 1) * w, :], _NT, preferred_element_type=F32)
        mg_ref[:, c * w:(c + 1) * w] = _sigmoid_pair(mgc)[0].astype(BF16)

    def fox(c, ref, scale=None):
        y = proj_t(c) if fox_transposed else proj(c)
        y = y if scale is None else y * scale
        if fox_transposed:
            ref[0] = y.astype(ref.dtype)
        else:
            ref[...] = y.astype(ref.dtype)

    assert wmg_ref.shape[0] == 4 * w
    merge_gate(0)
    q_ref[...] = proj(0) * (HG_DK ** -0.5)
    merge_gate(1)
    v_ref[...] = proj(2).astype(BF16)
    merge_gate(2)
    fox(4, fq_ref, (LOG2E if fox_transposed else 1.0) * FOX_DH ** -0.5)
    merge_gate(3)
    fox(5, fk_ref)
    s_pos, s_neg = _sigmoid_pair(proj(1))
    g_ref[...] = jnp.log(lb + (1.0 - lb) * s_pos)
    k_ref[...] = (1.0 - lb) * s_neg
    fox(6, fv_ref)
    hg = proj(3)
    gate_ref[...] = (hg * _sigmoid_pair(hg)[0]).astype(BF16)
    z = lax.dot_general(hb, wff_ref[...], _NT, preferred_element_type=F32) + bf_ref[...]
    lf = jnp.minimum(z, 0.0) - jnp.log1p(jnp.exp(-jnp.abs(z)))
    lf128_ref[...] = lf
    lf8_ref[...] = lf[:, 0:FOX_HEADS]


def _inproj(x3, mod3, n1, lbraw, bf128, wa, wff, wmg, groups, rows, fox_transposed):
    assert groups == 1 or not fox_transposed
    nb, r_all, d = x3.shape
    nt = r_all // rows
    steps = (nb // groups) * nt
    t = nb * r_all
    m = groups * rows
    xmap = lambda s: (s // nt, s % nt, 0)
    mmap = lambda s: (s // nt, 0, 0)
    omap = lambda s: (s, 0)

    def out(width, dtype):
        return jax.ShapeDtypeStruct((t, width), dtype), pl.BlockSpec((m, width), omap)

    def fox(dtype):
        if not fox_transposed:
            return out(FOX_WIDTH, dtype)
        return (jax.ShapeDtypeStruct((nb, FOX_WIDTH, r_all), dtype),
                pl.BlockSpec((1, FOX_WIDTH, m), lambda s: (s // nt, 0, s % nt)))

    outs = [out(HG_WIDTH, F32), out(HG_WIDTH, F32), out(HG_WIDTH, F32), out(HG_WIDTH, BF16),
            out(HG_WIDTH, BF16), fox(BF16), fox(F32), fox(F32),
            out(FOX_HEADS, F32), out(LANES, F32), out(wmg.shape[0], BF16)]
    return pl.pallas_call(
        functools.partial(_inproj_kernel, fox_transposed=fox_transposed),
        grid=(steps,),
        in_specs=[pl.BlockSpec((groups, rows, d), xmap),
                  pl.BlockSpec((groups, 1, mod3.shape[2]), mmap),
                  _resident(n1.shape), _resident(lbraw.shape), _resident(bf128.shape),
                  _resident(wa.shape), _resident(wff.shape), _resident(wmg.shape)],
        out_specs=[o[1] for o in outs],
        out_shape=[o[0] for o in outs],
        compiler_params=_params("arbitrary"),
        name="inproj",
    )(x3, mod3, n1, lbraw, bf128, wa, wff, wmg)


def _decay_kernel(lf_ref, place_ref, bk_ref, bqt_ref):
    seq = lf_ref.shape[1]
    lane = lax.broadcasted_iota(jnp.int32, (LANES, LANES), 1)
    slot = lane & (BIAS_SLOTS - 1)
    used = lane < FOX_HEADS * BIAS_SLOTS
    ones_k = jnp.where(jnp.logical_and(used, slot < N_PIECES), 1.0, 0.0)
    ones_q = jnp.where(jnp.logical_and(used, jnp.logical_and(slot >= N_PIECES, slot < 2 * N_PIECES)), 1.0, 0.0)
    carry = jnp.zeros((1, LANES), F32)
    for j in range(seq // LANES):
        sl = slice(j * LANES, (j + 1) * LANES)
        cs = _cumsum_rows(lf_ref[0, sl, :], LANES) + carry
        carry = cs[LANES - 1:LANES, :]
        pieces = _split3(cs * LOG2E)

        def place(base):
            return sum(jnp.dot(pieces[j], place_ref[base + j], preferred_element_type=F32)
                       for j in range(N_PIECES))

        bk_ref[0, sl, :] = (ones_k - place(N_PIECES)).astype(BF16)
        bqt_ref[0, :, sl] = (ones_q + place(0)).T.astype(BF16)


def _decay(lf3, place):
    b, seq, _ = lf3.shape
    return pl.pallas_call(
        _decay_kernel,
        grid=(b,),
        in_specs=[pl.BlockSpec((1, seq, LANES), lambda i: (i, 0, 0)), _resident(place.shape)],
        out_specs=[pl.BlockSpec((1, seq, LANES), lambda i: (i, 0, 0)),
                   pl.BlockSpec((1, LANES, seq), lambda i: (i, 0, 0))],
        out_shape=[jax.ShapeDtypeStruct((b, seq, LANES), BF16),
                   jax.ShapeDtypeStruct((b, LANES, seq), BF16)],
        compiler_params=_params("arbitrary"),
        name="decay",
    )(lf3, place)


def _hgrn_decays(q, k, g, chunk):
    rows = q.shape[0]
    n_chunks = rows // chunk
    b = _cumsum_rows(g, chunk)
    qt = (q * jnp.exp(b)).astype(BF16)
    kt = (k * jnp.exp(-b)).astype(BF16)
    b3 = b.reshape(n_chunks, chunk, HG_DK)
    bl3 = b3[:, chunk - 1:chunk, :]
    kd = (k.reshape(n_chunks, chunk, HG_DK) * jnp.exp(bl3 - b3)).reshape(rows, HG_DK).astype(BF16)
    return qt, kt, kd, jnp.exp(bl3)


def _hgrn_products(qt, kt, kd, v, chunk):
    rows = qt.shape[0]
    n_chunks = rows // chunk
    shift = chunk.bit_length() - 1
    a = lax.dot_general(qt, kt, _NT, preferred_element_type=F32)
    v_t = v.astype(F32).T.astype(BF16)
    col_chunk = lax.broadcasted_iota(jnp.int32, v_t.shape, 1) >> shift
    v_blocks = jnp.concatenate([jnp.where(col_chunk == c, v_t, jnp.zeros_like(v_t)) for c in range(n_chunks)],
                               axis=0)
    incr = jnp.dot(v_blocks, kd, preferred_element_type=F32)
    ri = lax.broadcasted_iota(jnp.int32, (rows, rows), 0)
    ci = lax.broadcasted_iota(jnp.int32, (rows, rows), 1)
    keep = (ri - ci).astype(jnp.uint32) <= (ri & (chunk - 1)).astype(jnp.uint32)
    o_intra = jnp.dot(jnp.where(keep, a, 0.0).astype(BF16), v, preferred_element_type=F32)
    return o_intra, incr


def _hgrn_blocks(blocks, chunk, state_in, state_out):
    n_chunks = LANES // chunk
    decays = [_hgrn_decays(q, k, g, chunk) for (q, k, g, _) in blocks]
    products = [_hgrn_products(qt, kt, kd, blk[3], chunk) for (qt, kt, kd, _), blk in zip(decays, blocks)]
    entering = []
    st = None
    for n in range(len(blocks) * n_chunks):
        j, c = divmod(n, n_chunks)
        st = state_in(n, st)
        entering.append(st.astype(BF16))
        st = st * decays[j][3][c] + products[j][1][c * HG_DV:(c + 1) * HG_DV, :]
        state_out(n, st)
    outs = []
    for j, ((qt, _, _, _), (o_intra, _)) in enumerate(zip(decays, products)):
        o_inter = [lax.dot_general(qt[c * chunk:(c + 1) * chunk, :], entering[j * n_chunks + c], _NT,
                                   preferred_element_type=F32) for c in range(n_chunks)]
        outs.append(o_intra + jnp.concatenate(o_inter, axis=0))
    return outs


HGRN_GROUP = 8


def _hgrn_prompt_kernel(pt_ref, q_ref, k_ref, g_ref, v_ref,
                        sq_ref, sk_ref, sv_ref, slf_ref, sfx_ref, ck_hbm, cv_hbm, cl_hbm,
                        o_ref, s_ref, so_ref, k_buf, v_buf, lf_buf, sems, *, chunk, n_steps):
    step = pl.program_id(0) * pl.num_programs(1) + pl.program_id(1)
    seq = q_ref.shape[0]
    group_rows = HGRN_GROUP * LANES
    per_step = sq_ref.shape[0]
    n_groups = seq // group_rows
    bounds = [(j * n_groups) // per_step for j in range(per_step + 1)]
    state = [jnp.zeros((HG_DV, HG_DK), F32)]

    def group(i):
        row_slices = [pl.ds((i * HGRN_GROUP + j) * LANES, LANES) for j in range(HGRN_GROUP)]
        blocks = [(q_ref[rs, :], k_ref[rs, :], g_ref[rs, :], v_ref[rs, :]) for rs in row_slices]
        st0 = state[0]
        last = []
        outs = _hgrn_blocks(blocks, chunk, lambda n, prev: st0 if prev is None else prev,
                            lambda n, st: last.append(st))
        for rs, o in zip(row_slices, outs):
            o_ref[rs, :] = o
        state[0] = last[-1]

    def recurrence_share(j):
        for i in range(bounds[j], bounds[j + 1]):
            group(i)

    _attend_sample_sequences(step, n_steps, 0, pt_ref, sq_ref, sk_ref, sv_ref, slf_ref, sfx_ref,
                             ck_hbm, cv_hbm, cl_hbm, so_ref, k_buf, v_buf, lf_buf, sems, recurrence_share)
    s_ref[0, 0] = state[0].T


def _hgrn_prompt(q, k, g, v, batch, seq, chunk, page_table, sample, sfx_mat, caches):
    n_steps = batch * HG_HEADS
    n_local, steps, _ = sample[0].shape
    per_step = n_local // n_steps
    assert per_step * n_steps == n_local
    spec = pl.BlockSpec((seq, HG_DK), lambda b, h, *_: (b, h))
    s_in, s_out, s_scratch = _sample_stream_specs(sample, sfx_mat, caches, per_step, page_table.shape[1],
                                                  lambda b, h, *_: b * HG_HEADS + h)
    grid_spec = pltpu.PrefetchScalarGridSpec(
        num_scalar_prefetch=1,
        grid=(batch, HG_HEADS),
        in_specs=[spec, spec, spec, spec] + s_in,
        out_specs=[spec, pl.BlockSpec((1, 1, HG_DK, HG_DV), lambda b, h, *_: (b, h, 0, 0)), s_out],
        scratch_shapes=s_scratch,
    )
    return pl.pallas_call(
        functools.partial(_hgrn_prompt_kernel, chunk=chunk, n_steps=n_steps),
        grid_spec=grid_spec,
        out_shape=[jax.ShapeDtypeStruct((batch * seq, HG_WIDTH), F32),
                   jax.ShapeDtypeStruct((batch, HG_HEADS, HG_DK, HG_DV), F32),
                   jax.ShapeDtypeStruct((n_local, steps, FOX_WIDTH), BF16)],
        compiler_params=_params("arbitrary", "arbitrary"),
        name="hgrn_prompt",
    )(page_table.reshape(-1), q, k, g, v, *sample, sfx_mat, *caches)


def _hgrn_sample_kernel(q_ref, k_ref, g_ref, v_ref, s0_ref, o_ref, s_ref, *, chunk):
    per = LANES // chunk

    def state_in(n, prev):
        h, c = divmod(n, per)
        return s0_ref[c, h].T

    def state_out(n, st):
        h, c = divmod(n, per)
        s_ref[c, h] = st.T

    heads = [slice(h * HG_DK, (h + 1) * HG_DK) for h in range(HG_HEADS)]
    outs = _hgrn_blocks([(q_ref[:, hs], k_ref[:, hs], g_ref[:, hs], v_ref[:, hs]) for hs in heads],
                        chunk, state_in, state_out)
    for hs, o in zip(heads, outs):
        o_ref[:, hs] = o


def _hgrn_sample(q, k, g, v, s0, chunk):
    t = q.shape[0]
    per = LANES // chunk
    spec = pl.BlockSpec((LANES, HG_WIDTH), lambda i: (i, 0))
    sspec = pl.BlockSpec((per, HG_HEADS, HG_DK, HG_DV), lambda i: (i, 0, 0, 0))
    return pl.pallas_call(
        functools.partial(_hgrn_sample_kernel, chunk=chunk),
        grid=(t // LANES,),
        in_specs=[spec, spec, spec, spec, sspec],
        out_specs=[spec, sspec],
        out_shape=[jax.ShapeDtypeStruct((t, HG_WIDTH), F32),
                   jax.ShapeDtypeStruct(s0.shape, F32)],
        compiler_params=_params("arbitrary"),
        name="hgrn_sample",
    )(q, k, g, v, s0)


ONES_ROWS = 16
PAGE_PREFETCH_DEPTH = 2
SAMPLE_SEQS_PER_HGRN_STEP = 1


def _prompt_attention(uq_ref, uk_ref, um_ref, qt_ref, kt_ref, vt_ref, bk_ref, bqt_ref, o_ref,
                      kn_ref, va_ref, rhs_ref, mask_ref, s_ref, p_ref, *, pair, tq, tk, n_units, limits):
    seq = kt_ref.shape[1]
    n_heads = LANES // FOX_DH

    def prepare():
        for c in range(seq // LANES):
            cs = slice(c * LANES, (c + 1) * LANES)
            kn_ref[cs, 0:LANES] = kt_ref[:, cs].T.astype(BF16)
        kn_ref[:, LANES:2 * LANES] = bk_ref[...]
        for c in range(seq // tk):
            for e in range(n_heads):
                va_ref[c, e, 0:FOX_DH, :] = vt_ref[e * FOX_DH:(e + 1) * FOX_DH, c * tk:(c + 1) * tk].astype(BF16)
                va_ref[c, e, FOX_DH:FOX_DH + ONES_ROWS, :] = jnp.ones((ONES_ROWS, tk), BF16)
        row = lax.broadcasted_iota(jnp.int32, (LANES, tq), 0)
        for i in range(seq // tq):
            qs = slice(i * tq, (i + 1) * tq)
            qt = qt_ref[:, qs]
            bqt = bqt_ref[:, qs]
            for e in range(n_heads):
                rhs_ref[i, e, 0:LANES, :] = jnp.where(row // FOX_DH == e, qt, jnp.zeros_like(qt))
                rhs_ref[i, e, LANES:2 * LANES, :] = jnp.where(row // BIAS_SLOTS == pair * n_heads + e, bqt,
                                                              jnp.zeros_like(bqt))
        p_ref[1] = jnp.zeros(p_ref.shape[1:], BF16)
        key_minus_query = (lax.broadcasted_iota(jnp.int32, (tk, tq), 0)
                           - lax.broadcasted_iota(jnp.int32, (tk, tq), 1))
        for n, limit in enumerate(limits):
            mask_ref[n] = jnp.where(key_minus_query <= limit, 0.0, NEG_INF)
        rows = lambda n, v: [jnp.full((n, tq), v, F32) for _ in range(n_heads)]
        scores(1, 0)
        return rows(1, NEG_INF), rows(1, 1.0), rows(1, 1.0), rows(FOX_DH, 0.0)

    def scores(u, slot):
        lhs = kn_ref[pl.ds(pl.multiple_of(uk_ref[u] * tk, tk), tk), :]
        for e in range(n_heads):
            s_ref[slot, e] = jnp.dot(lhs, rhs_ref[uq_ref[u], e], preferred_element_type=F32)

    def numerators(u, slot, ms):
        first = uk_ref[u] == 0
        mask = mask_ref[um_ref[u]]
        new_ms, alphas = [], []
        for e in range(n_heads):
            m = jnp.where(first, NEG_INF, ms[e])
            s = s_ref[slot, e] + mask
            m_new = jnp.maximum(m, jnp.max(s, axis=0, keepdims=True))
            p_ref[slot, e] = jnp.exp2(s - m_new).astype(BF16)
            new_ms.append(m_new)
            alphas.append(jnp.exp2(m - m_new))
        return new_ms, alphas

    def values(u, slot, alphas, ls, accs):
        new_ls, new_accs = [], []
        for e in range(n_heads):
            pv = jnp.dot(va_ref[uk_ref[u], e], p_ref[slot, e], preferred_element_type=F32)
            new_ls.append(alphas[e] * ls[e] + pv[FOX_DH:FOX_DH + 1, :])
            new_accs.append(alphas[e] * accs[e] + pv[0:FOX_DH, :])
        o_t = jnp.concatenate([new_accs[e] / new_ls[e] for e in range(n_heads)], axis=0)
        o_ref[pl.ds(pl.multiple_of(uq_ref[u] * tq, tq), tq), :] = o_t.T.astype(BF16)
        return new_ls, new_accs

    def step(u, slot, carry):
        ms, alphas, ls, accs = carry
        new_ls, new_accs = values(u - 1, 1 - slot, alphas, ls, accs)
        scores(u + 1, 1 - slot)
        new_ms, new_alphas = numerators(u, slot, ms)
        return new_ms, new_alphas, new_ls, new_accs

    def body(k, carry):
        carry = step(2 * k + 1, 0, carry)
        return step(2 * k + 2, 1, carry)

    def run(lo, hi, state):
        for k in range(lo, hi):
            state = body(k, state)
        return state

    def finish(state):
        _, alphas, ls, accs = state
        values(n_units, 1, alphas, ls, accs)

    return prepare, run, finish


def _sample_attend(q, k_new, v_new, lf_new, sfx_ref, k_refs, v_refs, lf_refs):
    n_pages = len(k_refs)
    steps = q.shape[0]
    rows = FOX_HEADS * steps
    pad = PAGE_SIZE - steps

    def per_head_rows(x8):
        return jnp.broadcast_to(x8[:, None, :], (FOX_HEADS, steps, x8.shape[1])).reshape(rows, x8.shape[1])

    q = q.astype(F32)
    q_rows = jnp.concatenate([q] * FOX_HEADS, axis=0)
    row_head = lax.broadcasted_iota(jnp.int32, (rows, FOX_WIDTH), 0) // steps
    lane_head = lax.broadcasted_iota(jnp.int32, (rows, FOX_WIDTH), 1) // FOX_DH
    head_mask = row_head == lane_head
    qbd = jnp.where(head_mask, q_rows, 0.0).astype(BF16)

    pn = _cumsum_rows(lf_new, steps)
    pn_rows = jnp.concatenate([pn] * FOX_HEADS, axis=0)
    r_h = lax.broadcasted_iota(jnp.int32, (rows, LANES), 0) // steps
    r_i = lax.broadcasted_iota(jnp.int32, (rows, LANES), 0) % steps
    c_l = lax.broadcasted_iota(jnp.int32, (rows, LANES), 1)
    pcol = jnp.sum(jnp.where(c_l == r_h, pn_rows, 0.0), axis=1, keepdims=True)
    pn_t = jnp.concatenate([pn, jnp.zeros((pad, LANES), F32)], axis=0).T[0:FOX_HEADS, :]

    lf_all = jnp.concatenate([lf_refs[j][...] for j in range(n_pages)], axis=0)
    sfx = _dot_f32_lhs(lf_all, sfx_ref[...])
    carry = jnp.zeros((FOX_HEADS, PAGE_SIZE), F32)
    page_bias = [None] * n_pages
    for j in reversed(range(n_pages)):
        blk = sfx[j * FOX_HEADS:(j + 1) * FOX_HEADS, :]
        page_bias[j] = blk[:, 0:PAGE_SIZE] + carry
        carry = carry + blk[:, PAGE_SIZE:2 * PAGE_SIZE]

    s_tiles = []
    for j in range(n_pages):
        s = jnp.dot(qbd, k_refs[j][...].astype(BF16), preferred_element_type=F32)
        s_tiles.append(s + (per_head_rows(page_bias[j]) + pcol))
    kn = jnp.concatenate([k_new, jnp.zeros((pad, FOX_WIDTH), F32)], axis=0).astype(BF16)
    s_new = lax.dot_general(qbd, kn, _NT, preferred_element_type=F32) + (pcol - per_head_rows(pn_t))
    s_tiles.append(jnp.where(c_l <= r_i, s_new, NEG_INF))

    m_el = s_tiles[0]
    for s in s_tiles[1:]:
        m_el = jnp.maximum(m_el, s)
    m = jnp.max(m_el, axis=1, keepdims=True)
    p_tiles = [jnp.exp(s - m) for s in s_tiles]
    l_el = p_tiles[0]
    for p in p_tiles[1:]:
        l_el = l_el + p
    l = jnp.sum(l_el, axis=1, keepdims=True)

    vn = jnp.concatenate([v_new, jnp.zeros((pad, FOX_WIDTH), F32)], axis=0).astype(BF16)
    o = jnp.dot(p_tiles[n_pages].astype(BF16), vn, preferred_element_type=F32)
    for j in range(n_pages):
        o = o + lax.dot_general(p_tiles[j].astype(BF16), v_refs[j][...].astype(BF16), _NT,
                                preferred_element_type=F32)
    o = jnp.where(head_mask, o / l, 0.0)
    return jnp.sum(o.reshape(FOX_HEADS, steps, FOX_WIDTH), axis=0)


def _attend_sample_sequences(step, n_steps, first_seq, pt_ref, q_ref, k_ref, v_ref, lf_ref, sfx_ref,
                             ck_hbm, cv_hbm, cl_hbm, o_ref, k_buf, v_buf, lf_buf, sems, after_sequence):
    per_step = q_ref.shape[0]
    n_slots, n_pages = k_buf.shape[0], k_buf.shape[1]
    depth = PAGE_PREFETCH_DEPTH
    assert 0 < depth < n_slots
    total = n_steps * per_step
    last_seq = first_seq + total - 1
    static_slots = per_step % n_slots == 0

    def slot_of(j):
        return j % n_slots if static_slots else lax.rem(step * per_step + j, n_slots)

    def page_copies(seq, slot):
        copies = []
        for n in range(n_pages):
            page = pt_ref[seq * n_pages + n]
            copies.append(pltpu.make_async_copy(ck_hbm.at[page], k_buf.at[slot, n], sems.at[0, slot]))
            copies.append(pltpu.make_async_copy(cv_hbm.at[page], v_buf.at[slot, n], sems.at[0, slot]))
            copies.append(pltpu.make_async_copy(cl_hbm.at[page], lf_buf.at[slot, n], sems.at[1, slot]))
        return copies

    @pl.when(step == 0)
    def _():
        for d in range(depth):
            for c in page_copies(first_seq + d, d):
                c.start()

    for j in range(per_step):
        seq = first_seq + step * per_step + j
        slot = slot_of(j)
        for c in page_copies(jnp.minimum(seq + depth, last_seq), slot_of(j + depth)):
            c.start()
        for c in page_copies(seq, slot):
            c.wait()
        o_ref[j] = _sample_attend(
            q_ref[j], k_ref[j], v_ref[j], lf_ref[j], sfx_ref,
            [k_buf.at[slot, n] for n in range(n_pages)], [v_buf.at[slot, n] for n in range(n_pages)],
            [lf_buf.at[slot, n] for n in range(n_pages)]).astype(o_ref.dtype)
        after_sequence(j)

    @pl.when(step == n_steps - 1)
    def _():
        for e in range(depth):
            for c in page_copies(last_seq, (total - 1 - e + depth) % n_slots):
                c.wait()


def _fox_kernel(uq_ref, uk_ref, um_ref, pt_ref,
                qt_ref, kt_ref, vt_ref, bk_ref, bqt_ref,
                sq_ref, sk_ref, sv_ref, slf_ref, sfx_ref, ck_hbm, cv_hbm, cl_hbm,
                o_ref, so_ref,
                kn_ref, va_ref, rhs_ref, mask_ref, s_ref, p_ref, k_buf, v_buf, lf_buf, sems,
                *, tq, tk, n_units, limits, n_steps, first_seq):
    step = pl.program_id(0) * pl.num_programs(1) + pl.program_id(1)
    per_step = sq_ref.shape[0]
    prepare, run, finish = _prompt_attention(
        uq_ref, uk_ref, um_ref, qt_ref, kt_ref, vt_ref, bk_ref, bqt_ref, o_ref,
        kn_ref, va_ref, rhs_ref, mask_ref, s_ref, p_ref,
        pair=pl.program_id(1), tq=tq, tk=tk, n_units=n_units, limits=limits)
    state = [prepare()]
    n_pairs = n_units // 2
    bounds = [(j * n_pairs) // per_step for j in range(per_step + 1)]

    def prompt_segment(j):
        state[0] = run(bounds[j], bounds[j + 1], state[0])

    _attend_sample_sequences(step, n_steps, first_seq, pt_ref, sq_ref, sk_ref, sv_ref, slf_ref, sfx_ref,
                             ck_hbm, cv_hbm, cl_hbm, so_ref, k_buf, v_buf, lf_buf, sems, prompt_segment)
    finish(state[0])


def _sample_stream_specs(sample, sfx_mat, caches, per_step, n_pages, step_index):
    steps = sample[0].shape[1]
    spec = lambda a: pl.BlockSpec((per_step, steps, a.shape[2]), lambda *g: (step_index(*g), 0, 0))
    in_hbm = pl.BlockSpec(memory_space=pl.ANY)
    in_specs = [spec(a) for a in sample] + [pl.BlockSpec(sfx_mat.shape, lambda *g: (0, 0))] + [in_hbm] * 3
    n_slots = PAGE_PREFETCH_DEPTH + 1
    page = lambda rows, a: pltpu.VMEM((n_slots, n_pages, rows, PAGE_SIZE), a.dtype)
    scratch = [page(FOX_WIDTH, caches[0]), page(FOX_WIDTH, caches[1]), page(FOX_HEADS, caches[2]),
               pltpu.SemaphoreType.DMA((2, n_slots))]
    return in_specs, spec(sample[0]), scratch


def _fox(fqt, fkt, fvt, bk, bqt, page_table, first_seq, sample, sfx_mat, caches, tq, tk):
    batch, _, seq = fqt.shape
    n_pages = page_table.shape[1]
    n_local, steps, _ = sample[0].shape
    assert tk % tq == 0 and seq % tk == 0
    nq = seq // tq
    pairs = FOX_WIDTH // LANES
    n_heads = LANES // FOX_DH
    n_steps = batch * pairs
    per_step = n_local // n_steps
    assert per_step * n_steps == n_local
    units = [(i, j) for i in range(nq) for j in range((i * tq) // tk + 1)]
    n_units = len(units)
    assert n_units % 2 == 0
    padded = [units[0]] + units + [units[-1]]
    unit_limits = [min(i * tq - j * tk, tk - 1) for i, j in padded]
    limits = tuple(sorted(set(unit_limits)))
    table = lambda vals: jnp.asarray(vals, jnp.int32)
    uq, uk = table([u[0] for u in padded]), table([u[1] for u in padded])
    um = table([limits.index(v) for v in unit_limits])
    head_pair = lambda b, p, *_: (b, p, 0)
    per_batch = lambda b, p, *_: (b, 0, 0)
    s_in, s_out, s_scratch = _sample_stream_specs(sample, sfx_mat, caches, per_step, n_pages,
                                                  lambda b, p, *_: b * pairs + p)
    grid_spec = pltpu.PrefetchScalarGridSpec(
        num_scalar_prefetch=4,
        grid=(batch, pairs),
        in_specs=[pl.BlockSpec((None, LANES, seq), head_pair),
                  pl.BlockSpec((None, LANES, seq), head_pair),
                  pl.BlockSpec((None, LANES, seq), head_pair),
                  pl.BlockSpec((None, seq, LANES), per_batch),
                  pl.BlockSpec((None, LANES, seq), per_batch)] + s_in,
        out_specs=[pl.BlockSpec((seq, LANES), lambda b, p, *_: (b, p)), s_out],
        scratch_shapes=[pltpu.VMEM((seq, 2 * LANES), BF16),
                        pltpu.VMEM((seq // tk, n_heads, FOX_DH + ONES_ROWS, tk), BF16),
                        pltpu.VMEM((nq, n_heads, 2 * LANES, tq), BF16),
                        pltpu.VMEM((len(limits), tk, tq), F32),
                        pltpu.VMEM((2, n_heads, tk, tq), F32),
                        pltpu.VMEM((2, n_heads, tk, tq), BF16)] + s_scratch,
    )
    return pl.pallas_call(
        functools.partial(_fox_kernel, tq=tq, tk=tk, n_units=n_units, limits=limits, n_steps=n_steps,
                          first_seq=first_seq),
        grid_spec=grid_spec,
        out_shape=[jax.ShapeDtypeStruct((batch * seq, FOX_WIDTH), BF16),
                   jax.ShapeDtypeStruct((n_local, steps, FOX_WIDTH), BF16)],
        compiler_params=_params("arbitrary", "arbitrary"),
        name="fox",
    )(uq, uk, um, page_table.reshape(-1), fqt, fkt, fvt, bk, bqt, *sample, sfx_mat, *caches)


def _post_kernel(x_ref, mod_ref, oh_ref, gate_ref, of_ref, mg_ref, onorm_ref, n1post_ref, n2pre_ref,
                 n2post_ref, wbh_ref, wbf_ref, wout_ref, wup_ref, wdn_ref, y_ref, *, ff_chunk):
    g_, r_, d = x_ref.shape
    m = g_ * r_
    mod = mod_ref[...]
    gt1 = mod[:, :, 2 * d:3 * d]
    sh2 = mod[:, :, 3 * d:4 * d]
    sc2 = mod[:, :, 4 * d:5 * d]
    gt2 = mod[:, :, 5 * d:6 * d]

    oh = oh_ref[...]
    parts = [_rms(oh[:, h * HG_DV:(h + 1) * HG_DV], onorm_ref[...]) for h in range(HG_HEADS)]
    ohn = (jnp.concatenate(parts, axis=1) * gate_ref[...].astype(F32)).astype(BF16)
    br_h = jnp.dot(ohn, wbh_ref[...], preferred_element_type=F32)
    br_f = jnp.dot(of_ref[...], wbf_ref[...], preferred_element_type=F32)
    mg = mg_ref[...].astype(F32)
    z = (mg[:, 0:d] * br_h + mg[:, d:2 * d] * br_f).astype(BF16)
    y = jnp.dot(z, wout_ref[...], preferred_element_type=F32).reshape(g_, r_, d)
    x1 = x_ref[...] + _rms(y, gt1 * n1post_ref[...])
    h2 = (_rms(x1, n2pre_ref[...] * (1.0 + sc2)) + sh2).reshape(m, d).astype(BF16)
    u = jnp.zeros((m, d), F32)
    for c in range(wup_ref.shape[1] // ff_chunk):
        cs = slice(c * ff_chunk, (c + 1) * ff_chunk)
        a = jnp.maximum(jnp.dot(h2, wup_ref[:, cs], preferred_element_type=F32), 0.0)
        u = u + jnp.dot((a * a).astype(BF16), wdn_ref[cs, :], preferred_element_type=F32)
    y_ref[...] = x1 + _rms(u.reshape(g_, r_, d), gt2 * n2post_ref[...])


def _post(x3, mod3, oh, gate, of, mg, onorm, n1post, n2pre, n2post, wbh, wbf, wout, wup, wdn,
          groups, rows, ff_chunk=1024):
    nb, r_all, d = x3.shape
    nt = r_all // rows
    steps = (nb // groups) * nt
    m = groups * rows
    xmap = lambda s: (s // nt, s % nt, 0)
    mmap = lambda s: (s // nt, 0, 0)
    tmap = lambda s: (s, 0)
    tok = lambda a: pl.BlockSpec((m, a.shape[1]), tmap)
    consts = [onorm, n1post, n2pre, n2post, wbh, wbf, wout, wup, wdn]
    return pl.pallas_call(
        functools.partial(_post_kernel, ff_chunk=ff_chunk),
        grid=(steps,),
        in_specs=[pl.BlockSpec((groups, rows, d), xmap),
                  pl.BlockSpec((groups, 1, mod3.shape[2]), mmap),
                  tok(oh), tok(gate), tok(of), tok(mg)] + [_resident(c.shape) for c in consts],
        out_specs=pl.BlockSpec((groups, rows, d), xmap),
        out_shape=jax.ShapeDtypeStruct(x3.shape, F32),
        compiler_params=_params("arbitrary"),
        name="post",
    )(x3, mod3, oh, gate, of, mg, *consts)


def _bias_place_matrices():
    shape = (2 * N_PIECES, LANES, LANES)
    s = lax.broadcasted_iota(jnp.int32, shape, 0)
    r = lax.broadcasted_iota(jnp.int32, shape, 1)
    c = lax.broadcasted_iota(jnp.int32, shape, 2)
    return jnp.logical_and(r < FOX_HEADS, c == BIAS_SLOTS * r + s).astype(BF16)


def _page_suffix_matrix():
    r = lax.broadcasted_iota(jnp.int32, (PAGE_SIZE, 2 * PAGE_SIZE), 0)
    c = lax.broadcasted_iota(jnp.int32, (PAGE_SIZE, 2 * PAGE_SIZE), 1)
    return jnp.logical_or(r > c, c >= PAGE_SIZE).astype(BF16)


def kernel(x_prompt, x_sample, c_prompt, c_sample, cache_k, cache_v, cache_logf, state_hgrn, page_table,
           ada_w, ada_b, norm_mix_pre, norm_mix_post, norm_mlp_pre, norm_mlp_post, w_in,
           hgrn_lower_bounds, hgrn_onorm, fox_b_f, w_br_h, w_br_f, w_out, w_mlp_up, w_mlp_down):
    batch, seq, d = x_prompt.shape
    nseq, steps, _ = x_sample.shape
    layer = 0
    n_phys = cache_k.shape[1]

    w_t = jnp.transpose(w_in[layer]).astype(BF16)
    n_a = 4 * HG_WIDTH + 3 * FOX_WIDTH
    wa = w_t[:n_a]
    wff = jnp.pad(w_t[n_a:n_a + FOX_HEADS], ((0, LANES - FOX_HEADS), (0, 0)))
    wmg = w_t[n_a + FOX_HEADS:]
    bf128 = jnp.pad(fox_b_f[layer], (0, LANES - FOX_HEADS)).reshape(1, LANES)
    lbraw = hgrn_lower_bounds
    vec3 = lambda v: v.reshape(1, 1, -1)
    wbh, wbf = w_br_h[layer].astype(BF16), w_br_f[layer].astype(BF16)
    wout = w_out[layer].astype(BF16)
    wup, wdn = w_mlp_up[layer].astype(BF16), w_mlp_down[layer].astype(BF16)
    onorm = hgrn_onorm[layer].reshape(1, HG_DV)

    n_c = batch + nseq
    c_pad = -n_c % (2 * SUBLANES)
    c_all = jnp.concatenate([c_prompt, c_sample, jnp.zeros((c_pad, d), F32)], axis=0)
    mod = _ada(c_all, ada_w[layer], ada_b[layer].reshape(1, -1))
    mod_p = mod[:batch].reshape(batch, 1, -1)
    mod_s = mod[batch:n_c].reshape(nseq, 1, -1)

    tm = 512
    sgroups = tm // steps
    proj_args = (vec3(norm_mix_pre[layer]), lbraw, bf128, wa, wff, wmg)
    (q_p, g_p, k_p, v_p, gate_p, fq_p, fk_p, fv_p, lf8_p, lf128_p, mg_p) = _inproj(
        x_prompt, mod_p, *proj_args, groups=1, rows=tm, fox_transposed=True)
    (q_s, g_s, k_s, v_s, gate_s, fq_s, fk_s, fv_s, lf8_s, lf128_s, mg_s) = _inproj(
        x_sample, mod_s, *proj_args, groups=sgroups, rows=steps, fox_transposed=False)

    caches = (jnp.transpose(cache_k[layer], (0, 2, 3, 1)).reshape(n_phys, FOX_WIDTH, PAGE_SIZE),
              jnp.transpose(cache_v[layer], (0, 2, 3, 1)).reshape(n_phys, FOX_WIDTH, PAGE_SIZE),
              jnp.transpose(cache_logf[layer], (0, 2, 1)))
    per_seq = lambda a: a.reshape(nseq, steps, a.shape[-1])
    sample = (per_seq(fq_s), per_seq(fk_s), per_seq(fv_s), per_seq(lf128_s))
    n_with_hgrn = SAMPLE_SEQS_PER_HGRN_STEP * batch * HG_HEADS
    sfx = _page_suffix_matrix()

    chunk_p = HG_CHUNK if seq % HG_CHUNK == 0 else seq
    oh_p, s_p, of_s0 = _hgrn_prompt(q_p, k_p, g_p, v_p, batch, seq, chunk_p, page_table,
                                    tuple(a[:n_with_hgrn] for a in sample), sfx, caches)
    oh_s, s_s = _hgrn_sample(q_s, k_s, g_s, v_s, state_hgrn[layer], steps)

    tq, tk = 256, 512
    bias_k, bias_qt = _decay(lf128_p.reshape(batch, seq, LANES), _bias_place_matrices())
    of_p, of_s1 = _fox(fq_p, fk_p, fv_p, bias_k, bias_qt, page_table, n_with_hgrn,
                       tuple(a[n_with_hgrn:] for a in sample), sfx, caches, tq, tk)
    of_s = jnp.concatenate([of_s0, of_s1], axis=0).reshape(nseq * steps, FOX_WIDTH)

    post_args = (onorm, vec3(norm_mix_post[layer]), vec3(norm_mlp_pre[layer]), vec3(norm_mlp_post[layer]),
                 wbh, wbf, wout, wup, wdn)
    y_p = _post(x_prompt, mod_p, oh_p, gate_p, of_p, mg_p, *post_args, groups=1, rows=tm)
    y_s = _post(x_sample, mod_s, oh_s, gate_s, of_s, mg_s, *post_args, groups=sgroups, rows=steps)

    k_prompt = fk_p.reshape(1, batch, FOX_HEADS, FOX_DH, seq).transpose(0, 1, 4, 2, 3)
    v_prompt = fv_p.reshape(1, batch, FOX_HEADS, FOX_DH, seq).transpose(0, 1, 4, 2, 3)
    logf_prompt = lf8_p.reshape(1, batch, seq, FOX_HEADS)
    k_sample = fk_s.reshape(1, nseq, steps, FOX_HEADS, FOX_DH)
    v_sample = fv_s.reshape(1, nseq, steps, FOX_HEADS, FOX_DH)
    logf_sample = lf8_s.reshape(1, nseq, steps, FOX_HEADS)
    return (y_p, y_s, k_prompt, v_prompt, logf_prompt, s_p[None], k_sample, v_sample, logf_sample, s_s[None])
```

```python
import functools

import jax
import jax.numpy as jnp
from jax import lax
from jax.experimental import pallas as pl
from jax.experimental.pallas import tpu as pltpu

F32 = jnp.float32
BF16 = jnp.bfloat16

LANES = 128
SUBLANES = 8
VMEM_LIMIT_BYTES = 56 * 1024 * 1024
TOKEN_TILE = 512
FOX_Q_TILE = 256
FOX_K_TILE = 512

HG_HEADS = 4
HG_DK = 128
HG_DV = 128
HG_CHUNK = 32
FOX_HEADS = 8
FOX_DH = 64
FOX_WIDTH = FOX_HEADS * FOX_DH
HG_WIDTH = HG_HEADS * HG_DV
PAGE_SIZE = 128
RMS_EPS = 1e-6
NEG_INF = float("-inf")
LOG2E = 1.4426950408889634
N_PIECES = 3
BIAS_SLOTS = 8

_NT = (((1,), (1,)), ((), ()))


def _params(*sem):
    return pltpu.CompilerParams(dimension_semantics=sem, vmem_limit_bytes=VMEM_LIMIT_BYTES)


def _resident(shape):
    nd = len(shape)
    return pl.BlockSpec(shape, lambda *_: (0,) * nd, pipeline_mode=pl.Buffered(1))


def _sigmoid_pair(x):
    t = jnp.exp(-jnp.abs(x))
    r = 1.0 / (1.0 + t)
    tr = t * r
    pos = x >= 0
    return jnp.where(pos, r, tr), jnp.where(pos, tr, r)


def _sigmoid(x):
    return 0.5 * jnp.tanh(0.5 * x) + 0.5


def _split3(x):
    hi = x.astype(BF16)
    r1 = x - hi.astype(F32)
    mid = r1.astype(BF16)
    lo = (r1 - mid.astype(F32)).astype(BF16)
    return hi, mid, lo


def _dot_f32_lhs(x, w):
    hi, mid, lo = _split3(x)
    d = lambda p: jnp.dot(p, w, preferred_element_type=F32)
    return (d(lo) + d(mid)) + d(hi)


def _cumsum_rows(x, period):
    row = lax.broadcasted_iota(jnp.int32, x.shape, 0) & (period - 1)
    s = 1
    while s < period:
        x = x + jnp.where(row >= s, pltpu.roll(x, s, axis=0), 0.0)
        s *= 2
    return x


def _rms(x, w):
    return x * lax.rsqrt(jnp.mean(x * x, axis=-1, keepdims=True) + RMS_EPS) * w


def _ada_kernel(c_ref, w_ref, b_ref, o_ref):
    c = c_ref[...]
    s, _ = _sigmoid_pair(c)
    a = (c * s).astype(BF16)
    o_ref[...] = jnp.dot(a, w_ref[...].astype(BF16), preferred_element_type=F32) + b_ref[...]


def _ada(c, w, b, tn=1536):
    m, d = c.shape
    n = w.shape[1]
    return pl.pallas_call(
        _ada_kernel,
        grid=(n // tn,),
        in_specs=[pl.BlockSpec((m, d), lambda j: (0, 0)),
                  pl.BlockSpec((d, tn), lambda j: (0, j)),
                  pl.BlockSpec((1, tn), lambda j: (0, j))],
        out_specs=pl.BlockSpec((m, tn), lambda j: (0, j)),
        out_shape=jax.ShapeDtypeStruct((m, n), F32),
        compiler_params=_params("arbitrary"),
        name="ada",
    )(c, w, b)


def _inproj_kernel(x_ref, mod_ref, n1_ref, lbraw_ref, bf_ref, wa_ref, wff_ref, wmg_ref,
                   q_ref, g_ref, k_ref, v_ref, gate_ref, fq_ref, fk_ref, fv_ref,
                   lf8_ref, lf128_ref, mg_ref, *, fox_transposed):
    g_, r_, d = x_ref.shape
    m = g_ * r_
    x = x_ref[...]
    mod = mod_ref[...]
    sh1 = mod[:, :, 0:d]
    sc1 = mod[:, :, d:2 * d]
    h = _rms(x, n1_ref[...] * (1.0 + sc1)) + sh1
    hb = h.reshape(m, d).astype(BF16)

    raw = lbraw_ref[...]
    e = jnp.exp(raw - jnp.max(raw, axis=0, keepdims=True))
    lb = e[0:1, :] / jnp.sum(e, axis=0, keepdims=True)

    w = HG_WIDTH

    def proj(c):
        return lax.dot_general(hb, wa_ref[c * w:(c + 1) * w, :], _NT, preferred_element_type=F32)

    def proj_t(c):
        return lax.dot_general(wa_ref[c * w:(c + 1) * w, :], hb, _NT, preferred_element_type=F32)

    def merge_gate(c):
        mgc = lax.dot_general(hb, wmg_ref[c * w:(c + 1) * w, :], _NT, preferred_element_type=F32)
        mg_ref[:, c * w:(c + 1) * w] = _sigmoid(mgc).astype(BF16)

    def fox(c, ref, scale=None):
        y = proj_t(c) if fox_transposed else proj(c)
        y = y if scale is None else y * scale
        if fox_transposed:
            ref[0] = y.astype(ref.dtype)
        else:
            ref[...] = y.astype(ref.dtype)

    assert wmg_ref.shape[0] == 4 * w
    merge_gate(0)
    q_ref[...] = proj(0) * (HG_DK ** -0.5)
    merge_gate(1)
    v_ref[...] = proj(2).astype(BF16)
    merge_gate(2)
    fox(4, fq_ref, (LOG2E if fox_transposed else 1.0) * FOX_DH ** -0.5)
    merge_gate(3)
    fox(5, fk_ref)
    s_pos, s_neg = _sigmoid_pair(proj(1))
    g_ref[...] = jnp.log(lb + (1.0 - lb) * s_pos)
    k_ref[...] = (1.0 - lb) * s_neg
    fox(6, fv_ref)
    hg = proj(3)
    gate_ref[...] = (hg * _sigmoid(hg)).astype(BF16)
    z = lax.dot_general(hb, wff_ref[...], _NT, preferred_element_type=F32) + bf_ref[...]
    lf = jnp.minimum(z, 0.0) - jnp.log1p(jnp.exp(-jnp.abs(z)))
    lf128_ref[...] = lf
    lf8_ref[...] = lf[:, 0:FOX_HEADS]


def _inproj(x3, mod3, n1, lbraw, bf128, wa, wff, wmg, groups, rows, fox_transposed):
    assert groups == 1 or not fox_transposed
    nb, r_all, d = x3.shape
    nt = r_all // rows
    steps = (nb // groups) * nt
    t = nb * r_all
    m = groups * rows
    xmap = lambda s: (s // nt, s % nt, 0)
    mmap = lambda s: (s // nt, 0, 0)
    omap = lambda s: (s, 0)

    def out(width, dtype):
        return jax.ShapeDtypeStruct((t, width), dtype), pl.BlockSpec((m, width), omap)

    def fox(dtype):
        if not fox_transposed:
            return out(FOX_WIDTH, dtype)
        return (jax.ShapeDtypeStruct((nb, FOX_WIDTH, r_all), dtype),
                pl.BlockSpec((1, FOX_WIDTH, m), lambda s: (s // nt, 0, s % nt)))

    outs = [out(HG_WIDTH, F32), out(HG_WIDTH, F32), out(HG_WIDTH, F32), out(HG_WIDTH, BF16),
            out(HG_WIDTH, BF16), fox(BF16), fox(F32), fox(F32),
            out(FOX_HEADS, F32), out(LANES, F32), out(wmg.shape[0], BF16)]
    return pl.pallas_call(
        functools.partial(_inproj_kernel, fox_transposed=fox_transposed),
        grid=(steps,),
        in_specs=[pl.BlockSpec((groups, rows, d), xmap),
                  pl.BlockSpec((groups, 1, mod3.shape[2]), mmap),
                  _resident(n1.shape), _resident(lbraw.shape), _resident(bf128.shape),
                  _resident(wa.shape), _resident(wff.shape), _resident(wmg.shape)],
        out_specs=[o[1] for o in outs],
        out_shape=[o[0] for o in outs],
        compiler_params=_params("arbitrary"),
        name="inproj",
    )(x3, mod3, n1, lbraw, bf128, wa, wff, wmg)


def _decay_kernel(lf_ref, place_ref, bk_ref, bqt_ref):
    seq = lf_ref.shape[1]
    lane = lax.broadcasted_iota(jnp.int32, (LANES, LANES), 1)
    slot = lane & (BIAS_SLOTS - 1)
    used = lane < FOX_HEADS * BIAS_SLOTS
    ones_k = jnp.where(jnp.logical_and(used, slot < N_PIECES), 1.0, 0.0)
    ones_q = jnp.where(jnp.logical_and(used, jnp.logical_and(slot >= N_PIECES, slot < 2 * N_PIECES)), 1.0, 0.0)
    carry = jnp.zeros((1, LANES), F32)
    for j in range(seq // LANES):
        sl = slice(j * LANES, (j + 1) * LANES)
        cs = _cumsum_rows(lf_ref[0, sl, :], LANES) + carry
        carry = cs[LANES - 1:LANES, :]
        pieces = _split3(cs * LOG2E)

        def place(base):
            return sum(jnp.dot(pieces[j], place_ref[base + j], preferred_element_type=F32)
                       for j in range(N_PIECES))

        bk_ref[0, sl, :] = (ones_k - place(N_PIECES)).astype(BF16)
        bqt_ref[0, :, sl] = (ones_q + place(0)).T.astype(BF16)


def _decay(lf3, place):
    b, seq, _ = lf3.shape
    return pl.pallas_call(
        _decay_kernel,
        grid=(b,),
        in_specs=[pl.BlockSpec((1, seq, LANES), lambda i: (i, 0, 0)), _resident(place.shape)],
        out_specs=[pl.BlockSpec((1, seq, LANES), lambda i: (i, 0, 0)),
                   pl.BlockSpec((1, LANES, seq), lambda i: (i, 0, 0))],
        out_shape=[jax.ShapeDtypeStruct((b, seq, LANES), BF16),
                   jax.ShapeDtypeStruct((b, LANES, seq), BF16)],
        compiler_params=_params("arbitrary"),
        name="decay",
    )(lf3, place)


def _hgrn_decays(q, k, g, chunk):
    rows = q.shape[0]
    n_chunks = rows // chunk
    b = _cumsum_rows(g, chunk)
    qt = (q * jnp.exp(b)).astype(BF16)
    kt = (k * jnp.exp(-b)).astype(BF16)
    b3 = b.reshape(n_chunks, chunk, HG_DK)
    bl3 = b3[:, chunk - 1:chunk, :]
    kd = (k.reshape(n_chunks, chunk, HG_DK) * jnp.exp(bl3 - b3)).reshape(rows, HG_DK).astype(BF16)
    return qt, kt, kd, jnp.exp(bl3)


def _hgrn_products(qt, kt, kd, v, chunk):
    rows = qt.shape[0]
    n_chunks = rows // chunk
    shift = chunk.bit_length() - 1
    a = lax.dot_general(qt, kt, _NT, preferred_element_type=F32)
    v_t = v.astype(F32).T.astype(BF16)
    col_chunk = lax.broadcasted_iota(jnp.int32, v_t.shape, 1) >> shift
    v_blocks = jnp.concatenate([jnp.where(col_chunk == c, v_t, jnp.zeros_like(v_t)) for c in range(n_chunks)],
                               axis=0)
    incr = jnp.dot(v_blocks, kd, preferred_element_type=F32)
    ri = lax.broadcasted_iota(jnp.int32, (rows, rows), 0)
    ci = lax.broadcasted_iota(jnp.int32, (rows, rows), 1)
    keep = (ri - ci).astype(jnp.uint32) <= (ri & (chunk - 1)).astype(jnp.uint32)
    o_intra = jnp.dot(jnp.where(keep, a, 0.0).astype(BF16), v, preferred_element_type=F32)
    return o_intra, incr


def _hgrn_blocks(blocks, chunk, state_in, state_out):
    n_chunks = LANES // chunk
    decays = [_hgrn_decays(q, k, g, chunk) for (q, k, g, _) in blocks]
    products = [_hgrn_products(qt, kt, kd, blk[3], chunk) for (qt, kt, kd, _), blk in zip(decays, blocks)]
    entering = []
    st = None
    for n in range(len(blocks) * n_chunks):
        j, c = divmod(n, n_chunks)
        st = state_in(n, st)
        entering.append(st.astype(BF16))
        st = st * decays[j][3][c] + products[j][1][c * HG_DV:(c + 1) * HG_DV, :]
        state_out(n, st)
    outs = []
    for j, ((qt, _, _, _), (o_intra, _)) in enumerate(zip(decays, products)):
        o_inter = [lax.dot_general(qt[c * chunk:(c + 1) * chunk, :], entering[j * n_chunks + c], _NT,
                                   preferred_element_type=F32) for c in range(n_chunks)]
        outs.append(o_intra + jnp.concatenate(o_inter, axis=0))
    return outs


HGRN_GROUP = 8


def _hgrn_prompt_kernel(q_ref, k_ref, g_ref, v_ref, o_ref, s_ref, *, chunk):
    seq = q_ref.shape[0]
    group_rows = HGRN_GROUP * LANES

    def body(i, st0):
        base = i * group_rows
        row_slices = [pl.ds(base + j * LANES, LANES) for j in range(HGRN_GROUP)]
        blocks = [(q_ref[rs, :], k_ref[rs, :], g_ref[rs, :], v_ref[rs, :]) for rs in row_slices]
        last = []
        outs = _hgrn_blocks(blocks, chunk, lambda n, prev: st0 if prev is None else prev,
                            lambda n, st: last.append(st))
        for rs, o in zip(row_slices, outs):
            o_ref[rs, :] = o
        return last[-1]

    st = jnp.zeros((HG_DV, HG_DK), F32)
    for i in range(seq // group_rows):
        st = body(i, st)
    s_ref[0, 0] = st.T


def _hgrn_prompt(q, k, g, v, batch, seq, chunk):
    spec = pl.BlockSpec((seq, HG_DK), lambda b, h: (b, h))
    return pl.pallas_call(
        functools.partial(_hgrn_prompt_kernel, chunk=chunk),
        grid=(batch, HG_HEADS),
        in_specs=[spec, spec, spec, spec],
        out_specs=[spec, pl.BlockSpec((1, 1, HG_DK, HG_DV), lambda b, h: (b, h, 0, 0))],
        out_shape=[jax.ShapeDtypeStruct((batch * seq, HG_WIDTH), F32),
                   jax.ShapeDtypeStruct((batch, HG_HEADS, HG_DK, HG_DV), F32)],
        compiler_params=_params("arbitrary", "arbitrary"),
        name="hgrn_prompt",
    )(q, k, g, v)


def _hgrn_sample_kernel(q_ref, k_ref, g_ref, v_ref, s0_ref, o_ref, s_ref, *, chunk):
    per = LANES // chunk

    def state_in(n, prev):
        h, c = divmod(n, per)
        return s0_ref[c, h].T

    def state_out(n, st):
        h, c = divmod(n, per)
        s_ref[c, h] = st.T

    heads = [slice(h * HG_DK, (h + 1) * HG_DK) for h in range(HG_HEADS)]
    outs = _hgrn_blocks([(q_ref[:, hs], k_ref[:, hs], g_ref[:, hs], v_ref[:, hs]) for hs in heads],
                        chunk, state_in, state_out)
    for hs, o in zip(heads, outs):
        o_ref[:, hs] = o


def _hgrn_sample(q, k, g, v, s0, chunk):
    t = q.shape[0]
    per = LANES // chunk
    spec = pl.BlockSpec((LANES, HG_WIDTH), lambda i: (i, 0))
    sspec = pl.BlockSpec((per, HG_HEADS, HG_DK, HG_DV), lambda i: (i, 0, 0, 0))
    return pl.pallas_call(
        functools.partial(_hgrn_sample_kernel, chunk=chunk),
        grid=(t // LANES,),
        in_specs=[spec, spec, spec, spec, sspec],
        out_specs=[spec, sspec],
        out_shape=[jax.ShapeDtypeStruct((t, HG_WIDTH), F32),
                   jax.ShapeDtypeStruct(s0.shape, F32)],
        compiler_params=_params("arbitrary"),
        name="hgrn_sample",
    )(q, k, g, v, s0)


ONES_ROWS = 16
PAGE_PREFETCH_DEPTH = 2


def _prompt_attention(uq_ref, uk_ref, um_ref, qt_ref, kt_ref, vt_ref, bk_ref, bqt_ref, o_ref,
                      kn_ref, va_ref, rhs_ref, mask_ref, s_ref, p_ref, *, pair, tq, tk, n_units, limits):
    seq = kt_ref.shape[1]
    n_heads = LANES // FOX_DH

    def prepare():
        for c in range(seq // LANES):
            cs = slice(c * LANES, (c + 1) * LANES)
            kn_ref[cs, 0:LANES] = kt_ref[:, cs].T.astype(BF16)
        kn_ref[:, LANES:2 * LANES] = bk_ref[...]
        for c in range(seq // tk):
            for e in range(n_heads):
                va_ref[c, e, 0:FOX_DH, :] = vt_ref[e * FOX_DH:(e + 1) * FOX_DH, c * tk:(c + 1) * tk].astype(BF16)
                va_ref[c, e, FOX_DH:FOX_DH + ONES_ROWS, :] = jnp.ones((ONES_ROWS, tk), BF16)
        row = lax.broadcasted_iota(jnp.int32, (LANES, tq), 0)
        for i in range(seq // tq):
            qs = slice(i * tq, (i + 1) * tq)
            qt = qt_ref[:, qs]
            bqt = bqt_ref[:, qs]
            for e in range(n_heads):
                rhs_ref[i, e, 0:LANES, :] = jnp.where(row // FOX_DH == e, qt, jnp.zeros_like(qt))
                rhs_ref[i, e, LANES:2 * LANES, :] = jnp.where(row // BIAS_SLOTS == pair * n_heads + e, bqt,
                                                              jnp.zeros_like(bqt))
        p_ref[1] = jnp.zeros(p_ref.shape[1:], BF16)
        key_minus_query = (lax.broadcasted_iota(jnp.int32, (tk, tq), 0)
                           - lax.broadcasted_iota(jnp.int32, (tk, tq), 1))
        for n, limit in enumerate(limits):
            mask_ref[n] = jnp.where(key_minus_query <= limit, 0.0, NEG_INF)
        rows = lambda n, v: [jnp.full((n, tq), v, F32) for _ in range(n_heads)]
        scores(1, 0)
        return rows(1, NEG_INF), rows(1, 1.0), rows(1, 1.0), rows(FOX_DH, 0.0)

    def scores(u, slot):
        lhs = kn_ref[pl.ds(pl.multiple_of(uk_ref[u] * tk, tk), tk), :]
        for e in range(n_heads):
            s_ref[slot, e] = jnp.dot(lhs, rhs_ref[uq_ref[u], e], preferred_element_type=F32)

    def numerators(u, slot, ms):
        first = uk_ref[u] == 0
        mask = mask_ref[um_ref[u]]
        new_ms, alphas = [], []
        for e in range(n_heads):
            m = jnp.where(first, NEG_INF, ms[e])
            s = s_ref[slot, e] + mask
            m_new = jnp.maximum(m, jnp.max(s, axis=0, keepdims=True))
            p_ref[slot, e] = jnp.exp2(s - m_new).astype(BF16)
            new_ms.append(m_new)
            alphas.append(jnp.exp2(m - m_new))
        return new_ms, alphas

    def values(u, slot, alphas, ls, accs):
        new_ls, new_accs = [], []
        for e in range(n_heads):
            pv = jnp.dot(va_ref[uk_ref[u], e], p_ref[slot, e], preferred_element_type=F32)
            new_ls.append(alphas[e] * ls[e] + pv[FOX_DH:FOX_DH + 1, :])
            new_accs.append(alphas[e] * accs[e] + pv[0:FOX_DH, :])
        o_t = jnp.concatenate([new_accs[e] / new_ls[e] for e in range(n_heads)], axis=0)
        o_ref[pl.ds(pl.multiple_of(uq_ref[u] * tq, tq), tq), :] = o_t.T.astype(BF16)
        return new_ls, new_accs

    def step(u, slot, carry):
        ms, alphas, ls, accs = carry
        new_ls, new_accs = values(u - 1, 1 - slot, alphas, ls, accs)
        scores(u + 1, 1 - slot)
        new_ms, new_alphas = numerators(u, slot, ms)
        return new_ms, new_alphas, new_ls, new_accs

    def body(k, carry):
        carry = step(2 * k + 1, 0, carry)
        return step(2 * k + 2, 1, carry)

    def run(lo, hi, state):
        for k in range(lo, hi):
            state = body(k, state)
        return state

    def finish(state):
        _, alphas, ls, accs = state
        values(n_units, 1, alphas, ls, accs)

    return prepare, run, finish


def _sample_attend(q, k_new, v_new, lf_new, sfx_ref, k_refs, v_refs, lf_refs):
    n_pages = len(k_refs)
    steps = q.shape[0]
    rows = FOX_HEADS * steps
    pad = PAGE_SIZE - steps

    def per_head_rows(x8):
        return jnp.broadcast_to(x8[:, None, :], (FOX_HEADS, steps, x8.shape[1])).reshape(rows, x8.shape[1])

    q = q.astype(F32)
    q_rows = jnp.concatenate([q] * FOX_HEADS, axis=0)
    row_head = lax.broadcasted_iota(jnp.int32, (rows, FOX_WIDTH), 0) // steps
    lane_head = lax.broadcasted_iota(jnp.int32, (rows, FOX_WIDTH), 1) // FOX_DH
    head_mask = row_head == lane_head
    qbd = jnp.where(head_mask, q_rows, 0.0).astype(BF16)

    pn = _cumsum_rows(lf_new, steps)
    pn_rows = jnp.concatenate([pn] * FOX_HEADS, axis=0)
    r_h = lax.broadcasted_iota(jnp.int32, (rows, LANES), 0) // steps
    r_i = lax.broadcasted_iota(jnp.int32, (rows, LANES), 0) % steps
    c_l = lax.broadcasted_iota(jnp.int32, (rows, LANES), 1)
    pcol = jnp.sum(jnp.where(c_l == r_h, pn_rows, 0.0), axis=1, keepdims=True)
    pn_t = jnp.concatenate([pn, jnp.zeros((pad, LANES), F32)], axis=0).T[0:FOX_HEADS, :]

    lf_all = jnp.concatenate([lf_refs[j][...] for j in range(n_pages)], axis=0)
    sfx = _dot_f32_lhs(lf_all, sfx_ref[...])
    carry = jnp.zeros((FOX_HEADS, PAGE_SIZE), F32)
    page_bias = [None] * n_pages
    for j in reversed(range(n_pages)):
        blk = sfx[j * FOX_HEADS:(j + 1) * FOX_HEADS, :]
        page_bias[j] = blk[:, 0:PAGE_SIZE] + carry
        carry = carry + blk[:, PAGE_SIZE:2 * PAGE_SIZE]

    s_tiles = []
    for j in range(n_pages):
        s = jnp.dot(qbd, k_refs[j][...].astype(BF16), preferred_element_type=F32)
        s_tiles.append(s + (per_head_rows(page_bias[j]) + pcol))
    kn = jnp.concatenate([k_new, jnp.zeros((pad, FOX_WIDTH), F32)], axis=0).astype(BF16)
    s_new = lax.dot_general(qbd, kn, _NT, preferred_element_type=F32) + (pcol - per_head_rows(pn_t))
    s_tiles.append(jnp.where(c_l <= r_i, s_new, NEG_INF))

    m_el = s_tiles[0]
    for s in s_tiles[1:]:
        m_el = jnp.maximum(m_el, s)
    m = jnp.max(m_el, axis=1, keepdims=True)
    p_tiles = [jnp.exp(s - m) for s in s_tiles]
    l_el = p_tiles[0]
    for p in p_tiles[1:]:
        l_el = l_el + p
    l = jnp.sum(l_el, axis=1, keepdims=True)

    vn = jnp.concatenate([v_new, jnp.zeros((pad, FOX_WIDTH), F32)], axis=0).astype(BF16)
    o = jnp.dot(p_tiles[n_pages].astype(BF16), vn, preferred_element_type=F32)
    for j in range(n_pages):
        o = o + lax.dot_general(p_tiles[j].astype(BF16), v_refs[j][...].astype(BF16), _NT,
                                preferred_element_type=F32)
    o = jnp.where(head_mask, o / l, 0.0)
    return jnp.sum(o.reshape(FOX_HEADS, steps, FOX_WIDTH), axis=0)


def _fox_kernel(uq_ref, uk_ref, um_ref, pt_ref,
                qt_ref, kt_ref, vt_ref, bk_ref, bqt_ref,
                sq_ref, sk_ref, sv_ref, slf_ref, sfx_ref, ck_hbm, cv_hbm, cl_hbm,
                o_ref, so_ref,
                kn_ref, va_ref, rhs_ref, mask_ref, s_ref, p_ref, k_buf, v_buf, lf_buf, sems,
                *, tq, tk, n_units, limits, n_pages):
    step = pl.program_id(0) * pl.num_programs(1) + pl.program_id(1)
    n_steps = pl.num_programs(0) * pl.num_programs(1)
    per_step = sq_ref.shape[0]
    depth = PAGE_PREFETCH_DEPTH
    assert 0 < depth < per_step
    last_seq = n_steps * per_step - 1

    def page_copies(seq, slot):
        copies = []
        for j in range(n_pages):
            page = pt_ref[seq * n_pages + j]
            copies.append(pltpu.make_async_copy(ck_hbm.at[page], k_buf.at[slot, j], sems.at[0, slot]))
            copies.append(pltpu.make_async_copy(cv_hbm.at[page], v_buf.at[slot, j], sems.at[0, slot]))
            copies.append(pltpu.make_async_copy(cl_hbm.at[page], lf_buf.at[slot, j], sems.at[1, slot]))
        return copies

    @pl.when(step == 0)
    def _():
        for d in range(depth):
            for c in page_copies(d, d):
                c.start()

    prepare, run, finish = _prompt_attention(
        uq_ref, uk_ref, um_ref, qt_ref, kt_ref, vt_ref, bk_ref, bqt_ref, o_ref,
        kn_ref, va_ref, rhs_ref, mask_ref, s_ref, p_ref,
        pair=pl.program_id(1), tq=tq, tk=tk, n_units=n_units, limits=limits)
    state = prepare()
    n_pairs = n_units // 2
    bounds = [(j * n_pairs) // per_step for j in range(per_step + 1)]
    for j in range(per_step):
        seq = step * per_step + j
        slot = j
        ahead = jnp.minimum(seq + depth, last_seq)
        for c in page_copies(ahead, (j + depth) % per_step):
            c.start()
        for c in page_copies(seq, slot):
            c.wait()
        so_ref[j] = _sample_attend(
            sq_ref[j], sk_ref[j], sv_ref[j], slf_ref[j], sfx_ref,
            [k_buf.at[slot, n] for n in range(n_pages)], [v_buf.at[slot, n] for n in range(n_pages)],
            [lf_buf.at[slot, n] for n in range(n_pages)]).astype(BF16)
        state = run(bounds[j], bounds[j + 1], state)
    finish(state)

    @pl.when(step == n_steps - 1)
    def _():
        for d in range(depth):
            for c in page_copies(last_seq, d):
                c.wait()


def _fox(fqt, fkt, fvt, bk, bqt, page_table, sq3, sk3, sv3, slf3, sfx_mat, cache_kt, cache_vt, cache_lft,
         tq, tk):
    batch, _, seq = fqt.shape
    nseq, n_pages = page_table.shape
    steps = sq3.shape[1]
    assert tk % tq == 0 and seq % tk == 0
    nq = seq // tq
    pairs = FOX_WIDTH // LANES
    n_heads = LANES // FOX_DH
    per_step = nseq // (batch * pairs)
    assert per_step * batch * pairs == nseq and per_step > PAGE_PREFETCH_DEPTH
    units = [(i, j) for i in range(nq) for j in range((i * tq) // tk + 1)]
    n_units = len(units)
    assert n_units % 2 == 0
    padded = [units[0]] + units + [units[-1]]
    unit_limits = [min(i * tq - j * tk, tk - 1) for i, j in padded]
    limits = tuple(sorted(set(unit_limits)))
    table = lambda vals: jnp.asarray(vals, jnp.int32)
    uq, uk = table([u[0] for u in padded]), table([u[1] for u in padded])
    um = table([limits.index(v) for v in unit_limits])
    head_pair = lambda b, p, *_: (b, p, 0)
    per_batch = lambda b, p, *_: (b, 0, 0)
    sample = lambda w: pl.BlockSpec((per_step, steps, w), lambda b, p, *_: (b * pairs + p, 0, 0))
    in_hbm = pl.BlockSpec(memory_space=pl.ANY)
    grid_spec = pltpu.PrefetchScalarGridSpec(
        num_scalar_prefetch=4,
        grid=(batch, pairs),
        in_specs=[pl.BlockSpec((None, LANES, seq), head_pair),
                  pl.BlockSpec((None, LANES, seq), head_pair),
                  pl.BlockSpec((None, LANES, seq), head_pair),
                  pl.BlockSpec((None, seq, LANES), per_batch),
                  pl.BlockSpec((None, LANES, seq), per_batch),
                  sample(FOX_WIDTH), sample(FOX_WIDTH), sample(FOX_WIDTH), sample(LANES),
                  pl.BlockSpec(sfx_mat.shape, lambda b, p, *_: (0, 0)), in_hbm, in_hbm, in_hbm],
        out_specs=[pl.BlockSpec((seq, LANES), lambda b, p, *_: (b, p)), sample(FOX_WIDTH)],
        scratch_shapes=[pltpu.VMEM((seq, 2 * LANES), BF16),
                        pltpu.VMEM((seq // tk, n_heads, FOX_DH + ONES_ROWS, tk), BF16),
                        pltpu.VMEM((nq, n_heads, 2 * LANES, tq), BF16),
                        pltpu.VMEM((len(limits), tk, tq), F32),
                        pltpu.VMEM((2, n_heads, tk, tq), F32),
                        pltpu.VMEM((2, n_heads, tk, tq), BF16),
                        pltpu.VMEM((per_step, n_pages, FOX_WIDTH, PAGE_SIZE), cache_kt.dtype),
                        pltpu.VMEM((per_step, n_pages, FOX_WIDTH, PAGE_SIZE), cache_vt.dtype),
                        pltpu.VMEM((per_step, n_pages, FOX_HEADS, PAGE_SIZE), cache_lft.dtype),
                        pltpu.SemaphoreType.DMA((2, per_step))],
    )
    return pl.pallas_call(
        functools.partial(_fox_kernel, tq=tq, tk=tk, n_units=n_units, limits=limits, n_pages=n_pages),
        grid_spec=grid_spec,
        out_shape=[jax.ShapeDtypeStruct((batch * seq, FOX_WIDTH), BF16),
                   jax.ShapeDtypeStruct((nseq, steps, FOX_WIDTH), BF16)],
        compiler_params=_params("arbitrary", "arbitrary"),
        name="fox",
    )(uq, uk, um, page_table.reshape(-1), fqt, fkt, fvt, bk, bqt, sq3, sk3, sv3, slf3, sfx_mat,
      cache_kt, cache_vt, cache_lft)


def _post_kernel(x_ref, mod_ref, oh_ref, gate_ref, of_ref, mg_ref, onorm_ref, n1post_ref, n2pre_ref,
                 n2post_ref, wbh_ref, wbf_ref, wout_ref, wup_ref, wdn_ref, y_ref, *, ff_chunk):
    g_, r_, d = x_ref.shape
    m = g_ * r_
    mod = mod_ref[...]
    gt1 = mod[:, :, 2 * d:3 * d]
    sh2 = mod[:, :, 3 * d:4 * d]
    sc2 = mod[:, :, 4 * d:5 * d]
    gt2 = mod[:, :, 5 * d:6 * d]

    oh = oh_ref[...]
    parts = [_rms(oh[:, h * HG_DV:(h + 1) * HG_DV], onorm_ref[...]) for h in range(HG_HEADS)]
    ohn = (jnp.concatenate(parts, axis=1) * gate_ref[...].astype(F32)).astype(BF16)
    br_h = jnp.dot(ohn, wbh_ref[...], preferred_element_type=F32)
    br_f = jnp.dot(of_ref[...], wbf_ref[...], preferred_element_type=F32)
    mg = mg_ref[...].astype(F32)
    z = (mg[:, 0:d] * br_h + mg[:, d:2 * d] * br_f).astype(BF16)
    y = jnp.dot(z, wout_ref[...], preferred_element_type=F32).reshape(g_, r_, d)
    x1 = x_ref[...] + _rms(y, gt1 * n1post_ref[...])
    h2 = (_rms(x1, n2pre_ref[...] * (1.0 + sc2)) + sh2).reshape(m, d).astype(BF16)
    u = jnp.zeros((m, d), F32)
    for c in range(wup_ref.shape[1] // ff_chunk):
        cs = slice(c * ff_chunk, (c + 1) * ff_chunk)
        a = jnp.maximum(jnp.dot(h2, wup_ref[:, cs], preferred_element_type=F32), 0.0)
        u = u + jnp.dot((a * a).astype(BF16), wdn_ref[cs, :], preferred_element_type=F32)
    y_ref[...] = x1 + _rms(u.reshape(g_, r_, d), gt2 * n2post_ref[...])


def _post(x3, mod3, oh, gate, of, mg, onorm, n1post, n2pre, n2post, wbh, wbf, wout, wup, wdn,
          groups, rows, ff_chunk=1024):
    nb, r_all, d = x3.shape
    nt = r_all // rows
    steps = (nb // groups) * nt
    m = groups * rows
    xmap = lambda s: (s // nt, s % nt, 0)
    mmap = lambda s: (s // nt, 0, 0)
    tmap = lambda s: (s, 0)
    tok = lambda a: pl.BlockSpec((m, a.shape[1]), tmap)
    consts = [onorm, n1post, n2pre, n2post, wbh, wbf, wout, wup, wdn]
    return pl.pallas_call(
        functools.partial(_post_kernel, ff_chunk=ff_chunk),
        grid=(steps,),
        in_specs=[pl.BlockSpec((groups, rows, d), xmap),
                  pl.BlockSpec((groups, 1, mod3.shape[2]), mmap),
                  tok(oh), tok(gate), tok(of), tok(mg)] + [_resident(c.shape) for c in consts],
        out_specs=pl.BlockSpec((groups, rows, d), xmap),
        out_shape=jax.ShapeDtypeStruct(x3.shape, F32),
        compiler_params=_params("arbitrary"),
        name="post",
    )(x3, mod3, oh, gate, of, mg, *consts)


def _bias_place_matrices():
    shape = (2 * N_PIECES, LANES, LANES)
    s = lax.broadcasted_iota(jnp.int32, shape, 0)
    r = lax.broadcasted_iota(jnp.int32, shape, 1)
    c = lax.broadcasted_iota(jnp.int32, shape, 2)
    return jnp.logical_and(r < FOX_HEADS, c == BIAS_SLOTS * r + s).astype(BF16)


def _page_suffix_matrix():
    r = lax.broadcasted_iota(jnp.int32, (PAGE_SIZE, 2 * PAGE_SIZE), 0)
    c = lax.broadcasted_iota(jnp.int32, (PAGE_SIZE, 2 * PAGE_SIZE), 1)
    return jnp.logical_or(r > c, c >= PAGE_SIZE).astype(BF16)


def kernel(x_prompt, x_sample, c_prompt, c_sample, cache_k, cache_v, cache_logf, state_hgrn, page_table,
           ada_w, ada_b, norm_mix_pre, norm_mix_post, norm_mlp_pre, norm_mlp_post, w_in,
           hgrn_lower_bounds, hgrn_onorm, fox_b_f, w_br_h, w_br_f, w_out, w_mlp_up, w_mlp_down):
    batch, seq, d = x_prompt.shape
    nseq, steps, _ = x_sample.shape
    layer = 0
    n_phys = cache_k.shape[1]

    w_t = jnp.transpose(w_in[layer]).astype(BF16)
    n_a = 4 * HG_WIDTH + 3 * FOX_WIDTH
    wa = w_t[:n_a]
    wff = jnp.pad(w_t[n_a:n_a + FOX_HEADS], ((0, LANES - FOX_HEADS), (0, 0)))
    wmg = w_t[n_a + FOX_HEADS:]
    bf128 = jnp.pad(fox_b_f[layer], (0, LANES - FOX_HEADS)).reshape(1, LANES)
    lbraw = hgrn_lower_bounds
    vec3 = lambda v: v.reshape(1, 1, -1)
    wbh, wbf = w_br_h[layer].astype(BF16), w_br_f[layer].astype(BF16)
    wout = w_out[layer].astype(BF16)
    wup, wdn = w_mlp_up[layer].astype(BF16), w_mlp_down[layer].astype(BF16)
    onorm = hgrn_onorm[layer].reshape(1, HG_DV)

    n_c = batch + nseq
    c_pad = -n_c % (2 * SUBLANES)
    c_all = jnp.concatenate([c_prompt, c_sample, jnp.zeros((c_pad, d), F32)], axis=0)
    mod = _ada(c_all, ada_w[layer], ada_b[layer].reshape(1, -1))
    mod_p = mod[:batch].reshape(batch, 1, -1)
    mod_s = mod[batch:n_c].reshape(nseq, 1, -1)

    tm = TOKEN_TILE
    sgroups = tm // steps
    proj_args = (vec3(norm_mix_pre[layer]), lbraw, bf128, wa, wff, wmg)
    (q_p, g_p, k_p, v_p, gate_p, fq_p, fk_p, fv_p, lf8_p, lf128_p, mg_p) = _inproj(
        x_prompt, mod_p, *proj_args, groups=1, rows=tm, fox_transposed=True)
    (q_s, g_s, k_s, v_s, gate_s, fq_s, fk_s, fv_s, lf8_s, lf128_s, mg_s) = _inproj(
        x_sample, mod_s, *proj_args, groups=sgroups, rows=steps, fox_transposed=False)

    chunk_p = HG_CHUNK if seq % HG_CHUNK == 0 else seq
    oh_p, s_p = _hgrn_prompt(q_p, k_p, g_p, v_p, batch, seq, chunk_p)
    oh_s, s_s = _hgrn_sample(q_s, k_s, g_s, v_s, state_hgrn[layer], steps)

    tq, tk = FOX_Q_TILE, FOX_K_TILE
    bias_k, bias_qt = _decay(lf128_p.reshape(batch, seq, LANES), _bias_place_matrices())
    ckt = jnp.transpose(cache_k[layer], (0, 2, 3, 1)).reshape(n_phys, FOX_WIDTH, PAGE_SIZE)
    cvt = jnp.transpose(cache_v[layer], (0, 2, 3, 1)).reshape(n_phys, FOX_WIDTH, PAGE_SIZE)
    clt = jnp.transpose(cache_logf[layer], (0, 2, 1))
    per_seq = lambda a: a.reshape(nseq, steps, a.shape[-1])
    of_p, of_s = _fox(fq_p, fk_p, fv_p, bias_k, bias_qt, page_table, per_seq(fq_s), per_seq(fk_s), per_seq(fv_s),
                      per_seq(lf128_s), _page_suffix_matrix(), ckt, cvt, clt, tq, tk)
    of_s = of_s.reshape(nseq * steps, FOX_WIDTH)

    post_args = (onorm, vec3(norm_mix_post[layer]), vec3(norm_mlp_pre[layer]), vec3(norm_mlp_post[layer]),
                 wbh, wbf, wout, wup, wdn)
    y_p = _post(x_prompt, mod_p, oh_p, gate_p, of_p, mg_p, *post_args, groups=1, rows=tm)
    y_s = _post(x_sample, mod_s, oh_s, gate_s, of_s, mg_s, *post_args, groups=sgroups, rows=steps)

    k_prompt = fk_p.reshape(1, batch, FOX_HEADS, FOX_DH, seq).transpose(0, 1, 4, 2, 3)
    v_prompt = fv_p.reshape(1, batch, FOX_HEADS, FOX_DH, seq).transpose(0, 1, 4, 2, 3)
    logf_prompt = lf8_p.reshape(1, batch, seq, FOX_HEADS)
    k_sample = fk_s.reshape(1, nseq, steps, FOX_HEADS, FOX_DH)
    v_sample = fv_s.reshape(1, nseq, steps, FOX_HEADS, FOX_DH)
    logf_sample = lf8_s.reshape(1, nseq, steps, FOX_HEADS)
    return (y_p, y_s, k_prompt, v_prompt, logf_prompt, s_p[None], k_sample, v_sample, logf_sample, s_s[None])
```

```python
import functools

import jax
import jax.numpy as jnp
from jax import lax
from jax.experimental import pallas as pl
from jax.experimental.pallas import tpu as pltpu

F32 = jnp.float32
BF16 = jnp.bfloat16

LANES = 128
SUBLANES = 8
VMEM_LIMIT_BYTES = 56 * 1024 * 1024
TOKEN_TILE = 512
FOX_Q_TILE = 256
FOX_K_TILE = 512

HG_HEADS = 4
HG_DK = 128
HG_DV = 128
HG_CHUNK = 32
FOX_HEADS = 8
FOX_DH = 64
FOX_WIDTH = FOX_HEADS * FOX_DH
HG_WIDTH = HG_HEADS * HG_DV
PAGE_SIZE = 128
RMS_EPS = 1e-6
NEG_INF = float("-inf")
LOG2E = 1.4426950408889634
N_PIECES = 3
BIAS_SLOTS = 8

_NT = (((1,), (1,)), ((), ()))


def _params(*sem):
    return pltpu.CompilerParams(dimension_semantics=sem, vmem_limit_bytes=VMEM_LIMIT_BYTES)


def _resident(shape):
    nd = len(shape)
    return pl.BlockSpec(shape, lambda *_: (0,) * nd, pipeline_mode=pl.Buffered(1))


def _sigmoid_pair(x):
    t = jnp.exp(-jnp.abs(x))
    r = 1.0 / (1.0 + t)
    tr = t * r
    pos = x >= 0
    return jnp.where(pos, r, tr), jnp.where(pos, tr, r)


def _sigmoid(x):
    return 0.5 * jnp.tanh(0.5 * x) + 0.5


def _split3(x):
    hi = x.astype(BF16)
    r1 = x - hi.astype(F32)
    mid = r1.astype(BF16)
    lo = (r1 - mid.astype(F32)).astype(BF16)
    return hi, mid, lo


def _dot_f32_lhs(x, w):
    hi, mid, lo = _split3(x)
    d = lambda p: jnp.dot(p, w, preferred_element_type=F32)
    return (d(lo) + d(mid)) + d(hi)


def _cumsum_rows(x, period):
    row = lax.broadcasted_iota(jnp.int32, x.shape, 0) & (period - 1)
    s = 1
    while s < period:
        x = x + jnp.where(row >= s, pltpu.roll(x, s, axis=0), 0.0)
        s *= 2
    return x


def _rms(x, w):
    return x * lax.rsqrt(jnp.mean(x * x, axis=-1, keepdims=True) + RMS_EPS) * w


def _ada_kernel(c_ref, w_ref, b_ref, o_ref):
    c = c_ref[...]
    s, _ = _sigmoid_pair(c)
    a = (c * s).astype(BF16)
    o_ref[...] = jnp.dot(a, w_ref[...].astype(BF16), preferred_element_type=F32) + b_ref[...]


def _ada(c, w, b, tn=1536):
    m, d = c.shape
    n = w.shape[1]
    return pl.pallas_call(
        _ada_kernel,
        grid=(n // tn,),
        in_specs=[pl.BlockSpec((m, d), lambda j: (0, 0)),
                  pl.BlockSpec((d, tn), lambda j: (0, j)),
                  pl.BlockSpec((1, tn), lambda j: (0, j))],
        out_specs=pl.BlockSpec((m, tn), lambda j: (0, j)),
        out_shape=jax.ShapeDtypeStruct((m, n), F32),
        compiler_params=_params("arbitrary"),
        name="ada",
    )(c, w, b)


def _inproj_kernel(x_ref, mod_ref, n1_ref, lbraw_ref, bf_ref, wa_ref, wff_ref, wmg_ref,
                   q_ref, g_ref, k_ref, v_ref, gate_ref, fq_ref, fk_ref, fv_ref,
                   lf8_ref, lf128_ref, mg_ref, *, fox_transposed):
    g_, r_, d = x_ref.shape
    m = g_ * r_
    x = x_ref[...]
    mod = mod_ref[...]
    sh1 = mod[:, :, 0:d]
    sc1 = mod[:, :, d:2 * d]
    h = _rms(x, n1_ref[...] * (1.0 + sc1)) + sh1
    hb = h.reshape(m, d).astype(BF16)

    raw = lbraw_ref[...]
    e = jnp.exp(raw - jnp.max(raw, axis=0, keepdims=True))
    lb = e[0:1, :] / jnp.sum(e, axis=0, keepdims=True)

    w = HG_WIDTH

    def proj(c):
        return lax.dot_general(hb, wa_ref[c * w:(c + 1) * w, :], _NT, preferred_element_type=F32)

    def proj_t(c):
        return lax.dot_general(wa_ref[c * w:(c + 1) * w, :], hb, _NT, preferred_element_type=F32)

    def merge_gate(c):
        mgc = lax.dot_general(hb, wmg_ref[c * w:(c + 1) * w, :], _NT, preferred_element_type=F32)
        mg_ref[:, c * w:(c + 1) * w] = _sigmoid(mgc).astype(BF16)

    def fox(c, ref, scale=None):
        y = proj_t(c) if fox_transposed else proj(c)
        y = y if scale is None else y * scale
        if fox_transposed:
            ref[0] = y.astype(ref.dtype)
        else:
            ref[...] = y.astype(ref.dtype)

    assert wmg_ref.shape[0] == 4 * w
    merge_gate(0)
    q_ref[...] = proj(0) * (HG_DK ** -0.5)
    merge_gate(1)
    v_ref[...] = proj(2).astype(BF16)
    merge_gate(2)
    fox(4, fq_ref, (LOG2E if fox_transposed else 1.0) * FOX_DH ** -0.5)
    merge_gate(3)
    fox(5, fk_ref)
    s_pos, s_neg = _sigmoid_pair(proj(1))
    g_ref[...] = jnp.log(lb + (1.0 - lb) * s_pos)
    k_ref[...] = (1.0 - lb) * s_neg
    fox(6, fv_ref)
    hg = proj(3)
    gate_ref[...] = (hg * _sigmoid(hg)).astype(BF16)
    z = lax.dot_general(hb, wff_ref[...], _NT, preferred_element_type=F32) + bf_ref[...]
    lf = jnp.minimum(z, 0.0) - jnp.log1p(jnp.exp(-jnp.abs(z)))
    lf128_ref[...] = lf
    lf8_ref[...] = lf[:, 0:FOX_HEADS]


def _inproj(x3, mod3, n1, lbraw, bf128, wa, wff, wmg, groups, rows, fox_transposed):
    assert groups == 1 or not fox_transposed
    nb, r_all, d = x3.shape
    nt = r_all // rows
    steps = (nb // groups) * nt
    t = nb * r_all
    m = groups * rows
    xmap = lambda s: (s // nt, s % nt, 0)
    mmap = lambda s: (s // nt, 0, 0)
    omap = lambda s: (s, 0)

    def out(width, dtype):
        return jax.ShapeDtypeStruct((t, width), dtype), pl.BlockSpec((m, width), omap)

    def fox(dtype):
        if not fox_transposed:
            return out(FOX_WIDTH, dtype)
        return (jax.ShapeDtypeStruct((nb, FOX_WIDTH, r_all), dtype),
                pl.BlockSpec((1, FOX_WIDTH, m), lambda s: (s // nt, 0, s % nt)))

    outs = [out(HG_WIDTH, F32), out(HG_WIDTH, F32), out(HG_WIDTH, F32), out(HG_WIDTH, BF16),
            out(HG_WIDTH, BF16), fox(BF16), fox(F32), fox(F32),
            out(FOX_HEADS, F32), out(LANES, F32), out(wmg.shape[0], BF16)]
    return pl.pallas_call(
        functools.partial(_inproj_kernel, fox_transposed=fox_transposed),
        grid=(steps,),
        in_specs=[pl.BlockSpec((groups, rows, d), xmap),
                  pl.BlockSpec((groups, 1, mod3.shape[2]), mmap),
                  _resident(n1.shape), _resident(lbraw.shape), _resident(bf128.shape),
                  _resident(wa.shape), _resident(wff.shape), _resident(wmg.shape)],
        out_specs=[o[1] for o in outs],
        out_shape=[o[0] for o in outs],
        compiler_params=_params("arbitrary"),
        name="inproj",
    )(x3, mod3, n1, lbraw, bf128, wa, wff, wmg)


def _decay_kernel(lf_ref, place_ref, bk_ref, bqt_ref):
    seq = lf_ref.shape[1]
    lane = lax.broadcasted_iota(jnp.int32, (LANES, LANES), 1)
    slot = lane & (BIAS_SLOTS - 1)
    used = lane < FOX_HEADS * BIAS_SLOTS
    ones_k = jnp.where(jnp.logical_and(used, slot < N_PIECES), 1.0, 0.0)
    ones_q = jnp.where(jnp.logical_and(used, jnp.logical_and(slot >= N_PIECES, slot < 2 * N_PIECES)), 1.0, 0.0)
    carry = jnp.zeros((1, LANES), F32)
    for j in range(seq // LANES):
        sl = slice(j * LANES, (j + 1) * LANES)
        cs = _cumsum_rows(lf_ref[0, sl, :], LANES) + carry
        carry = cs[LANES - 1:LANES, :]
        pieces = _split3(cs * LOG2E)

        def place(base):
            return sum(jnp.dot(pieces[j], place_ref[base + j], preferred_element_type=F32)
                       for j in range(N_PIECES))

        bk_ref[0, sl, :] = (ones_k - place(N_PIECES)).astype(BF16)
        bqt_ref[0, :, sl] = (ones_q + place(0)).T.astype(BF16)


def _decay(lf3, place):
    b, seq, _ = lf3.shape
    return pl.pallas_call(
        _decay_kernel,
        grid=(b,),
        in_specs=[pl.BlockSpec((1, seq, LANES), lambda i: (i, 0, 0)), _resident(place.shape)],
        out_specs=[pl.BlockSpec((1, seq, LANES), lambda i: (i, 0, 0)),
                   pl.BlockSpec((1, LANES, seq), lambda i: (i, 0, 0))],
        out_shape=[jax.ShapeDtypeStruct((b, seq, LANES), BF16),
                   jax.ShapeDtypeStruct((b, LANES, seq), BF16)],
        compiler_params=_params("arbitrary"),
        name="decay",
    )(lf3, place)


def _hgrn_decays(q, k, g, chunk):
    rows = q.shape[0]
    n_chunks = rows // chunk
    b = _cumsum_rows(g, chunk)
    qt = (q * jnp.exp(b)).astype(BF16)
    kt = (k * jnp.exp(-b)).astype(BF16)
    b3 = b.reshape(n_chunks, chunk, HG_DK)
    bl3 = b3[:, chunk - 1:chunk, :]
    kd = (k.reshape(n_chunks, chunk, HG_DK) * jnp.exp(bl3 - b3)).reshape(rows, HG_DK).astype(BF16)
    return qt, kt, kd, jnp.exp(bl3)


def _hgrn_products(qt, kt, kd, v, chunk):
    rows = qt.shape[0]
    n_chunks = rows // chunk
    shift = chunk.bit_length() - 1
    a = lax.dot_general(qt, kt, _NT, preferred_element_type=F32)
    v_t = v.astype(F32).T.astype(BF16)
    col_chunk = lax.broadcasted_iota(jnp.int32, v_t.shape, 1) >> shift
    v_blocks = jnp.concatenate([jnp.where(col_chunk == c, v_t, jnp.zeros_like(v_t)) for c in range(n_chunks)],
                               axis=0)
    incr = jnp.dot(v_blocks, kd, preferred_element_type=F32)
    ri = lax.broadcasted_iota(jnp.int32, (rows, rows), 0)
    ci = lax.broadcasted_iota(jnp.int32, (rows, rows), 1)
    keep = (ri - ci).astype(jnp.uint32) <= (ri & (chunk - 1)).astype(jnp.uint32)
    o_intra = jnp.dot(jnp.where(keep, a, 0.0).astype(BF16), v, preferred_element_type=F32)
    return o_intra, incr


def _hgrn_blocks(blocks, chunk, state_in, state_out):
    n_chunks = LANES // chunk
    decays = [_hgrn_decays(q, k, g, chunk) for (q, k, g, _) in blocks]
    products = [_hgrn_products(qt, kt, kd, blk[3], chunk) for (qt, kt, kd, _), blk in zip(decays, blocks)]
    entering = []
    st = None
    for n in range(len(blocks) * n_chunks):
        j, c = divmod(n, n_chunks)
        st = state_in(n, st)
        entering.append(st.astype(BF16))
        st = st * decays[j][3][c] + products[j][1][c * HG_DV:(c + 1) * HG_DV, :]
        state_out(n, st)
    outs = []
    for j, ((qt, _, _, _), (o_intra, _)) in enumerate(zip(decays, products)):
        o_inter = [lax.dot_general(qt[c * chunk:(c + 1) * chunk, :], entering[j * n_chunks + c], _NT,
                                   preferred_element_type=F32) for c in range(n_chunks)]
        outs.append(o_intra + jnp.concatenate(o_inter, axis=0))
    return outs


HGRN_GROUP = 8


def _hgrn_prompt_kernel(q_ref, k_ref, g_ref, v_ref, o_ref, s_ref, *, chunk):
    seq = q_ref.shape[0]
    group_rows = HGRN_GROUP * LANES

    def body(i, st0):
        base = i * group_rows
        row_slices = [pl.ds(base + j * LANES, LANES) for j in range(HGRN_GROUP)]
        blocks = [(q_ref[rs, :], k_ref[rs, :], g_ref[rs, :], v_ref[rs, :]) for rs in row_slices]
        last = []
        outs = _hgrn_blocks(blocks, chunk, lambda n, prev: st0 if prev is None else prev,
                            lambda n, st: last.append(st))
        for rs, o in zip(row_slices, outs):
            o_ref[rs, :] = o
        return last[-1]

    st = jnp.zeros((HG_DV, HG_DK), F32)
    for i in range(seq // group_rows):
        st = body(i, st)
    s_ref[0, 0] = st.T


def _hgrn_prompt(q, k, g, v, batch, seq, chunk):
    spec = pl.BlockSpec((seq, HG_DK), lambda b, h: (b, h))
    return pl.pallas_call(
        functools.partial(_hgrn_prompt_kernel, chunk=chunk),
        grid=(batch, HG_HEADS),
        in_specs=[spec, spec, spec, spec],
        out_specs=[spec, pl.BlockSpec((1, 1, HG_DK, HG_DV), lambda b, h: (b, h, 0, 0))],
        out_shape=[jax.ShapeDtypeStruct((batch * seq, HG_WIDTH), F32),
                   jax.ShapeDtypeStruct((batch, HG_HEADS, HG_DK, HG_DV), F32)],
        compiler_params=_params("arbitrary", "arbitrary"),
        name="hgrn_prompt",
    )(q, k, g, v)


def _hgrn_sample_kernel(q_ref, k_ref, g_ref, v_ref, s0_ref, o_ref, s_ref, *, chunk):
    per = LANES // chunk

    def state_in(n, prev):
        h, c = divmod(n, per)
        return s0_ref[c, h].T

    def state_out(n, st):
        h, c = divmod(n, per)
        s_ref[c, h] = st.T

    heads = [slice(h * HG_DK, (h + 1) * HG_DK) for h in range(HG_HEADS)]
    outs = _hgrn_blocks([(q_ref[:, hs], k_ref[:, hs], g_ref[:, hs], v_ref[:, hs]) for hs in heads],
                        chunk, state_in, state_out)
    for hs, o in zip(heads, outs):
        o_ref[:, hs] = o


def _hgrn_sample(q, k, g, v, s0, chunk):
    t = q.shape[0]
    per = LANES // chunk
    spec = pl.BlockSpec((LANES, HG_WIDTH), lambda i: (i, 0))
    sspec = pl.BlockSpec((per, HG_HEADS, HG_DK, HG_DV), lambda i: (i, 0, 0, 0))
    return pl.pallas_call(
        functools.partial(_hgrn_sample_kernel, chunk=chunk),
        grid=(t // LANES,),
        in_specs=[spec, spec, spec, spec, sspec],
        out_specs=[spec, sspec],
        out_shape=[jax.ShapeDtypeStruct((t, HG_WIDTH), F32),
                   jax.ShapeDtypeStruct(s0.shape, F32)],
        compiler_params=_params("arbitrary"),
        name="hgrn_sample",
    )(q, k, g, v, s0)


ONES_ROWS = 16
PAGE_PREFETCH_DEPTH = 3


def _prompt_attention(uq_ref, uk_ref, um_ref, qt_ref, kt_ref, vt_ref, bk_ref, bqt_ref, o_ref,
                      kn_ref, va_ref, rhs_ref, mask_ref, s_ref, p_ref, *, pair, tq, tk, n_units, limits):
    seq = kt_ref.shape[1]
    n_heads = LANES // FOX_DH

    def prepare():
        for c in range(seq // LANES):
            cs = slice(c * LANES, (c + 1) * LANES)
            kn_ref[cs, 0:LANES] = kt_ref[:, cs].T.astype(BF16)
        kn_ref[:, LANES:2 * LANES] = bk_ref[...]
        for c in range(seq // tk):
            for e in range(n_heads):
                va_ref[c, e, 0:FOX_DH, :] = vt_ref[e * FOX_DH:(e + 1) * FOX_DH, c * tk:(c + 1) * tk].astype(BF16)
                va_ref[c, e, FOX_DH:FOX_DH + ONES_ROWS, :] = jnp.ones((ONES_ROWS, tk), BF16)
        row = lax.broadcasted_iota(jnp.int32, (LANES, tq), 0)
        for i in range(seq // tq):
            qs = slice(i * tq, (i + 1) * tq)
            qt = qt_ref[:, qs]
            bqt = bqt_ref[:, qs]
            for e in range(n_heads):
                rhs_ref[i, e, 0:LANES, :] = jnp.where(row // FOX_DH == e, qt, jnp.zeros_like(qt))
                rhs_ref[i, e, LANES:2 * LANES, :] = jnp.where(row // BIAS_SLOTS == pair * n_heads + e, bqt,
                                                              jnp.zeros_like(bqt))
        p_ref[1] = jnp.zeros(p_ref.shape[1:], BF16)
        key_minus_query = (lax.broadcasted_iota(jnp.int32, (tk, tq), 0)
                           - lax.broadcasted_iota(jnp.int32, (tk, tq), 1))
        for n, limit in enumerate(limits):
            mask_ref[n] = jnp.where(key_minus_query <= limit, 0.0, NEG_INF)
        rows = lambda n, v: [jnp.full((n, tq), v, F32) for _ in range(n_heads)]
        scores(1, 0)
        return rows(1, NEG_INF), rows(1, 1.0), rows(1, 1.0), rows(FOX_DH, 0.0)

    def scores(u, slot):
        lhs = kn_ref[pl.ds(pl.multiple_of(uk_ref[u] * tk, tk), tk), :]
        for e in range(n_heads):
            s_ref[slot, e] = jnp.dot(lhs, rhs_ref[uq_ref[u], e], preferred_element_type=F32)

    def numerators(u, slot, ms):
        first = uk_ref[u] == 0
        mask = mask_ref[um_ref[u]]
        new_ms, alphas = [], []
        for e in range(n_heads):
            m = jnp.where(first, NEG_INF, ms[e])
            s = s_ref[slot, e] + mask
            m_new = jnp.maximum(m, jnp.max(s, axis=0, keepdims=True))
            p_ref[slot, e] = jnp.exp2(s - m_new).astype(BF16)
            new_ms.append(m_new)
            alphas.append(jnp.exp2(m - m_new))
        return new_ms, alphas

    def values(u, slot, alphas, ls, accs):
        new_ls, new_accs = [], []
        for e in range(n_heads):
            pv = jnp.dot(va_ref[uk_ref[u], e], p_ref[slot, e], preferred_element_type=F32)
            new_ls.append(alphas[e] * ls[e] + pv[FOX_DH:FOX_DH + 1, :])
            new_accs.append(alphas[e] * accs[e] + pv[0:FOX_DH, :])
        o_t = jnp.concatenate([new_accs[e] / new_ls[e] for e in range(n_heads)], axis=0)
        o_ref[pl.ds(pl.multiple_of(uq_ref[u] * tq, tq), tq), :] = o_t.T.astype(BF16)
        return new_ls, new_accs

    def step(u, slot, carry):
        ms, alphas, ls, accs = carry
        new_ls, new_accs = values(u - 1, 1 - slot, alphas, ls, accs)
        scores(u + 1, 1 - slot)
        new_ms, new_alphas = numerators(u, slot, ms)
        return new_ms, new_alphas, new_ls, new_accs

    def body(k, carry):
        carry = step(2 * k + 1, 0, carry)
        return step(2 * k + 2, 1, carry)

    def run(lo, hi, state):
        for k in range(lo, hi):
            state = body(k, state)
        return state

    def finish(state):
        _, alphas, ls, accs = state
        values(n_units, 1, alphas, ls, accs)

    return prepare, run, finish


def _sample_attend(q, k_new, v_new, lf_new, sfx_ref, k_refs, v_refs, lf_refs):
    n_pages = len(k_refs)
    steps = q.shape[0]
    rows = FOX_HEADS * steps
    pad = PAGE_SIZE - steps

    def per_head_rows(x8):
        return jnp.broadcast_to(x8[:, None, :], (FOX_HEADS, steps, x8.shape[1])).reshape(rows, x8.shape[1])

    q = q.astype(F32)
    q_rows = jnp.concatenate([q] * FOX_HEADS, axis=0)
    row_head = lax.broadcasted_iota(jnp.int32, (rows, FOX_WIDTH), 0) // steps
    lane_head = lax.broadcasted_iota(jnp.int32, (rows, FOX_WIDTH), 1) // FOX_DH
    head_mask = row_head == lane_head
    qbd = jnp.where(head_mask, q_rows, 0.0).astype(BF16)

    pn = _cumsum_rows(lf_new, steps)
    pn_rows = jnp.concatenate([pn] * FOX_HEADS, axis=0)
    r_h = lax.broadcasted_iota(jnp.int32, (rows, LANES), 0) // steps
    r_i = lax.broadcasted_iota(jnp.int32, (rows, LANES), 0) % steps
    c_l = lax.broadcasted_iota(jnp.int32, (rows, LANES), 1)
    pcol = jnp.sum(jnp.where(c_l == r_h, pn_rows, 0.0), axis=1, keepdims=True)
    pn_t = jnp.concatenate([pn, jnp.zeros((pad, LANES), F32)], axis=0).T[0:FOX_HEADS, :]

    lf_all = jnp.concatenate([lf_refs[j][...] for j in range(n_pages)], axis=0)
    sfx = _dot_f32_lhs(lf_all, sfx_ref[...])
    carry = jnp.zeros((FOX_HEADS, PAGE_SIZE), F32)
    page_bias = [None] * n_pages
    for j in reversed(range(n_pages)):
        blk = sfx[j * FOX_HEADS:(j + 1) * FOX_HEADS, :]
        page_bias[j] = blk[:, 0:PAGE_SIZE] + carry
        carry = carry + blk[:, PAGE_SIZE:2 * PAGE_SIZE]

    s_tiles = []
    for j in range(n_pages):
        s = jnp.dot(qbd, k_refs[j][...].astype(BF16), preferred_element_type=F32)
        s_tiles.append(s + (per_head_rows(page_bias[j]) + pcol))
    kn = jnp.concatenate([k_new, jnp.zeros((pad, FOX_WIDTH), F32)], axis=0).astype(BF16)
    s_new = lax.dot_general(qbd, kn, _NT, preferred_element_type=F32) + (pcol - per_head_rows(pn_t))
    s_tiles.append(jnp.where(c_l <= r_i, s_new, NEG_INF))

    m_el = s_tiles[0]
    for s in s_tiles[1:]:
        m_el = jnp.maximum(m_el, s)
    m = jnp.max(m_el, axis=1, keepdims=True)
    p_tiles = [jnp.exp(s - m) for s in s_tiles]
    l_el = p_tiles[0]
    for p in p_tiles[1:]:
        l_el = l_el + p
    l = jnp.sum(l_el, axis=1, keepdims=True)

    vn = jnp.concatenate([v_new, jnp.zeros((pad, FOX_WIDTH), F32)], axis=0).astype(BF16)
    o = jnp.dot(p_tiles[n_pages].astype(BF16), vn, preferred_element_type=F32)
    for j in range(n_pages):
        o = o + lax.dot_general(p_tiles[j].astype(BF16), v_refs[j][...].astype(BF16), _NT,
                                preferred_element_type=F32)
    o = jnp.where(head_mask, o / l, 0.0)
    return jnp.sum(o.reshape(FOX_HEADS, steps, FOX_WIDTH), axis=0)


def _fox_kernel(uq_ref, uk_ref, um_ref, pt_ref,
                qt_ref, kt_ref, vt_ref, bk_ref, bqt_ref,
                sq_ref, sk_ref, sv_ref, slf_ref, sfx_ref, ck_hbm, cv_hbm, cl_hbm,
                o_ref, so_ref,
                kn_ref, va_ref, rhs_ref, mask_ref, s_ref, p_ref, k_buf, v_buf, lf_buf, sems,
                *, tq, tk, n_units, limits, n_pages):
    step = pl.program_id(0) * pl.num_programs(1) + pl.program_id(1)
    n_steps = pl.num_programs(0) * pl.num_programs(1)
    per_step = sq_ref.shape[0]
    depth = PAGE_PREFETCH_DEPTH
    assert 0 < depth < per_step
    last_seq = n_steps * per_step - 1

    def page_copies(seq, slot):
        copies = []
        for j in range(n_pages):
            page = pt_ref[seq * n_pages + j]
            copies.append(pltpu.make_async_copy(ck_hbm.at[page], k_buf.at[slot, j], sems.at[0, slot]))
            copies.append(pltpu.make_async_copy(cv_hbm.at[page], v_buf.at[slot, j], sems.at[0, slot]))
            copies.append(pltpu.make_async_copy(cl_hbm.at[page], lf_buf.at[slot, j], sems.at[1, slot]))
        return copies

    @pl.when(step == 0)
    def _():
        for d in range(depth):
            for c in page_copies(d, d):
                c.start()

    prepare, run, finish = _prompt_attention(
        uq_ref, uk_ref, um_ref, qt_ref, kt_ref, vt_ref, bk_ref, bqt_ref, o_ref,
        kn_ref, va_ref, rhs_ref, mask_ref, s_ref, p_ref,
        pair=pl.program_id(1), tq=tq, tk=tk, n_units=n_units, limits=limits)
    state = prepare()
    n_pairs = n_units // 2
    bounds = [(j * n_pairs) // per_step for j in range(per_step + 1)]
    for j in range(per_step):
        seq = step * per_step + j
        slot = j
        ahead = jnp.minimum(seq + depth, last_seq)
        for c in page_copies(ahead, (j + depth) % per_step):
            c.start()
        for c in page_copies(seq, slot):
            c.wait()
        so_ref[j] = _sample_attend(
            sq_ref[j], sk_ref[j], sv_ref[j], slf_ref[j], sfx_ref,
            [k_buf.at[slot, n] for n in range(n_pages)], [v_buf.at[slot, n] for n in range(n_pages)],
            [lf_buf.at[slot, n] for n in range(n_pages)]).astype(BF16)
        state = run(bounds[j], bounds[j + 1], state)
    finish(state)

    @pl.when(step == n_steps - 1)
    def _():
        for d in range(depth):
            for c in page_copies(last_seq, d):
                c.wait()


def _fox(fqt, fkt, fvt, bk, bqt, page_table, sq3, sk3, sv3, slf3, sfx_mat, cache_kt, cache_vt, cache_lft,
         tq, tk):
    batch, _, seq = fqt.shape
    nseq, n_pages = page_table.shape
    steps = sq3.shape[1]
    assert tk % tq == 0 and seq % tk == 0
    nq = seq // tq
    pairs = FOX_WIDTH // LANES
    n_heads = LANES // FOX_DH
    per_step = nseq // (batch * pairs)
    assert per_step * batch * pairs == nseq and per_step > PAGE_PREFETCH_DEPTH
    units = [(i, j) for i in range(nq) for j in range((i * tq) // tk + 1)]
    n_units = len(units)
    assert n_units % 2 == 0
    padded = [units[0]] + units + [units[-1]]
    unit_limits = [min(i * tq - j * tk, tk - 1) for i, j in padded]
    limits = tuple(sorted(set(unit_limits)))
    table = lambda vals: jnp.asarray(vals, jnp.int32)
    uq, uk = table([u[0] for u in padded]), table([u[1] for u in padded])
    um = table([limits.index(v) for v in unit_limits])
    head_pair = lambda b, p, *_: (b, p, 0)
    per_batch = lambda b, p, *_: (b, 0, 0)
    sample = lambda w: pl.BlockSpec((per_step, steps, w), lambda b, p, *_: (b * pairs + p, 0, 0))
    in_hbm = pl.BlockSpec(memory_space=pl.ANY)
    grid_spec = pltpu.PrefetchScalarGridSpec(
        num_scalar_prefetch=4,
        grid=(batch, pairs),
        in_specs=[pl.BlockSpec((None, LANES, seq), head_pair),
                  pl.BlockSpec((None, LANES, seq), head_pair),
                  pl.BlockSpec((None, LANES, seq), head_pair),
                  pl.BlockSpec((None, seq, LANES), per_batch),
                  pl.BlockSpec((None, LANES, seq), per_batch),
                  sample(FOX_WIDTH), sample(FOX_WIDTH), sample(FOX_WIDTH), sample(LANES),
                  pl.BlockSpec(sfx_mat.shape, lambda b, p, *_: (0, 0)), in_hbm, in_hbm, in_hbm],
        out_specs=[pl.BlockSpec((seq, LANES), lambda b, p, *_: (b, p)), sample(FOX_WIDTH)],
        scratch_shapes=[pltpu.VMEM((seq, 2 * LANES), BF16),
                        pltpu.VMEM((seq // tk, n_heads, FOX_DH + ONES_ROWS, tk), BF16),
                        pltpu.VMEM((nq, n_heads, 2 * LANES, tq), BF16),
                        pltpu.VMEM((len(limits), tk, tq), F32),
                        pltpu.VMEM((2, n_heads, tk, tq), F32),
                        pltpu.VMEM((2, n_heads, tk, tq), BF16),
                        pltpu.VMEM((per_step, n_pages, FOX_WIDTH, PAGE_SIZE), cache_kt.dtype),
                        pltpu.VMEM((per_step, n_pages, FOX_WIDTH, PAGE_SIZE), cache_vt.dtype),
                        pltpu.VMEM((per_step, n_pages, FOX_HEADS, PAGE_SIZE), cache_lft.dtype),
                        pltpu.SemaphoreType.DMA((2, per_step))],
    )
    return pl.pallas_call(
        functools.partial(_fox_kernel, tq=tq, tk=tk, n_units=n_units, limits=limits, n_pages=n_pages),
        grid_spec=grid_spec,
        out_shape=[jax.ShapeDtypeStruct((batch * seq, FOX_WIDTH), BF16),
                   jax.ShapeDtypeStruct((nseq, steps, FOX_WIDTH), BF16)],
        compiler_params=_params("arbitrary", "arbitrary"),
        name="fox",
    )(uq, uk, um, page_table.reshape(-1), fqt, fkt, fvt, bk, bqt, sq3, sk3, sv3, slf3, sfx_mat,
      cache_kt, cache_vt, cache_lft)


def _post_kernel(x_ref, mod_ref, oh_ref, gate_ref, of_ref, mg_ref, onorm_ref, n1post_ref, n2pre_ref,
                 n2post_ref, wbh_ref, wbf_ref, wout_ref, wup_ref, wdn_ref, y_ref, *, ff_chunk):
    g_, r_, d = x_ref.shape
    m = g_ * r_
    mod = mod_ref[...]
    gt1 = mod[:, :, 2 * d:3 * d]
    sh2 = mod[:, :, 3 * d:4 * d]
    sc2 = mod[:, :, 4 * d:5 * d]
    gt2 = mod[:, :, 5 * d:6 * d]

    oh = oh_ref[...]
    parts = [_rms(oh[:, h * HG_DV:(h + 1) * HG_DV], onorm_ref[...]) for h in range(HG_HEADS)]
    ohn = (jnp.concatenate(parts, axis=1) * gate_ref[...].astype(F32)).astype(BF16)
    br_h = jnp.dot(ohn, wbh_ref[...], preferred_element_type=F32)
    br_f = jnp.dot(of_ref[...], wbf_ref[...], preferred_element_type=F32)
    mg = mg_ref[...].astype(F32)
    z = (mg[:, 0:d] * br_h + mg[:, d:2 * d] * br_f).astype(BF16)
    y = jnp.dot(z, wout_ref[...], preferred_element_type=F32).reshape(g_, r_, d)
    x1 = x_ref[...] + _rms(y, gt1 * n1post_ref[...])
    h2 = (_rms(x1, n2pre_ref[...] * (1.0 + sc2)) + sh2).reshape(m, d).astype(BF16)
    u = jnp.zeros((m, d), F32)
    for c in range(wup_ref.shape[1] // ff_chunk):
        cs = slice(c * ff_chunk, (c + 1) * ff_chunk)
        a = jnp.maximum(jnp.dot(h2, wup_ref[:, cs], preferred_element_type=F32), 0.0)
        u = u + jnp.dot((a * a).astype(BF16), wdn_ref[cs, :], preferred_element_type=F32)
    y_ref[...] = x1 + _rms(u.reshape(g_, r_, d), gt2 * n2post_ref[...])


def _post(x3, mod3, oh, gate, of, mg, onorm, n1post, n2pre, n2post, wbh, wbf, wout, wup, wdn,
          groups, rows, ff_chunk=2048):
    nb, r_all, d = x3.shape
    nt = r_all // rows
    steps = (nb // groups) * nt
    m = groups * rows
    xmap = lambda s: (s // nt, s % nt, 0)
    mmap = lambda s: (s // nt, 0, 0)
    tmap = lambda s: (s, 0)
    tok = lambda a: pl.BlockSpec((m, a.shape[1]), tmap)
    consts = [onorm, n1post, n2pre, n2post, wbh, wbf, wout, wup, wdn]
    return pl.pallas_call(
        functools.partial(_post_kernel, ff_chunk=ff_chunk),
        grid=(steps,),
        in_specs=[pl.BlockSpec((groups, rows, d), xmap),
                  pl.BlockSpec((groups, 1, mod3.shape[2]), mmap),
                  tok(oh), tok(gate), tok(of), tok(mg)] + [_resident(c.shape) for c in consts],
        out_specs=pl.BlockSpec((groups, rows, d), xmap),
        out_shape=jax.ShapeDtypeStruct(x3.shape, F32),
        compiler_params=_params("arbitrary"),
        name="post",
    )(x3, mod3, oh, gate, of, mg, *consts)


def _bias_place_matrices():
    shape = (2 * N_PIECES, LANES, LANES)
    s = lax.broadcasted_iota(jnp.int32, shape, 0)
    r = lax.broadcasted_iota(jnp.int32, shape, 1)
    c = lax.broadcasted_iota(jnp.int32, shape, 2)
    return jnp.logical_and(r < FOX_HEADS, c == BIAS_SLOTS * r + s).astype(BF16)


def _page_suffix_matrix():
    r = lax.broadcasted_iota(jnp.int32, (PAGE_SIZE, 2 * PAGE_SIZE), 0)
    c = lax.broadcasted_iota(jnp.int32, (PAGE_SIZE, 2 * PAGE_SIZE), 1)
    return jnp.logical_or(r > c, c >= PAGE_SIZE).astype(BF16)


def kernel(x_prompt, x_sample, c_prompt, c_sample, cache_k, cache_v, cache_logf, state_hgrn, page_table,
           ada_w, ada_b, norm_mix_pre, norm_mix_post, norm_mlp_pre, norm_mlp_post, w_in,
           hgrn_lower_bounds, hgrn_onorm, fox_b_f, w_br_h, w_br_f, w_out, w_mlp_up, w_mlp_down):
    batch, seq, d = x_prompt.shape
    nseq, steps, _ = x_sample.shape
    layer = 0
    n_phys = cache_k.shape[1]

    w_t = jnp.transpose(w_in[layer]).astype(BF16)
    n_a = 4 * HG_WIDTH + 3 * FOX_WIDTH
    wa = w_t[:n_a]
    wff = jnp.pad(w_t[n_a:n_a + FOX_HEADS], ((0, LANES - FOX_HEADS), (0, 0)))
    wmg = w_t[n_a + FOX_HEADS:]
    bf128 = jnp.pad(fox_b_f[layer], (0, LANES - FOX_HEADS)).reshape(1, LANES)
    lbraw = hgrn_lower_bounds
    vec3 = lambda v: v.reshape(1, 1, -1)
    wbh, wbf = w_br_h[layer].astype(BF16), w_br_f[layer].astype(BF16)
    wout = w_out[layer].astype(BF16)
    wup, wdn = w_mlp_up[layer].astype(BF16), w_mlp_down[layer].astype(BF16)
    onorm = hgrn_onorm[layer].reshape(1, HG_DV)

    n_c = batch + nseq
    c_pad = -n_c % (2 * SUBLANES)
    c_all = jnp.concatenate([c_prompt, c_sample, jnp.zeros((c_pad, d), F32)], axis=0)
    mod = _ada(c_all, ada_w[layer], ada_b[layer].reshape(1, -1))
    mod_p = mod[:batch].reshape(batch, 1, -1)
    mod_s = mod[batch:n_c].reshape(nseq, 1, -1)

    tm = TOKEN_TILE
    sgroups = tm // steps
    proj_args = (vec3(norm_mix_pre[layer]), lbraw, bf128, wa, wff, wmg)
    (q_p, g_p, k_p, v_p, gate_p, fq_p, fk_p, fv_p, lf8_p, lf128_p, mg_p) = _inproj(
        x_prompt, mod_p, *proj_args, groups=1, rows=tm, fox_transposed=True)
    (q_s, g_s, k_s, v_s, gate_s, fq_s, fk_s, fv_s, lf8_s, lf128_s, mg_s) = _inproj(
        x_sample, mod_s, *proj_args, groups=sgroups, rows=steps, fox_transposed=False)

    chunk_p = HG_CHUNK if seq % HG_CHUNK == 0 else seq
    oh_p, s_p = _hgrn_prompt(q_p, k_p, g_p, v_p, batch, seq, chunk_p)
    oh_s, s_s = _hgrn_sample(q_s, k_s, g_s, v_s, state_hgrn[layer], steps)

    tq, tk = FOX_Q_TILE, FOX_K_TILE
    bias_k, bias_qt = _decay(lf128_p.reshape(batch, seq, LANES), _bias_place_matrices())
    ckt = jnp.transpose(cache_k[layer], (0, 2, 3, 1)).reshape(n_phys, FOX_WIDTH, PAGE_SIZE)
    cvt = jnp.transpose(cache_v[layer], (0, 2, 3, 1)).reshape(n_phys, FOX_WIDTH, PAGE_SIZE)
    clt = jnp.transpose(cache_logf[layer], (0, 2, 1))
    per_seq = lambda a: a.reshape(nseq, steps, a.shape[-1])
    of_p, of_s = _fox(fq_p, fk_p, fv_p, bias_k, bias_qt, page_table, per_seq(fq_s), per_seq(fk_s), per_seq(fv_s),
                      per_seq(lf128_s), _page_suffix_matrix(), ckt, cvt, clt, tq, tk)
    of_s = of_s.reshape(nseq * steps, FOX_WIDTH)

    post_args = (onorm, vec3(norm_mix_post[layer]), vec3(norm_mlp_pre[layer]), vec3(norm_mlp_post[layer]),
                 wbh, wbf, wout, wup, wdn)
    y_p = _post(x_prompt, mod_p, oh_p, gate_p, of_p, mg_p, *post_args, groups=1, rows=tm)
    y_s = _post(x_sample, mod_s, oh_s, gate_s, of_s, mg_s, *post_args, groups=sgroups, rows=steps)

    k_prompt = fk_p.reshape(1, batch, FOX_HEADS, FOX_DH, seq).transpose(0, 1, 4, 2, 3)
    v_prompt = fv_p.reshape(1, batch, FOX_HEADS, FOX_DH, seq).transpose(0, 1, 4, 2, 3)
    logf_prompt = lf8_p.reshape(1, batch, seq, FOX_HEADS)
    k_sample = fk_s.reshape(1, nseq, steps, FOX_HEADS, FOX_DH)
    v_sample = fv_s.reshape(1, nseq, steps, FOX_HEADS, FOX_DH)
    logf_sample = lf8_s.reshape(1, nseq, steps, FOX_HEADS)
    return (y_p, y_s, k_prompt, v_prompt, logf_prompt, s_p[None], k_sample, v_sample, logf_sample, s_s[None])
```

```python
import functools

import jax
import jax.numpy as jnp
from jax import lax
from jax.experimental import pallas as pl
from jax.experimental.pallas import tpu as pltpu

F32 = jnp.float32
BF16 = jnp.bfloat16

LANES = 128
SUBLANES = 8
VMEM_LIMIT_BYTES = 56 * 1024 * 1024
TOKEN_TILE = 512
FOX_Q_TILE = 256
FOX_K_TILE = 512

HG_HEADS = 4
HG_DK = 128
HG_DV = 128
HG_CHUNK = 32
FOX_HEADS = 8
FOX_DH = 64
FOX_WIDTH = FOX_HEADS * FOX_DH
HG_WIDTH = HG_HEADS * HG_DV
PAGE_SIZE = 128
RMS_EPS = 1e-6
NEG_INF = float("-inf")
LOG2E = 1.4426950408889634
N_PIECES = 3
BIAS_SLOTS = 8

_NT = (((1,), (1,)), ((), ()))


def _params(*sem):
    return pltpu.CompilerParams(dimension_semantics=sem, vmem_limit_bytes=VMEM_LIMIT_BYTES)


def _resident(shape):
    nd = len(shape)
    return pl.BlockSpec(shape, lambda *_: (0,) * nd, pipeline_mode=pl.Buffered(1))


def _sigmoid(x):
    return 0.5 * jnp.tanh(0.5 * x) + 0.5


def _split3(x):
    hi = x.astype(BF16)
    r1 = x - hi.astype(F32)
    mid = r1.astype(BF16)
    lo = (r1 - mid.astype(F32)).astype(BF16)
    return hi, mid, lo


def _dot_f32_lhs(x, w):
    hi, mid, lo = _split3(x)
    d = lambda p: jnp.dot(p, w, preferred_element_type=F32)
    return (d(lo) + d(mid)) + d(hi)


def _cumsum_rows(x, period):
    row = lax.broadcasted_iota(jnp.int32, x.shape, 0) & (period - 1)
    s = 1
    while s < period:
        x = x + jnp.where(row >= s, pltpu.roll(x, s, axis=0), 0.0)
        s *= 2
    return x


def _rms(x, w):
    return x * lax.rsqrt(jnp.mean(x * x, axis=-1, keepdims=True) + RMS_EPS) * w


def _ada_kernel(c_ref, w_ref, b_ref, o_ref):
    c = c_ref[...]
    a = (c * _sigmoid(c)).astype(BF16)
    o_ref[...] = jnp.dot(a, w_ref[...].astype(BF16), preferred_element_type=F32) + b_ref[...]


def _ada(c, w, b, tn=1536):
    m, d = c.shape
    n = w.shape[1]
    return pl.pallas_call(
        _ada_kernel,
        grid=(n // tn,),
        in_specs=[pl.BlockSpec((m, d), lambda j: (0, 0)),
                  pl.BlockSpec((d, tn), lambda j: (0, j)),
                  pl.BlockSpec((1, tn), lambda j: (0, j))],
        out_specs=pl.BlockSpec((m, tn), lambda j: (0, j)),
        out_shape=jax.ShapeDtypeStruct((m, n), F32),
        compiler_params=_params("arbitrary"),
        name="ada",
    )(c, w, b)


def _inproj_kernel(x_ref, mod_ref, n1_ref, lbraw_ref, bf_ref, wa_ref, wff_ref, wmg_ref,
                   q_ref, g_ref, k_ref, v_ref, gate_ref, fq_ref, fk_ref, fv_ref,
                   lf8_ref, lf128_ref, mg_ref, *, fox_transposed):
    g_, r_, d = x_ref.shape
    m = g_ * r_
    x = x_ref[...]
    mod = mod_ref[...]
    sh1 = mod[:, :, 0:d]
    sc1 = mod[:, :, d:2 * d]
    h = _rms(x, n1_ref[...] * (1.0 + sc1)) + sh1
    hb = h.reshape(m, d).astype(BF16)

    raw = lbraw_ref[...]
    e = jnp.exp(raw - jnp.max(raw, axis=0, keepdims=True))
    lb = e[0:1, :] / jnp.sum(e, axis=0, keepdims=True)

    w = HG_WIDTH

    def proj(c):
        return lax.dot_general(hb, wa_ref[c * w:(c + 1) * w, :], _NT, preferred_element_type=F32)

    def proj_t(c):
        return lax.dot_general(wa_ref[c * w:(c + 1) * w, :], hb, _NT, preferred_element_type=F32)

    def merge_gate(c):
        mgc = lax.dot_general(hb, wmg_ref[c * w:(c + 1) * w, :], _NT, preferred_element_type=F32)
        mg_ref[:, c * w:(c + 1) * w] = _sigmoid(mgc).astype(BF16)

    def fox(c, ref, scale=None):
        y = proj_t(c) if fox_transposed else proj(c)
        y = y if scale is None else y * scale
        if fox_transposed:
            ref[0] = y.astype(ref.dtype)
        else:
            ref[...] = y.astype(ref.dtype)

    assert wmg_ref.shape[0] == 4 * w
    merge_gate(0)
    q_ref[...] = proj(0) * (HG_DK ** -0.5)
    merge_gate(1)
    v_ref[...] = proj(2).astype(BF16)
    merge_gate(2)
    fox(4, fq_ref, (LOG2E if fox_transposed else 1.0) * FOX_DH ** -0.5)
    merge_gate(3)
    fox(5, fk_ref)
    half_t = 0.5 * jnp.tanh(0.5 * proj(1))
    g_ref[...] = jnp.log(lb + (1.0 - lb) * (0.5 + half_t))
    k_ref[...] = (1.0 - lb) * (0.5 - half_t)
    fox(6, fv_ref)
    hg = proj(3)
    gate_ref[...] = (hg * _sigmoid(hg)).astype(BF16)
    z = lax.dot_general(hb, wff_ref[...], _NT, preferred_element_type=F32) + bf_ref[...]
    lf = jnp.minimum(z, 0.0) - jnp.log1p(jnp.exp(-jnp.abs(z)))
    lf128_ref[...] = lf
    lf8_ref[...] = lf[:, 0:FOX_HEADS]


def _inproj(x3, mod3, n1, lbraw, bf128, wa, wff, wmg, groups, rows, fox_transposed):
    assert groups == 1 or not fox_transposed
    nb, r_all, d = x3.shape
    nt = r_all // rows
    steps = (nb // groups) * nt
    t = nb * r_all
    m = groups * rows
    xmap = lambda s: (s // nt, s % nt, 0)
    mmap = lambda s: (s // nt, 0, 0)
    omap = lambda s: (s, 0)

    def out(width, dtype):
        return jax.ShapeDtypeStruct((t, width), dtype), pl.BlockSpec((m, width), omap)

    def fox(dtype):
        if not fox_transposed:
            return out(FOX_WIDTH, dtype)
        return (jax.ShapeDtypeStruct((nb, FOX_WIDTH, r_all), dtype),
                pl.BlockSpec((1, FOX_WIDTH, m), lambda s: (s // nt, 0, s % nt)))

    outs = [out(HG_WIDTH, F32), out(HG_WIDTH, F32), out(HG_WIDTH, F32), out(HG_WIDTH, BF16),
            out(HG_WIDTH, BF16), fox(BF16), fox(F32), fox(F32),
            out(FOX_HEADS, F32), out(LANES, F32), out(wmg.shape[0], BF16)]
    return pl.pallas_call(
        functools.partial(_inproj_kernel, fox_transposed=fox_transposed),
        grid=(steps,),
        in_specs=[pl.BlockSpec((groups, rows, d), xmap),
                  pl.BlockSpec((groups, 1, mod3.shape[2]), mmap),
                  _resident(n1.shape), _resident(lbraw.shape), _resident(bf128.shape),
                  _resident(wa.shape), _resident(wff.shape), _resident(wmg.shape)],
        out_specs=[o[1] for o in outs],
        out_shape=[o[0] for o in outs],
        compiler_params=_params("arbitrary"),
        name="inproj",
    )(x3, mod3, n1, lbraw, bf128, wa, wff, wmg)


def _decay_kernel(lf_ref, place_ref, bk_ref, bqt_ref):
    seq = lf_ref.shape[1]
    lane = lax.broadcasted_iota(jnp.int32, (LANES, LANES), 1)
    slot = lane & (BIAS_SLOTS - 1)
    used = lane < FOX_HEADS * BIAS_SLOTS
    ones_k = jnp.where(jnp.logical_and(used, slot < N_PIECES), 1.0, 0.0)
    ones_q = jnp.where(jnp.logical_and(used, jnp.logical_and(slot >= N_PIECES, slot < 2 * N_PIECES)), 1.0, 0.0)
    carry = jnp.zeros((1, LANES), F32)
    for j in range(seq // LANES):
        sl = slice(j * LANES, (j + 1) * LANES)
        cs = _cumsum_rows(lf_ref[0, sl, :], LANES) + carry
        carry = cs[LANES - 1:LANES, :]
        pieces = _split3(cs * LOG2E)

        def place(base):
            return sum(jnp.dot(pieces[j], place_ref[base + j], preferred_element_type=F32)
                       for j in range(N_PIECES))

        bk_ref[0, sl, :] = (ones_k - place(N_PIECES)).astype(BF16)
        bqt_ref[0, :, sl] = (ones_q + place(0)).T.astype(BF16)


def _decay(lf3, place):
    b, seq, _ = lf3.shape
    return pl.pallas_call(
        _decay_kernel,
        grid=(b,),
        in_specs=[pl.BlockSpec((1, seq, LANES), lambda i: (i, 0, 0)), _resident(place.shape)],
        out_specs=[pl.BlockSpec((1, seq, LANES), lambda i: (i, 0, 0)),
                   pl.BlockSpec((1, LANES, seq), lambda i: (i, 0, 0))],
        out_shape=[jax.ShapeDtypeStruct((b, seq, LANES), BF16),
                   jax.ShapeDtypeStruct((b, LANES, seq), BF16)],
        compiler_params=_params("arbitrary"),
        name="decay",
    )(lf3, place)


def _hgrn_decays(q, k, g, chunk):
    rows = q.shape[0]
    n_chunks = rows // chunk
    b = _cumsum_rows(g, chunk)
    qt = (q * jnp.exp(b)).astype(BF16)
    kt = (k * jnp.exp(-b)).astype(BF16)
    b3 = b.reshape(n_chunks, chunk, HG_DK)
    bl3 = b3[:, chunk - 1:chunk, :]
    kd = (k.reshape(n_chunks, chunk, HG_DK) * jnp.exp(bl3 - b3)).reshape(rows, HG_DK).astype(BF16)
    return qt, kt, kd, jnp.exp(bl3)


def _hgrn_products(qt, kt, kd, v, chunk):
    rows = qt.shape[0]
    n_chunks = rows // chunk
    shift = chunk.bit_length() - 1
    a = lax.dot_general(qt, kt, _NT, preferred_element_type=F32)
    v_t = v.astype(F32).T.astype(BF16)
    col_chunk = lax.broadcasted_iota(jnp.int32, v_t.shape, 1) >> shift
    v_blocks = jnp.concatenate([jnp.where(col_chunk == c, v_t, jnp.zeros_like(v_t)) for c in range(n_chunks)],
                               axis=0)
    incr = jnp.dot(v_blocks, kd, preferred_element_type=F32)
    ri = lax.broadcasted_iota(jnp.int32, (rows, rows), 0)
    ci = lax.broadcasted_iota(jnp.int32, (rows, rows), 1)
    keep = (ri - ci).astype(jnp.uint32) <= (ri & (chunk - 1)).astype(jnp.uint32)
    o_intra = jnp.dot(jnp.where(keep, a, 0.0).astype(BF16), v, preferred_element_type=F32)
    return o_intra, incr


def _hgrn_blocks(blocks, chunk, state_in, state_out):
    n_chunks = LANES // chunk
    decays = [_hgrn_decays(q, k, g, chunk) for (q, k, g, _) in blocks]
    products = [_hgrn_products(qt, kt, kd, blk[3], chunk) for (qt, kt, kd, _), blk in zip(decays, blocks)]
    entering = []
    st = None
    for n in range(len(blocks) * n_chunks):
        j, c = divmod(n, n_chunks)
        st = state_in(n, st)
        entering.append(st.astype(BF16))
        st = st * decays[j][3][c] + products[j][1][c * HG_DV:(c + 1) * HG_DV, :]
        state_out(n, st)
    outs = []
    for j, ((qt, _, _, _), (o_intra, _)) in enumerate(zip(decays, products)):
        o_inter = [lax.dot_general(qt[c * chunk:(c + 1) * chunk, :], entering[j * n_chunks + c], _NT,
                                   preferred_element_type=F32) for c in range(n_chunks)]
        outs.append(o_intra + jnp.concatenate(o_inter, axis=0))
    return outs


HGRN_GROUP = 8


def _hgrn_prompt_kernel(q_ref, k_ref, g_ref, v_ref, o_ref, s_ref, *, chunk):
    seq = q_ref.shape[0]
    group_rows = HGRN_GROUP * LANES

    def body(i, st0):
        base = i * group_rows
        row_slices = [pl.ds(base + j * LANES, LANES) for j in range(HGRN_GROUP)]
        blocks = [(q_ref[rs, :], k_ref[rs, :], g_ref[rs, :], v_ref[rs, :]) for rs in row_slices]
        last = []
        outs = _hgrn_blocks(blocks, chunk, lambda n, prev: st0 if prev is None else prev,
                            lambda n, st: last.append(st))
        for rs, o in zip(row_slices, outs):
            o_ref[rs, :] = o
        return last[-1]

    st = jnp.zeros((HG_DV, HG_DK), F32)
    for i in range(seq // group_rows):
        st = body(i, st)
    s_ref[0, 0] = st.T


def _hgrn_prompt(q, k, g, v, batch, seq, chunk):
    spec = pl.BlockSpec((seq, HG_DK), lambda b, h: (b, h))
    return pl.pallas_call(
        functools.partial(_hgrn_prompt_kernel, chunk=chunk),
        grid=(batch, HG_HEADS),
        in_specs=[spec, spec, spec, spec],
        out_specs=[spec, pl.BlockSpec((1, 1, HG_DK, HG_DV), lambda b, h: (b, h, 0, 0))],
        out_shape=[jax.ShapeDtypeStruct((batch * seq, HG_WIDTH), F32),
                   jax.ShapeDtypeStruct((batch, HG_HEADS, HG_DK, HG_DV), F32)],
        compiler_params=_params("arbitrary", "arbitrary"),
        name="hgrn_prompt",
    )(q, k, g, v)


def _hgrn_sample_kernel(q_ref, k_ref, g_ref, v_ref, s0_ref, o_ref, s_ref, *, chunk):
    per = LANES // chunk

    def state_in(n, prev):
        h, c = divmod(n, per)
        return s0_ref[c, h].T

    def state_out(n, st):
        h, c = divmod(n, per)
        s_ref[c, h] = st.T

    heads = [slice(h * HG_DK, (h + 1) * HG_DK) for h in range(HG_HEADS)]
    outs = _hgrn_blocks([(q_ref[:, hs], k_ref[:, hs], g_ref[:, hs], v_ref[:, hs]) for hs in heads],
                        chunk, state_in, state_out)
    for hs, o in zip(heads, outs):
        o_ref[:, hs] = o


def _hgrn_sample(q, k, g, v, s0, chunk):
    t = q.shape[0]
    per = LANES // chunk
    spec = pl.BlockSpec((LANES, HG_WIDTH), lambda i: (i, 0))
    sspec = pl.BlockSpec((per, HG_HEADS, HG_DK, HG_DV), lambda i: (i, 0, 0, 0))
    return pl.pallas_call(
        functools.partial(_hgrn_sample_kernel, chunk=chunk),
        grid=(t // LANES,),
        in_specs=[spec, spec, spec, spec, sspec],
        out_specs=[spec, sspec],
        out_shape=[jax.ShapeDtypeStruct((t, HG_WIDTH), F32),
                   jax.ShapeDtypeStruct(s0.shape, F32)],
        compiler_params=_params("arbitrary"),
        name="hgrn_sample",
    )(q, k, g, v, s0)


ONES_ROWS = 16
PAGE_PREFETCH_DEPTH = 3


def _prompt_attention(uq_ref, uk_ref, um_ref, qt_ref, kt_ref, vt_ref, bk_ref, bqt_ref, o_ref,
                      kn_ref, va_ref, rhs_ref, mask_ref, s_ref, p_ref, *, pair, tq, tk, n_units, limits):
    seq = kt_ref.shape[1]
    n_heads = LANES // FOX_DH

    def prepare():
        for c in range(seq // LANES):
            cs = slice(c * LANES, (c + 1) * LANES)
            kn_ref[cs, 0:LANES] = kt_ref[:, cs].T.astype(BF16)
        kn_ref[:, LANES:2 * LANES] = bk_ref[...]
        for c in range(seq // tk):
            for e in range(n_heads):
                va_ref[c, e, 0:FOX_DH, :] = vt_ref[e * FOX_DH:(e + 1) * FOX_DH, c * tk:(c + 1) * tk].astype(BF16)
                va_ref[c, e, FOX_DH:FOX_DH + ONES_ROWS, :] = jnp.ones((ONES_ROWS, tk), BF16)
        row = lax.broadcasted_iota(jnp.int32, (LANES, tq), 0)
        for i in range(seq // tq):
            qs = slice(i * tq, (i + 1) * tq)
            qt = qt_ref[:, qs]
            bqt = bqt_ref[:, qs]
            for e in range(n_heads):
                rhs_ref[i, e, 0:LANES, :] = jnp.where(row // FOX_DH == e, qt, jnp.zeros_like(qt))
                rhs_ref[i, e, LANES:2 * LANES, :] = jnp.where(row // BIAS_SLOTS == pair * n_heads + e, bqt,
                                                              jnp.zeros_like(bqt))
        p_ref[1] = jnp.zeros(p_ref.shape[1:], BF16)
        key_minus_query = (lax.broadcasted_iota(jnp.int32, (tk, tq), 0)
                           - lax.broadcasted_iota(jnp.int32, (tk, tq), 1))
        for n, limit in enumerate(limits):
            mask_ref[n] = jnp.where(key_minus_query <= limit, 0.0, NEG_INF)
        rows = lambda n, v: [jnp.full((n, tq), v, F32) for _ in range(n_heads)]
        scores(1, 0)
        return rows(1, NEG_INF), rows(1, 1.0), rows(1, 1.0), rows(FOX_DH, 0.0)

    def scores(u, slot):
        lhs = kn_ref[pl.ds(pl.multiple_of(uk_ref[u] * tk, tk), tk), :]
        for e in range(n_heads):
            s_ref[slot, e] = jnp.dot(lhs, rhs_ref[uq_ref[u], e], preferred_element_type=F32)

    def numerators(u, slot, ms):
        first = uk_ref[u] == 0
        mask = mask_ref[um_ref[u]]
        new_ms, alphas = [], []
        for e in range(n_heads):
            m = jnp.where(first, NEG_INF, ms[e])
            s = s_ref[slot, e] + mask
            m_new = jnp.maximum(m, jnp.max(s, axis=0, keepdims=True))
            p_ref[slot, e] = jnp.exp2(s - m_new).astype(BF16)
            new_ms.append(m_new)
            alphas.append(jnp.exp2(m - m_new))
        return new_ms, alphas

    def values(u, slot, alphas, ls, accs):
        new_ls, new_accs = [], []
        for e in range(n_heads):
            pv = jnp.dot(va_ref[uk_ref[u], e], p_ref[slot, e], preferred_element_type=F32)
            new_ls.append(alphas[e] * ls[e] + pv[FOX_DH:FOX_DH + 1, :])
            new_accs.append(alphas[e] * accs[e] + pv[0:FOX_DH, :])
        o_t = jnp.concatenate([new_accs[e] / new_ls[e] for e in range(n_heads)], axis=0)
        o_ref[pl.ds(pl.multiple_of(uq_ref[u] * tq, tq), tq), :] = o_t.T.astype(BF16)
        return new_ls, new_accs

    def step(u, slot, carry):
        ms, alphas, ls, accs = carry
        new_ls, new_accs = values(u - 1, 1 - slot, alphas, ls, accs)
        scores(u + 1, 1 - slot)
        new_ms, new_alphas = numerators(u, slot, ms)
        return new_ms, new_alphas, new_ls, new_accs

    def body(k, carry):
        carry = step(2 * k + 1, 0, carry)
        return step(2 * k + 2, 1, carry)

    def run(lo, hi, state):
        for k in range(lo, hi):
            state = body(k, state)
        return state

    def finish(state):
        _, alphas, ls, accs = state
        values(n_units, 1, alphas, ls, accs)

    return prepare, run, finish


def _sample_attend(q, k_new, v_new, lf_new, sfx_ref, k_refs, v_refs, lf_refs):
    n_pages = len(k_refs)
    steps = q.shape[0]
    rows = FOX_HEADS * steps
    pad = PAGE_SIZE - steps

    def per_head_rows(x8):
        return jnp.broadcast_to(x8[:, None, :], (FOX_HEADS, steps, x8.shape[1])).reshape(rows, x8.shape[1])

    q = q.astype(F32)
    q_rows = jnp.concatenate([q] * FOX_HEADS, axis=0)
    row_head = lax.broadcasted_iota(jnp.int32, (rows, FOX_WIDTH), 0) // steps
    lane_head = lax.broadcasted_iota(jnp.int32, (rows, FOX_WIDTH), 1) // FOX_DH
    head_mask = row_head == lane_head
    qbd = jnp.where(head_mask, q_rows, 0.0).astype(BF16)

    pn = _cumsum_rows(lf_new, steps)
    pn_rows = jnp.concatenate([pn] * FOX_HEADS, axis=0)
    r_h = lax.broadcasted_iota(jnp.int32, (rows, LANES), 0) // steps
    r_i = lax.broadcasted_iota(jnp.int32, (rows, LANES), 0) % steps
    c_l = lax.broadcasted_iota(jnp.int32, (rows, LANES), 1)
    pcol = jnp.sum(jnp.where(c_l == r_h, pn_rows, 0.0), axis=1, keepdims=True)
    pn_t = jnp.concatenate([pn, jnp.zeros((pad, LANES), F32)], axis=0).T[0:FOX_HEADS, :]

    lf_all = jnp.concatenate([lf_refs[j][...] for j in range(n_pages)], axis=0)
    sfx = _dot_f32_lhs(lf_all, sfx_ref[...])
    carry = jnp.zeros((FOX_HEADS, PAGE_SIZE), F32)
    page_bias = [None] * n_pages
    for j in reversed(range(n_pages)):
        blk = sfx[j * FOX_HEADS:(j + 1) * FOX_HEADS, :]
        page_bias[j] = blk[:, 0:PAGE_SIZE] + carry
        carry = carry + blk[:, PAGE_SIZE:2 * PAGE_SIZE]

    s_tiles = []
    for j in range(n_pages):
        s = jnp.dot(qbd, k_refs[j][...].astype(BF16), preferred_element_type=F32)
        s_tiles.append(s + (per_head_rows(page_bias[j]) + pcol))
    kn = jnp.concatenate([k_new, jnp.zeros((pad, FOX_WIDTH), F32)], axis=0).astype(BF16)
    s_new = lax.dot_general(qbd, kn, _NT, preferred_element_type=F32) + (pcol - per_head_rows(pn_t))
    s_tiles.append(jnp.where(c_l <= r_i, s_new, NEG_INF))

    m_el = s_tiles[0]
    for s in s_tiles[1:]:
        m_el = jnp.maximum(m_el, s)
    m = jnp.max(m_el, axis=1, keepdims=True)
    p_tiles = [jnp.exp(s - m) for s in s_tiles]
    l_el = p_tiles[0]
    for p in p_tiles[1:]:
        l_el = l_el + p
    l = jnp.sum(l_el, axis=1, keepdims=True)

    vn = jnp.concatenate([v_new, jnp.zeros((pad, FOX_WIDTH), F32)], axis=0).astype(BF16)
    o = jnp.dot(p_tiles[n_pages].astype(BF16), vn, preferred_element_type=F32)
    for j in range(n_pages):
        o = o + lax.dot_general(p_tiles[j].astype(BF16), v_refs[j][...].astype(BF16), _NT,
                                preferred_element_type=F32)
    o = jnp.where(head_mask, o / l, 0.0)
    return jnp.sum(o.reshape(FOX_HEADS, steps, FOX_WIDTH), axis=0)


def _fox_kernel(uq_ref, uk_ref, um_ref, pt_ref,
                qt_ref, kt_ref, vt_ref, bk_ref, bqt_ref,
                sq_ref, sk_ref, sv_ref, slf_ref, sfx_ref, ck_hbm, cv_hbm, cl_hbm,
                o_ref, so_ref,
                kn_ref, va_ref, rhs_ref, mask_ref, s_ref, p_ref, k_buf, v_buf, lf_buf, sems,
                *, tq, tk, n_units, limits, n_pages):
    step = pl.program_id(0) * pl.num_programs(1) + pl.program_id(1)
    n_steps = pl.num_programs(0) * pl.num_programs(1)
    per_step = sq_ref.shape[0]
    depth = PAGE_PREFETCH_DEPTH
    assert 0 < depth < per_step
    last_seq = n_steps * per_step - 1

    def page_copies(seq, slot):
        copies = []
        for j in range(n_pages):
            page = pt_ref[seq * n_pages + j]
            copies.append(pltpu.make_async_copy(ck_hbm.at[page], k_buf.at[slot, j], sems.at[0, slot]))
            copies.append(pltpu.make_async_copy(cv_hbm.at[page], v_buf.at[slot, j], sems.at[0, slot]))
            copies.append(pltpu.make_async_copy(cl_hbm.at[page], lf_buf.at[slot, j], sems.at[1, slot]))
        return copies

    @pl.when(step == 0)
    def _():
        for d in range(depth):
            for c in page_copies(d, d):
                c.start()

    prepare, run, finish = _prompt_attention(
        uq_ref, uk_ref, um_ref, qt_ref, kt_ref, vt_ref, bk_ref, bqt_ref, o_ref,
        kn_ref, va_ref, rhs_ref, mask_ref, s_ref, p_ref,
        pair=pl.program_id(1), tq=tq, tk=tk, n_units=n_units, limits=limits)
    state = prepare()
    n_pairs = n_units // 2
    bounds = [(j * n_pairs) // per_step for j in range(per_step + 1)]
    for j in range(per_step):
        seq = step * per_step + j
        slot = j
        ahead = jnp.minimum(seq + depth, last_seq)
        for c in page_copies(ahead, (j + depth) % per_step):
            c.start()
        for c in page_copies(seq, slot):
            c.wait()
        so_ref[j] = _sample_attend(
            sq_ref[j], sk_ref[j], sv_ref[j], slf_ref[j], sfx_ref,
            [k_buf.at[slot, n] for n in range(n_pages)], [v_buf.at[slot, n] for n in range(n_pages)],
            [lf_buf.at[slot, n] for n in range(n_pages)]).astype(BF16)
        state = run(bounds[j], bounds[j + 1], state)
    finish(state)

    @pl.when(step == n_steps - 1)
    def _():
        for d in range(depth):
            for c in page_copies(last_seq, d):
                c.wait()


def _fox(fqt, fkt, fvt, bk, bqt, page_table, sq3, sk3, sv3, slf3, sfx_mat, cache_kt, cache_vt, cache_lft,
         tq, tk):
    batch, _, seq = fqt.shape
    nseq, n_pages = page_table.shape
    steps = sq3.shape[1]
    assert tk % tq == 0 and seq % tk == 0
    nq = seq // tq
    pairs = FOX_WIDTH // LANES
    n_heads = LANES // FOX_DH
    per_step = nseq // (batch * pairs)
    assert per_step * batch * pairs == nseq and per_step > PAGE_PREFETCH_DEPTH
    units = [(i, j) for i in range(nq) for j in range((i * tq) // tk + 1)]
    n_units = len(units)
    assert n_units % 2 == 0
    padded = [units[0]] + units + [units[-1]]
    unit_limits = [min(i * tq - j * tk, tk - 1) for i, j in padded]
    limits = tuple(sorted(set(unit_limits)))
    table = lambda vals: jnp.asarray(vals, jnp.int32)
    uq, uk = table([u[0] for u in padded]), table([u[1] for u in padded])
    um = table([limits.index(v) for v in unit_limits])
    head_pair = lambda b, p, *_: (b, p, 0)
    per_batch = lambda b, p, *_: (b, 0, 0)
    sample = lambda w: pl.BlockSpec((per_step, steps, w), lambda b, p, *_: (b * pairs + p, 0, 0))
    in_hbm = pl.BlockSpec(memory_space=pl.ANY)
    grid_spec = pltpu.PrefetchScalarGridSpec(
        num_scalar_prefetch=4,
        grid=(batch, pairs),
        in_specs=[pl.BlockSpec((None, LANES, seq), head_pair),
                  pl.BlockSpec((None, LANES, seq), head_pair),
                  pl.BlockSpec((None, LANES, seq), head_pair),
                  pl.BlockSpec((None, seq, LANES), per_batch),
                  pl.BlockSpec((None, LANES, seq), per_batch),
                  sample(FOX_WIDTH), sample(FOX_WIDTH), sample(FOX_WIDTH), sample(LANES),
                  pl.BlockSpec(sfx_mat.shape, lambda b, p, *_: (0, 0)), in_hbm, in_hbm, in_hbm],
        out_specs=[pl.BlockSpec((seq, LANES), lambda b, p, *_: (b, p)), sample(FOX_WIDTH)],
        scratch_shapes=[pltpu.VMEM((seq, 2 * LANES), BF16),
                        pltpu.VMEM((seq // tk, n_heads, FOX_DH + ONES_ROWS, tk), BF16),
                        pltpu.VMEM((nq, n_heads, 2 * LANES, tq), BF16),
                        pltpu.VMEM((len(limits), tk, tq), F32),
                        pltpu.VMEM((2, n_heads, tk, tq), F32),
                        pltpu.VMEM((2, n_heads, tk, tq), BF16),
                        pltpu.VMEM((per_step, n_pages, FOX_WIDTH, PAGE_SIZE), cache_kt.dtype),
                        pltpu.VMEM((per_step, n_pages, FOX_WIDTH, PAGE_SIZE), cache_vt.dtype),
                        pltpu.VMEM((per_step, n_pages, FOX_HEADS, PAGE_SIZE), cache_lft.dtype),
                        pltpu.SemaphoreType.DMA((2, per_step))],
    )
    return pl.pallas_call(
        functools.partial(_fox_kernel, tq=tq, tk=tk, n_units=n_units, limits=limits, n_pages=n_pages),
        grid_spec=grid_spec,
        out_shape=[jax.ShapeDtypeStruct((batch * seq, FOX_WIDTH), BF16),
                   jax.ShapeDtypeStruct((nseq, steps, FOX_WIDTH), BF16)],
        compiler_params=_params("arbitrary", "arbitrary"),
        name="fox",
    )(uq, uk, um, page_table.reshape(-1), fqt, fkt, fvt, bk, bqt, sq3, sk3, sv3, slf3, sfx_mat,
      cache_kt, cache_vt, cache_lft)


def _post_kernel(x_ref, mod_ref, oh_ref, gate_ref, of_ref, mg_ref, onorm_ref, n1post_ref, n2pre_ref,
                 n2post_ref, wbh_ref, wbf_ref, wout_ref, wup_ref, wdn_ref, y_ref, *, ff_chunk):
    g_, r_, d = x_ref.shape
    m = g_ * r_
    mod = mod_ref[...]
    gt1 = mod[:, :, 2 * d:3 * d]
    sh2 = mod[:, :, 3 * d:4 * d]
    sc2 = mod[:, :, 4 * d:5 * d]
    gt2 = mod[:, :, 5 * d:6 * d]

    oh = oh_ref[...]
    parts = [_rms(oh[:, h * HG_DV:(h + 1) * HG_DV], onorm_ref[...]) for h in range(HG_HEADS)]
    ohn = (jnp.concatenate(parts, axis=1) * gate_ref[...].astype(F32)).astype(BF16)
    br_h = jnp.dot(ohn, wbh_ref[...], preferred_element_type=F32)
    br_f = jnp.dot(of_ref[...], wbf_ref[...], preferred_element_type=F32)
    mg = mg_ref[...].astype(F32)
    z = (mg[:, 0:d] * br_h + mg[:, d:2 * d] * br_f).astype(BF16)
    y = jnp.dot(z, wout_ref[...], preferred_element_type=F32).reshape(g_, r_, d)
    x1 = x_ref[...] + _rms(y, gt1 * n1post_ref[...])
    h2 = (_rms(x1, n2pre_ref[...] * (1.0 + sc2)) + sh2).reshape(m, d).astype(BF16)
    u = jnp.zeros((m, d), F32)
    for c in range(wup_ref.shape[1] // ff_chunk):
        cs = slice(c * ff_chunk, (c + 1) * ff_chunk)
        a = jnp.maximum(jnp.dot(h2, wup_ref[:, cs], preferred_element_type=F32), 0.0)
        u = u + jnp.dot((a * a).astype(BF16), wdn_ref[cs, :], preferred_element_type=F32)
    y_ref[...] = x1 + _rms(u.reshape(g_, r_, d), gt2 * n2post_ref[...])


def _post(x3, mod3, oh, gate, of, mg, onorm, n1post, n2pre, n2post, wbh, wbf, wout, wup, wdn,
          groups, rows, ff_chunk=2048):
    nb, r_all, d = x3.shape
    nt = r_all // rows
    steps = (nb // groups) * nt
    m = groups * rows
    xmap = lambda s: (s // nt, s % nt, 0)
    mmap = lambda s: (s // nt, 0, 0)
    tmap = lambda s: (s, 0)
    tok = lambda a: pl.BlockSpec((m, a.shape[1]), tmap)
    consts = [onorm, n1post, n2pre, n2post, wbh, wbf, wout, wup, wdn]
    return pl.pallas_call(
        functools.partial(_post_kernel, ff_chunk=ff_chunk),
        grid=(steps,),
        in_specs=[pl.BlockSpec((groups, rows, d), xmap),
                  pl.BlockSpec((groups, 1, mod3.shape[2]), mmap),
                  tok(oh), tok(gate), tok(of), tok(mg)] + [_resident(c.shape) for c in consts],
        out_specs=pl.BlockSpec((groups, rows, d), xmap),
        out_shape=jax.ShapeDtypeStruct(x3.shape, F32),
        compiler_params=_params("arbitrary"),
        name="post",
    )(x3, mod3, oh, gate, of, mg, *consts)


def _bias_place_matrices():
    shape = (2 * N_PIECES, LANES, LANES)
    s = lax.broadcasted_iota(jnp.int32, shape, 0)
    r = lax.broadcasted_iota(jnp.int32, shape, 1)
    c = lax.broadcasted_iota(jnp.int32, shape, 2)
    return jnp.logical_and(r < FOX_HEADS, c == BIAS_SLOTS * r + s).astype(BF16)


def _page_suffix_matrix():
    r = lax.broadcasted_iota(jnp.int32, (PAGE_SIZE, 2 * PAGE_SIZE), 0)
    c = lax.broadcasted_iota(jnp.int32, (PAGE_SIZE, 2 * PAGE_SIZE), 1)
    return jnp.logical_or(r > c, c >= PAGE_SIZE).astype(BF16)


def kernel(x_prompt, x_sample, c_prompt, c_sample, cache_k, cache_v, cache_logf, state_hgrn, page_table,
           ada_w, ada_b, norm_mix_pre, norm_mix_post, norm_mlp_pre, norm_mlp_post, w_in,
           hgrn_lower_bounds, hgrn_onorm, fox_b_f, w_br_h, w_br_f, w_out, w_mlp_up, w_mlp_down):
    batch, seq, d = x_prompt.shape
    nseq, steps, _ = x_sample.shape
    layer = 0
    n_phys = cache_k.shape[1]

    w_t = jnp.transpose(w_in[layer]).astype(BF16)
    n_a = 4 * HG_WIDTH + 3 * FOX_WIDTH
    wa = w_t[:n_a]
    wff = jnp.pad(w_t[n_a:n_a + FOX_HEADS], ((0, LANES - FOX_HEADS), (0, 0)))
    wmg = w_t[n_a + FOX_HEADS:]
    bf128 = jnp.pad(fox_b_f[layer], (0, LANES - FOX_HEADS)).reshape(1, LANES)
    lbraw = hgrn_lower_bounds
    vec3 = lambda v: v.reshape(1, 1, -1)
    wbh, wbf = w_br_h[layer].astype(BF16), w_br_f[layer].astype(BF16)
    wout = w_out[layer].astype(BF16)
    wup, wdn = w_mlp_up[layer].astype(BF16), w_mlp_down[layer].astype(BF16)
    onorm = hgrn_onorm[layer].reshape(1, HG_DV)

    n_c = batch + nseq
    c_pad = -n_c % (2 * SUBLANES)
    c_all = jnp.concatenate([c_prompt, c_sample, jnp.zeros((c_pad, d), F32)], axis=0)
    mod = _ada(c_all, ada_w[layer], ada_b[layer].reshape(1, -1))
    mod_p = mod[:batch].reshape(batch, 1, -1)
    mod_s = mod[batch:n_c].reshape(nseq, 1, -1)

    tm = TOKEN_TILE
    sgroups = tm // steps
    proj_args = (vec3(norm_mix_pre[layer]), lbraw, bf128, wa, wff, wmg)
    (q_p, g_p, k_p, v_p, gate_p, fq_p, fk_p, fv_p, lf8_p, lf128_p, mg_p) = _inproj(
        x_prompt, mod_p, *proj_args, groups=1, rows=tm, fox_transposed=True)
    (q_s, g_s, k_s, v_s, gate_s, fq_s, fk_s, fv_s, lf8_s, lf128_s, mg_s) = _inproj(
        x_sample, mod_s, *proj_args, groups=sgroups, rows=steps, fox_transposed=False)

    chunk_p = HG_CHUNK if seq % HG_CHUNK == 0 else seq
    oh_p, s_p = _hgrn_prompt(q_p, k_p, g_p, v_p, batch, seq, chunk_p)
    oh_s, s_s = _hgrn_sample(q_s, k_s, g_s, v_s, state_hgrn[layer], steps)

    tq, tk = FOX_Q_TILE, FOX_K_TILE
    bias_k, bias_qt = _decay(lf128_p.reshape(batch, seq, LANES), _bias_place_matrices())
    ckt = jnp.transpose(cache_k[layer], (0, 2, 3, 1)).reshape(n_phys, FOX_WIDTH, PAGE_SIZE)
    cvt = jnp.transpose(cache_v[layer], (0, 2, 3, 1)).reshape(n_phys, FOX_WIDTH, PAGE_SIZE)
    clt = jnp.transpose(cache_logf[layer], (0, 2, 1))
    per_seq = lambda a: a.reshape(nseq, steps, a.shape[-1])
    of_p, of_s = _fox(fq_p, fk_p, fv_p, bias_k, bias_qt, page_table, per_seq(fq_s), per_seq(fk_s), per_seq(fv_s),
                      per_seq(lf128_s), _page_suffix_matrix(), ckt, cvt, clt, tq, tk)
    of_s = of_s.reshape(nseq * steps, FOX_WIDTH)

    post_args = (onorm, vec3(norm_mix_post[layer]), vec3(norm_mlp_pre[layer]), vec3(norm_mlp_post[layer]),
                 wbh, wbf, wout, wup, wdn)
    y_p = _post(x_prompt, mod_p, oh_p, gate_p, of_p, mg_p, *post_args, groups=1, rows=tm)
    y_s = _post(x_sample, mod_s, oh_s, gate_s, of_s, mg_s, *post_args, groups=sgroups, rows=steps)

    k_prompt = fk_p.reshape(1, batch, FOX_HEADS, FOX_DH, seq).transpose(0, 1, 4, 2, 3)
    v_prompt = fv_p.reshape(1, batch, FOX_HEADS, FOX_DH, seq).transpose(0, 1, 4, 2, 3)
    logf_prompt = lf8_p.reshape(1, batch, seq, FOX_HEADS)
    k_sample = fk_s.reshape(1, nseq, steps, FOX_HEADS, FOX_DH)
    v_sample = fv_s.reshape(1, nseq, steps, FOX_HEADS, FOX_DH)
    logf_sample = lf8_s.reshape(1, nseq, steps, FOX_HEADS)
    return (y_p, y_s, k_prompt, v_prompt, logf_prompt, s_p[None], k_sample, v_sample, logf_sample, s_s[None])
```

```python
import functools

import jax
import jax.numpy as jnp
from jax import lax
from jax.experimental import pallas as pl
from jax.experimental.pallas import tpu as pltpu

F32 = jnp.float32
BF16 = jnp.bfloat16

LANES = 128
SUBLANES = 8
VMEM_LIMIT_BYTES = 56 * 1024 * 1024
TOKEN_TILE = 512
FOX_Q_TILE = 256
FOX_K_TILE = 512

HG_HEADS = 4
HG_DK = 128
HG_DV = 128
HG_CHUNK = 32
FOX_HEADS = 8
FOX_DH = 64
FOX_WIDTH = FOX_HEADS * FOX_DH
HG_WIDTH = HG_HEADS * HG_DV
PAGE_SIZE = 128
RMS_EPS = 1e-6
NEG_INF = float("-inf")
LOG2E = 1.4426950408889634
N_PIECES = 3
BIAS_SLOTS = 8

_NT = (((1,), (1,)), ((), ()))


def _params(*sem):
    return pltpu.CompilerParams(dimension_semantics=sem, vmem_limit_bytes=VMEM_LIMIT_BYTES)


def _resident(shape):
    nd = len(shape)
    return pl.BlockSpec(shape, lambda *_: (0,) * nd, pipeline_mode=pl.Buffered(1))


def _sigmoid(x):
    return 0.5 * jnp.tanh(0.5 * x) + 0.5


def _split3(x):
    hi = x.astype(BF16)
    r1 = x - hi.astype(F32)
    mid = r1.astype(BF16)
    lo = (r1 - mid.astype(F32)).astype(BF16)
    return hi, mid, lo


def _dot_f32_lhs(x, w):
    hi, mid, lo = _split3(x)
    d = lambda p: jnp.dot(p, w, preferred_element_type=F32)
    return (d(lo) + d(mid)) + d(hi)


def _cumsum_rows(x, period):
    row = lax.broadcasted_iota(jnp.int32, x.shape, 0) & (period - 1)
    s = 1
    while s < period:
        x = x + jnp.where(row >= s, pltpu.roll(x, s, axis=0), 0.0)
        s *= 2
    return x


def _rms(x, w):
    return x * lax.rsqrt(jnp.mean(x * x, axis=-1, keepdims=True) + RMS_EPS) * w


def _ada_kernel(c_ref, w_ref, b_ref, o_ref):
    c = c_ref[...]
    a = (c * _sigmoid(c)).astype(BF16)
    o_ref[...] = jnp.dot(a, w_ref[...].astype(BF16), preferred_element_type=F32) + b_ref[...]


def _ada(c, w, b, tn=1536):
    m, d = c.shape
    n = w.shape[1]
    return pl.pallas_call(
        _ada_kernel,
        grid=(n // tn,),
        in_specs=[pl.BlockSpec((m, d), lambda j: (0, 0)),
                  pl.BlockSpec((d, tn), lambda j: (0, j)),
                  pl.BlockSpec((1, tn), lambda j: (0, j))],
        out_specs=pl.BlockSpec((m, tn), lambda j: (0, j)),
        out_shape=jax.ShapeDtypeStruct((m, n), F32),
        compiler_params=_params("arbitrary"),
        name="ada",
    )(c, w, b)


def _inproj_kernel(x_ref, mod_ref, n1_ref, lbraw_ref, bf_ref, wa_ref, wff_ref, wmg_ref,
                   q_ref, g_ref, k_ref, v_ref, gate_ref, fq_ref, fk_ref, fv_ref,
                   lf8_ref, lf128_ref, mg_ref, *, fox_transposed):
    g_, r_, d = x_ref.shape
    m = g_ * r_
    x = x_ref[...]
    mod = mod_ref[...]
    sh1 = mod[:, :, 0:d]
    sc1 = mod[:, :, d:2 * d]
    h = _rms(x, n1_ref[...] * (1.0 + sc1)) + sh1
    hb = h.reshape(m, d).astype(BF16)

    raw = lbraw_ref[...]
    e = jnp.exp(raw - jnp.max(raw, axis=0, keepdims=True))
    lb = e[0:1, :] / jnp.sum(e, axis=0, keepdims=True)

    w = HG_WIDTH

    def proj(c):
        return lax.dot_general(hb, wa_ref[c * w:(c + 1) * w, :], _NT, preferred_element_type=F32)

    def proj_t(c):
        return lax.dot_general(wa_ref[c * w:(c + 1) * w, :], hb, _NT, preferred_element_type=F32)

    def merge_gate(c):
        mgc = lax.dot_general(hb, wmg_ref[c * w:(c + 1) * w, :], _NT, preferred_element_type=F32)
        mg_ref[:, c * w:(c + 1) * w] = _sigmoid(mgc).astype(BF16)

    def fox(c, ref, scale=None):
        y = proj_t(c) if fox_transposed else proj(c)
        y = y if scale is None else y * scale
        if fox_transposed:
            ref[0] = y.astype(ref.dtype)
        else:
            ref[...] = y.astype(ref.dtype)

    assert wmg_ref.shape[0] == 4 * w
    merge_gate(0)
    q_ref[...] = proj(0) * (HG_DK ** -0.5)
    merge_gate(1)
    v_ref[...] = proj(2).astype(BF16)
    merge_gate(2)
    fox(4, fq_ref, (LOG2E if fox_transposed else 1.0) * FOX_DH ** -0.5)
    merge_gate(3)
    fox(5, fk_ref)
    half_t = 0.5 * jnp.tanh(0.5 * proj(1))
    g_ref[...] = jnp.log(lb + (1.0 - lb) * (0.5 + half_t))
    k_ref[...] = (1.0 - lb) * (0.5 - half_t)
    fox(6, fv_ref)
    hg = proj(3)
    gate_ref[...] = (hg * _sigmoid(hg)).astype(BF16)
    z = lax.dot_general(hb, wff_ref[...], _NT, preferred_element_type=F32) + bf_ref[...]
    lf = jnp.minimum(z, 0.0) - jnp.log1p(jnp.exp(-jnp.abs(z)))
    lf128_ref[...] = lf
    lf8_ref[...] = lf[:, 0:FOX_HEADS]


def _inproj(x3, mod3, n1, lbraw, bf128, wa, wff, wmg, groups, rows, fox_transposed):
    assert groups == 1 or not fox_transposed
    nb, r_all, d = x3.shape
    nt = r_all // rows
    steps = (nb // groups) * nt
    t = nb * r_all
    m = groups * rows
    xmap = lambda s: (s // nt, s % nt, 0)
    mmap = lambda s: (s // nt, 0, 0)
    omap = lambda s: (s, 0)

    def out(width, dtype):
        return jax.ShapeDtypeStruct((t, width), dtype), pl.BlockSpec((m, width), omap)

    def fox(dtype):
        if not fox_transposed:
            return out(FOX_WIDTH, dtype)
        return (jax.ShapeDtypeStruct((nb, FOX_WIDTH, r_all), dtype),
                pl.BlockSpec((1, FOX_WIDTH, m), lambda s: (s // nt, 0, s % nt)))

    outs = [out(HG_WIDTH, F32), out(HG_WIDTH, F32), out(HG_WIDTH, F32), out(HG_WIDTH, BF16),
            out(HG_WIDTH, BF16), fox(BF16), fox(F32), fox(F32),
            out(FOX_HEADS, F32), out(LANES, F32), out(wmg.shape[0], BF16)]
    return pl.pallas_call(
        functools.partial(_inproj_kernel, fox_transposed=fox_transposed),
        grid=(steps,),
        in_specs=[pl.BlockSpec((groups, rows, d), xmap),
                  pl.BlockSpec((groups, 1, mod3.shape[2]), mmap),
                  _resident(n1.shape), _resident(lbraw.shape), _resident(bf128.shape),
                  _resident(wa.shape), _resident(wff.shape), _resident(wmg.shape)],
        out_specs=[o[1] for o in outs],
        out_shape=[o[0] for o in outs],
        compiler_params=_params("arbitrary"),
        name="inproj",
    )(x3, mod3, n1, lbraw, bf128, wa, wff, wmg)


def _decay_kernel(lf_ref, place_ref, bk_ref, bqt_ref):
    seq = lf_ref.shape[1]
    lane = lax.broadcasted_iota(jnp.int32, (LANES, LANES), 1)
    slot = lane & (BIAS_SLOTS - 1)
    used = lane < FOX_HEADS * BIAS_SLOTS
    ones_k = jnp.where(jnp.logical_and(used, slot < N_PIECES), 1.0, 0.0)
    ones_q = jnp.where(jnp.logical_and(used, jnp.logical_and(slot >= N_PIECES, slot < 2 * N_PIECES)), 1.0, 0.0)
    carry = jnp.zeros((1, LANES), F32)
    for j in range(seq // LANES):
        sl = slice(j * LANES, (j + 1) * LANES)
        cs = _cumsum_rows(lf_ref[0, sl, :], LANES) + carry
        carry = cs[LANES - 1:LANES, :]
        pieces = _split3(cs * LOG2E)

        def place(base):
            return sum(jnp.dot(pieces[j], place_ref[base + j], preferred_element_type=F32)
                       for j in range(N_PIECES))

        bk_ref[0, sl, :] = (ones_k - place(N_PIECES)).astype(BF16)
        bqt_ref[0, :, sl] = (ones_q + place(0)).T.astype(BF16)


def _decay(lf3, place):
    b, seq, _ = lf3.shape
    return pl.pallas_call(
        _decay_kernel,
        grid=(b,),
        in_specs=[pl.BlockSpec((1, seq, LANES), lambda i: (i, 0, 0)), _resident(place.shape)],
        out_specs=[pl.BlockSpec((1, seq, LANES), lambda i: (i, 0, 0)),
                   pl.BlockSpec((1, LANES, seq), lambda i: (i, 0, 0))],
        out_shape=[jax.ShapeDtypeStruct((b, seq, LANES), BF16),
                   jax.ShapeDtypeStruct((b, LANES, seq), BF16)],
        compiler_params=_params("arbitrary"),
        name="decay",
    )(lf3, place)


def _hgrn_decays(q, k, g, chunk):
    rows = q.shape[0]
    n_chunks = rows // chunk
    b = _cumsum_rows(g, chunk)
    qt = (q * jnp.exp(b)).astype(BF16)
    kt = (k * jnp.exp(-b)).astype(BF16)
    b3 = b.reshape(n_chunks, chunk, HG_DK)
    bl3 = b3[:, chunk - 1:chunk, :]
    kd = (k.reshape(n_chunks, chunk, HG_DK) * jnp.exp(bl3 - b3)).reshape(rows, HG_DK).astype(BF16)
    return qt, kt, kd, jnp.exp(bl3)


def _hgrn_products(qt, kt, kd, v, chunk):
    rows = qt.shape[0]
    n_chunks = rows // chunk
    shift = chunk.bit_length() - 1
    a = lax.dot_general(qt, kt, _NT, preferred_element_type=F32)
    v_t = v.astype(F32).T.astype(BF16)
    col_chunk = lax.broadcasted_iota(jnp.int32, v_t.shape, 1) >> shift
    v_blocks = jnp.concatenate([jnp.where(col_chunk == c, v_t, jnp.zeros_like(v_t)) for c in range(n_chunks)],
                               axis=0)
    incr = jnp.dot(v_blocks, kd, preferred_element_type=F32)
    ri = lax.broadcasted_iota(jnp.int32, (rows, rows), 0)
    ci = lax.broadcasted_iota(jnp.int32, (rows, rows), 1)
    keep = (ri - ci).astype(jnp.uint32) <= (ri & (chunk - 1)).astype(jnp.uint32)
    o_intra = jnp.dot(jnp.where(keep, a, 0.0).astype(BF16), v, preferred_element_type=F32)
    return o_intra, incr


def _hgrn_blocks(blocks, chunk, state_in, state_out):
    n_chunks = LANES // chunk
    decays = [_hgrn_decays(q, k, g, chunk) for (q, k, g, _) in blocks]
    products = [_hgrn_products(qt, kt, kd, blk[3], chunk) for (qt, kt, kd, _), blk in zip(decays, blocks)]
    entering = []
    st = None
    for n in range(len(blocks) * n_chunks):
        j, c = divmod(n, n_chunks)
        st = state_in(n, st)
        entering.append(st.astype(BF16))
        st = st * decays[j][3][c] + products[j][1][c * HG_DV:(c + 1) * HG_DV, :]
        state_out(n, st)
    outs = []
    for j, ((qt, _, _, _), (o_intra, _)) in enumerate(zip(decays, products)):
        o_inter = [lax.dot_general(qt[c * chunk:(c + 1) * chunk, :], entering[j * n_chunks + c], _NT,
                                   preferred_element_type=F32) for c in range(n_chunks)]
        outs.append(o_intra + jnp.concatenate(o_inter, axis=0))
    return outs


HGRN_GROUP = 8


def _hgrn_prompt_kernel(q_ref, k_ref, g_ref, v_ref, o_ref, s_ref, *, chunk):
    seq = q_ref.shape[0]
    group_rows = HGRN_GROUP * LANES

    def body(i, st0):
        base = i * group_rows
        row_slices = [pl.ds(base + j * LANES, LANES) for j in range(HGRN_GROUP)]
        blocks = [(q_ref[rs, :], k_ref[rs, :], g_ref[rs, :], v_ref[rs, :]) for rs in row_slices]
        last = []
        outs = _hgrn_blocks(blocks, chunk, lambda n, prev: st0 if prev is None else prev,
                            lambda n, st: last.append(st))
        for rs, o in zip(row_slices, outs):
            o_ref[rs, :] = o
        return last[-1]

    st = jnp.zeros((HG_DV, HG_DK), F32)
    for i in range(seq // group_rows):
        st = body(i, st)
    s_ref[0, 0] = st.T


def _hgrn_prompt(q, k, g, v, batch, seq, chunk):
    spec = pl.BlockSpec((seq, HG_DK), lambda b, h: (b, h))
    return pl.pallas_call(
        functools.partial(_hgrn_prompt_kernel, chunk=chunk),
        grid=(batch, HG_HEADS),
        in_specs=[spec, spec, spec, spec],
        out_specs=[spec, pl.BlockSpec((1, 1, HG_DK, HG_DV), lambda b, h: (b, h, 0, 0))],
        out_shape=[jax.ShapeDtypeStruct((batch * seq, HG_WIDTH), F32),
                   jax.ShapeDtypeStruct((batch, HG_HEADS, HG_DK, HG_DV), F32)],
        compiler_params=_params("arbitrary", "arbitrary"),
        name="hgrn_prompt",
    )(q, k, g, v)


def _hgrn_sample_kernel(q_ref, k_ref, g_ref, v_ref, s0_ref, o_ref, s_ref, *, chunk):
    per = LANES // chunk

    def state_in(n, prev):
        h, c = divmod(n, per)
        return s0_ref[c, h].T

    def state_out(n, st):
        h, c = divmod(n, per)
        s_ref[c, h] = st.T

    heads = [slice(h * HG_DK, (h + 1) * HG_DK) for h in range(HG_HEADS)]
    outs = _hgrn_blocks([(q_ref[:, hs], k_ref[:, hs], g_ref[:, hs], v_ref[:, hs]) for hs in heads],
                        chunk, state_in, state_out)
    for hs, o in zip(heads, outs):
        o_ref[:, hs] = o


def _hgrn_sample(q, k, g, v, s0, chunk):
    t = q.shape[0]
    per = LANES // chunk
    spec = pl.BlockSpec((LANES, HG_WIDTH), lambda i: (i, 0))
    sspec = pl.BlockSpec((per, HG_HEADS, HG_DK, HG_DV), lambda i: (i, 0, 0, 0))
    return pl.pallas_call(
        functools.partial(_hgrn_sample_kernel, chunk=chunk),
        grid=(t // LANES,),
        in_specs=[spec, spec, spec, spec, sspec],
        out_specs=[spec, sspec],
        out_shape=[jax.ShapeDtypeStruct((t, HG_WIDTH), F32),
                   jax.ShapeDtypeStruct(s0.shape, F32)],
        compiler_params=_params("arbitrary"),
        name="hgrn_sample",
    )(q, k, g, v, s0)


ONES_ROWS = 16
PAGE_PREFETCH_DEPTH = 3


def _prompt_attention(qt_ref, kt_ref, vt_ref, bk_ref, bqt_ref, o_ref,
                      kn_ref, va_ref, rhs_ref, mask_ref, s_ref, p_ref, *, pair, tq, tk, units):
    seq = kt_ref.shape[1]
    n_heads = LANES // FOX_DH
    n_units = len(units)
    limit = [i * tq - j * tk for i, j in units]
    masked_limits = sorted({v for v in limit if v < tk - 1})
    assert len(masked_limits) <= mask_ref.shape[0]
    is_first = [j == 0 for _, j in units]
    is_last = [u + 1 == n_units or units[u + 1][0] != units[u][0] for u in range(n_units)]

    def prepare():
        for c in range(seq // LANES):
            cs = slice(c * LANES, (c + 1) * LANES)
            kn_ref[cs, 0:LANES] = kt_ref[:, cs].T.astype(BF16)
        kn_ref[:, LANES:2 * LANES] = bk_ref[...]
        for c in range(seq // tk):
            for e in range(n_heads):
                va_ref[c, e, 0:FOX_DH, :] = vt_ref[e * FOX_DH:(e + 1) * FOX_DH, c * tk:(c + 1) * tk].astype(BF16)
                va_ref[c, e, FOX_DH:FOX_DH + ONES_ROWS, :] = jnp.ones((ONES_ROWS, tk), BF16)
        row = lax.broadcasted_iota(jnp.int32, (LANES, tq), 0)
        for i in range(seq // tq):
            qs = slice(i * tq, (i + 1) * tq)
            qt = qt_ref[:, qs]
            bqt = bqt_ref[:, qs]
            for e in range(n_heads):
                rhs_ref[i, e, 0:LANES, :] = jnp.where(row // FOX_DH == e, qt, jnp.zeros_like(qt))
                rhs_ref[i, e, LANES:2 * LANES, :] = jnp.where(row // BIAS_SLOTS == pair * n_heads + e, bqt,
                                                              jnp.zeros_like(bqt))
        key_minus_query = (lax.broadcasted_iota(jnp.int32, (tk, tq), 0)
                           - lax.broadcasted_iota(jnp.int32, (tk, tq), 1))
        for n, v in enumerate(masked_limits):
            mask_ref[n] = jnp.where(key_minus_query <= v, 0.0, NEG_INF)
        scores(0)
        return None, None, None, None

    def scores(u):
        qi, kj = units[u]
        lhs = kn_ref[kj * tk:(kj + 1) * tk, :]
        for e in range(n_heads):
            s_ref[u % 2, e] = jnp.dot(lhs, rhs_ref[qi, e], preferred_element_type=F32)

    def numerators(u, ms):
        new_ms, alphas = [], []
        for e in range(n_heads):
            s = s_ref[u % 2, e]
            if limit[u] < tk - 1:
                s = s + mask_ref[masked_limits.index(limit[u])]
            m_new = jnp.max(s, axis=0, keepdims=True)
            if not is_first[u]:
                m_new = jnp.maximum(ms[e], m_new)
                alphas.append(jnp.exp2(ms[e] - m_new))
            p_ref[u % 2, e] = jnp.exp2(s - m_new).astype(BF16)
            new_ms.append(m_new)
        return new_ms, alphas

    def values(u, alphas, ls, accs):
        qi, kj = units[u]
        new_ls, new_accs = [], []
        for e in range(n_heads):
            pv = jnp.dot(va_ref[kj, e], p_ref[u % 2, e], preferred_element_type=F32)
            l_e, acc_e = pv[FOX_DH:FOX_DH + 1, :], pv[0:FOX_DH, :]
            if not is_first[u]:
                l_e, acc_e = alphas[e] * ls[e] + l_e, alphas[e] * accs[e] + acc_e
            new_ls.append(l_e)
            new_accs.append(acc_e)
        if is_last[u]:
            o_t = jnp.concatenate([new_accs[e] / new_ls[e] for e in range(n_heads)], axis=0)
            o_ref[qi * tq:(qi + 1) * tq, :] = o_t.T.astype(BF16)
        return new_ls, new_accs

    def run(lo, hi, state):
        ms, alphas, ls, accs = state
        for u in range(lo, hi):
            if u > 0:
                ls, accs = values(u - 1, alphas, ls, accs)
            if u + 1 < n_units:
                scores(u + 1)
            ms, alphas = numerators(u, ms)
        return ms, alphas, ls, accs

    def finish(state):
        _, alphas, ls, accs = state
        values(n_units - 1, alphas, ls, accs)

    return prepare, run, finish


def _sample_attend(q, k_new, v_new, lf_new, sfx_ref, k_refs, v_refs, lf_refs):
    n_pages = len(k_refs)
    steps = q.shape[0]
    rows = FOX_HEADS * steps
    pad = PAGE_SIZE - steps

    def per_head_rows(x8):
        return jnp.broadcast_to(x8[:, None, :], (FOX_HEADS, steps, x8.shape[1])).reshape(rows, x8.shape[1])

    q = q.astype(F32)
    q_rows = jnp.concatenate([q] * FOX_HEADS, axis=0)
    row_head = lax.broadcasted_iota(jnp.int32, (rows, FOX_WIDTH), 0) // steps
    lane_head = lax.broadcasted_iota(jnp.int32, (rows, FOX_WIDTH), 1) // FOX_DH
    head_mask = row_head == lane_head
    qbd = jnp.where(head_mask, q_rows, 0.0).astype(BF16)

    pn = _cumsum_rows(lf_new, steps)
    pn_rows = jnp.concatenate([pn] * FOX_HEADS, axis=0)
    r_h = lax.broadcasted_iota(jnp.int32, (rows, LANES), 0) // steps
    r_i = lax.broadcasted_iota(jnp.int32, (rows, LANES), 0) % steps
    c_l = lax.broadcasted_iota(jnp.int32, (rows, LANES), 1)
    pcol = jnp.sum(jnp.where(c_l == r_h, pn_rows, 0.0), axis=1, keepdims=True)
    pn_t = jnp.concatenate([pn, jnp.zeros((pad, LANES), F32)], axis=0).T[0:FOX_HEADS, :]

    lf_all = jnp.concatenate([lf_refs[j][...] for j in range(n_pages)], axis=0)
    sfx = _dot_f32_lhs(lf_all, sfx_ref[...])
    carry = jnp.zeros((FOX_HEADS, PAGE_SIZE), F32)
    page_bias = [None] * n_pages
    for j in reversed(range(n_pages)):
        blk = sfx[j * FOX_HEADS:(j + 1) * FOX_HEADS, :]
        page_bias[j] = blk[:, 0:PAGE_SIZE] + carry
        carry = carry + blk[:, PAGE_SIZE:2 * PAGE_SIZE]

    s_tiles = []
    for j in range(n_pages):
        s = jnp.dot(qbd, k_refs[j][...].astype(BF16), preferred_element_type=F32)
        s_tiles.append(s + (per_head_rows(page_bias[j]) + pcol))
    kn = jnp.concatenate([k_new, jnp.zeros((pad, FOX_WIDTH), F32)], axis=0).astype(BF16)
    s_new = lax.dot_general(qbd, kn, _NT, preferred_element_type=F32) + (pcol - per_head_rows(pn_t))
    s_tiles.append(jnp.where(c_l <= r_i, s_new, NEG_INF))

    m_el = s_tiles[0]
    for s in s_tiles[1:]:
        m_el = jnp.maximum(m_el, s)
    m = jnp.max(m_el, axis=1, keepdims=True)
    p_tiles = [jnp.exp(s - m) for s in s_tiles]
    l_el = p_tiles[0]
    for p in p_tiles[1:]:
        l_el = l_el + p
    l = jnp.sum(l_el, axis=1, keepdims=True)

    vn = jnp.concatenate([v_new, jnp.zeros((pad, FOX_WIDTH), F32)], axis=0).astype(BF16)
    o = jnp.dot(p_tiles[n_pages].astype(BF16), vn, preferred_element_type=F32)
    for j in range(n_pages):
        o = o + lax.dot_general(p_tiles[j].astype(BF16), v_refs[j][...].astype(BF16), _NT,
                                preferred_element_type=F32)
    o = jnp.where(head_mask, o / l, 0.0)
    return jnp.sum(o.reshape(FOX_HEADS, steps, FOX_WIDTH), axis=0)


def _fox_kernel(pt_ref,
                qt_ref, kt_ref, vt_ref, bk_ref, bqt_ref,
                sq_ref, sk_ref, sv_ref, slf_ref, sfx_ref, ck_hbm, cv_hbm, cl_hbm,
                o_ref, so_ref,
                kn_ref, va_ref, rhs_ref, mask_ref, s_ref, p_ref, k_buf, v_buf, lf_buf, sems,
                *, tq, tk, units, n_pages):
    step = pl.program_id(0) * pl.num_programs(1) + pl.program_id(1)
    n_steps = pl.num_programs(0) * pl.num_programs(1)
    per_step = sq_ref.shape[0]
    depth = PAGE_PREFETCH_DEPTH
    assert 0 < depth < per_step
    last_seq = n_steps * per_step - 1

    def page_copies(seq, slot):
        copies = []
        for j in range(n_pages):
            page = pt_ref[seq * n_pages + j]
            copies.append(pltpu.make_async_copy(ck_hbm.at[page], k_buf.at[slot, j], sems.at[0, slot]))
            copies.append(pltpu.make_async_copy(cv_hbm.at[page], v_buf.at[slot, j], sems.at[0, slot]))
            copies.append(pltpu.make_async_copy(cl_hbm.at[page], lf_buf.at[slot, j], sems.at[1, slot]))
        return copies

    @pl.when(step == 0)
    def _():
        for d in range(depth):
            for c in page_copies(d, d):
                c.start()

    prepare, run, finish = _prompt_attention(
        qt_ref, kt_ref, vt_ref, bk_ref, bqt_ref, o_ref,
        kn_ref, va_ref, rhs_ref, mask_ref, s_ref, p_ref,
        pair=pl.program_id(1), tq=tq, tk=tk, units=units)
    state = prepare()
    bounds = [(j * len(units)) // per_step for j in range(per_step + 1)]
    for j in range(per_step):
        seq = step * per_step + j
        slot = j
        ahead = jnp.minimum(seq + depth, last_seq)
        for c in page_copies(ahead, (j + depth) % per_step):
            c.start()
        for c in page_copies(seq, slot):
            c.wait()
        so_ref[j] = _sample_attend(
            sq_ref[j], sk_ref[j], sv_ref[j], slf_ref[j], sfx_ref,
            [k_buf.at[slot, n] for n in range(n_pages)], [v_buf.at[slot, n] for n in range(n_pages)],
            [lf_buf.at[slot, n] for n in range(n_pages)]).astype(BF16)
        state = run(bounds[j], bounds[j + 1], state)
    finish(state)

    @pl.when(step == n_steps - 1)
    def _():
        for d in range(depth):
            for c in page_copies(last_seq, d):
                c.wait()


def _fox(fqt, fkt, fvt, bk, bqt, page_table, sq3, sk3, sv3, slf3, sfx_mat, cache_kt, cache_vt, cache_lft,
         tq, tk):
    batch, _, seq = fqt.shape
    nseq, n_pages = page_table.shape
    steps = sq3.shape[1]
    assert tk % tq == 0 and seq % tk == 0
    nq = seq // tq
    pairs = FOX_WIDTH // LANES
    n_heads = LANES // FOX_DH
    per_step = nseq // (batch * pairs)
    assert per_step * batch * pairs == nseq and per_step > PAGE_PREFETCH_DEPTH
    units = tuple((i, j) for i in range(nq) for j in range((i * tq) // tk + 1))
    n_masks = len({i * tq - j * tk for i, j in units if i * tq - j * tk < tk - 1})
    head_pair = lambda b, p, *_: (b, p, 0)
    per_batch = lambda b, p, *_: (b, 0, 0)
    sample = lambda w: pl.BlockSpec((per_step, steps, w), lambda b, p, *_: (b * pairs + p, 0, 0))
    in_hbm = pl.BlockSpec(memory_space=pl.ANY)
    grid_spec = pltpu.PrefetchScalarGridSpec(
        num_scalar_prefetch=1,
        grid=(batch, pairs),
        in_specs=[pl.BlockSpec((None, LANES, seq), head_pair),
                  pl.BlockSpec((None, LANES, seq), head_pair),
                  pl.BlockSpec((None, LANES, seq), head_pair),
                  pl.BlockSpec((None, seq, LANES), per_batch),
                  pl.BlockSpec((None, LANES, seq), per_batch),
                  sample(FOX_WIDTH), sample(FOX_WIDTH), sample(FOX_WIDTH), sample(LANES),
                  pl.BlockSpec(sfx_mat.shape, lambda b, p, *_: (0, 0)), in_hbm, in_hbm, in_hbm],
        out_specs=[pl.BlockSpec((seq, LANES), lambda b, p, *_: (b, p)), sample(FOX_WIDTH)],
        scratch_shapes=[pltpu.VMEM((seq, 2 * LANES), BF16),
                        pltpu.VMEM((seq // tk, n_heads, FOX_DH + ONES_ROWS, tk), BF16),
                        pltpu.VMEM((nq, n_heads, 2 * LANES, tq), BF16),
                        pltpu.VMEM((n_masks, tk, tq), F32),
                        pltpu.VMEM((2, n_heads, tk, tq), F32),
                        pltpu.VMEM((2, n_heads, tk, tq), BF16),
                        pltpu.VMEM((per_step, n_pages, FOX_WIDTH, PAGE_SIZE), cache_kt.dtype),
                        pltpu.VMEM((per_step, n_pages, FOX_WIDTH, PAGE_SIZE), cache_vt.dtype),
                        pltpu.VMEM((per_step, n_pages, FOX_HEADS, PAGE_SIZE), cache_lft.dtype),
                        pltpu.SemaphoreType.DMA((2, per_step))],
    )
    return pl.pallas_call(
        functools.partial(_fox_kernel, tq=tq, tk=tk, units=units, n_pages=n_pages),
        grid_spec=grid_spec,
        out_shape=[jax.ShapeDtypeStruct((batch * seq, FOX_WIDTH), BF16),
                   jax.ShapeDtypeStruct((nseq, steps, FOX_WIDTH), BF16)],
        compiler_params=_params("arbitrary", "arbitrary"),
        name="fox",
    )(page_table.reshape(-1), fqt, fkt, fvt, bk, bqt, sq3, sk3, sv3, slf3, sfx_mat,
      cache_kt, cache_vt, cache_lft)


def _post_kernel(x_ref, mod_ref, oh_ref, gate_ref, of_ref, mg_ref, onorm_ref, n1post_ref, n2pre_ref,
                 n2post_ref, wbh_ref, wbf_ref, wout_ref, wup_ref, wdn_ref, y_ref, *, ff_chunk):
    g_, r_, d = x_ref.shape
    m = g_ * r_
    mod = mod_ref[...]
    gt1 = mod[:, :, 2 * d:3 * d]
    sh2 = mod[:, :, 3 * d:4 * d]
    sc2 = mod[:, :, 4 * d:5 * d]
    gt2 = mod[:, :, 5 * d:6 * d]

    oh = oh_ref[...]
    parts = [_rms(oh[:, h * HG_DV:(h + 1) * HG_DV], onorm_ref[...]) for h in range(HG_HEADS)]
    ohn = (jnp.concatenate(parts, axis=1) * gate_ref[...].astype(F32)).astype(BF16)
    br_h = jnp.dot(ohn, wbh_ref[...], preferred_element_type=F32)
    br_f = jnp.dot(of_ref[...], wbf_ref[...], preferred_element_type=F32)
    mg = mg_ref[...].astype(F32)
    z = (mg[:, 0:d] * br_h + mg[:, d:2 * d] * br_f).astype(BF16)
    y = jnp.dot(z, wout_ref[...], preferred_element_type=F32).reshape(g_, r_, d)
    x1 = x_ref[...] + _rms(y, gt1 * n1post_ref[...])
    h2 = (_rms(x1, n2pre_ref[...] * (1.0 + sc2)) + sh2).reshape(m, d).astype(BF16)
    u = jnp.zeros((m, d), F32)
    for c in range(wup_ref.shape[1] // ff_chunk):
        cs = slice(c * ff_chunk, (c + 1) * ff_chunk)
        a = jnp.maximum(jnp.dot(h2, wup_ref[:, cs], preferred_element_type=F32), 0.0)
        u = u + jnp.dot((a * a).astype(BF16), wdn_ref[cs, :], preferred_element_type=F32)
    y_ref[...] = x1 + _rms(u.reshape(g_, r_, d), gt2 * n2post_ref[...])


def _post(x3, mod3, oh, gate, of, mg, onorm, n1post, n2pre, n2post, wbh, wbf, wout, wup, wdn,
          groups, rows, ff_chunk=2048):
    nb, r_all, d = x3.shape
    nt = r_all // rows
    steps = (nb // groups) * nt
    m = groups * rows
    xmap = lambda s: (s // nt, s % nt, 0)
    mmap = lambda s: (s // nt, 0, 0)
    tmap = lambda s: (s, 0)
    tok = lambda a: pl.BlockSpec((m, a.shape[1]), tmap)
    consts = [onorm, n1post, n2pre, n2post, wbh, wbf, wout, wup, wdn]
    return pl.pallas_call(
        functools.partial(_post_kernel, ff_chunk=ff_chunk),
        grid=(steps,),
        in_specs=[pl.BlockSpec((groups, rows, d), xmap),
                  pl.BlockSpec((groups, 1, mod3.shape[2]), mmap),
                  tok(oh), tok(gate), tok(of), tok(mg)] + [_resident(c.shape) for c in consts],
        out_specs=pl.BlockSpec((groups, rows, d), xmap),
        out_shape=jax.ShapeDtypeStruct(x3.shape, F32),
        compiler_params=_params("arbitrary"),
        name="post",
    )(x3, mod3, oh, gate, of, mg, *consts)


def _bias_place_matrices():
    shape = (2 * N_PIECES, LANES, LANES)
    s = lax.broadcasted_iota(jnp.int32, shape, 0)
    r = lax.broadcasted_iota(jnp.int32, shape, 1)
    c = lax.broadcasted_iota(jnp.int32, shape, 2)
    return jnp.logical_and(r < FOX_HEADS, c == BIAS_SLOTS * r + s).astype(BF16)


def _page_suffix_matrix():
    r = lax.broadcasted_iota(jnp.int32, (PAGE_SIZE, 2 * PAGE_SIZE), 0)
    c = lax.broadcasted_iota(jnp.int32, (PAGE_SIZE, 2 * PAGE_SIZE), 1)
    return jnp.logical_or(r > c, c >= PAGE_SIZE).astype(BF16)


def kernel(x_prompt, x_sample, c_prompt, c_sample, cache_k, cache_v, cache_logf, state_hgrn, page_table,
           ada_w, ada_b, norm_mix_pre, norm_mix_post, norm_mlp_pre, norm_mlp_post, w_in,
           hgrn_lower_bounds, hgrn_onorm, fox_b_f, w_br_h, w_br_f, w_out, w_mlp_up, w_mlp_down):
    batch, seq, d = x_prompt.shape
    nseq, steps, _ = x_sample.shape
    layer = 0
    n_phys = cache_k.shape[1]

    w_t = jnp.transpose(w_in[layer]).astype(BF16)
    n_a = 4 * HG_WIDTH + 3 * FOX_WIDTH
    wa = w_t[:n_a]
    wff = jnp.pad(w_t[n_a:n_a + FOX_HEADS], ((0, LANES - FOX_HEADS), (0, 0)))
    wmg = w_t[n_a + FOX_HEADS:]
    bf128 = jnp.pad(fox_b_f[layer], (0, LANES - FOX_HEADS)).reshape(1, LANES)
    lbraw = hgrn_lower_bounds
    vec3 = lambda v: v.reshape(1, 1, -1)
    wbh, wbf = w_br_h[layer].astype(BF16), w_br_f[layer].astype(BF16)
    wout = w_out[layer].astype(BF16)
    wup, wdn = w_mlp_up[layer].astype(BF16), w_mlp_down[layer].astype(BF16)
    onorm = hgrn_onorm[layer].reshape(1, HG_DV)

    n_c = batch + nseq
    c_pad = -n_c % (2 * SUBLANES)
    c_all = jnp.concatenate([c_prompt, c_sample, jnp.zeros((c_pad, d), F32)], axis=0)
    mod = _ada(c_all, ada_w[layer], ada_b[layer].reshape(1, -1))
    mod_p = mod[:batch].reshape(batch, 1, -1)
    mod_s = mod[batch:n_c].reshape(nseq, 1, -1)

    tm = TOKEN_TILE
    sgroups = tm // steps
    proj_args = (vec3(norm_mix_pre[layer]), lbraw, bf128, wa, wff, wmg)
    (q_p, g_p, k_p, v_p, gate_p, fq_p, fk_p, fv_p, lf8_p, lf128_p, mg_p) = _inproj(
        x_prompt, mod_p, *proj_args, groups=1, rows=tm, fox_transposed=True)
    (q_s, g_s, k_s, v_s, gate_s, fq_s, fk_s, fv_s, lf8_s, lf128_s, mg_s) = _inproj(
        x_sample, mod_s, *proj_args, groups=sgroups, rows=steps, fox_transposed=False)

    chunk_p = HG_CHUNK if seq % HG_CHUNK == 0 else seq
    oh_p, s_p = _hgrn_prompt(q_p, k_p, g_p, v_p, batch, seq, chunk_p)
    oh_s, s_s = _hgrn_sample(q_s, k_s, g_s, v_s, state_hgrn[layer], steps)

    tq, tk = FOX_Q_TILE, FOX_K_TILE
    bias_k, bias_qt = _decay(lf128_p.reshape(batch, seq, LANES), _bias_place_matrices())
    ckt = jnp.transpose(cache_k[layer], (0, 2, 3, 1)).reshape(n_phys, FOX_WIDTH, PAGE_SIZE)
    cvt = jnp.transpose(cache_v[layer], (0, 2, 3, 1)).reshape(n_phys, FOX_WIDTH, PAGE_SIZE)
    clt = jnp.transpose(cache_logf[layer], (0, 2, 1))
    per_seq = lambda a: a.reshape(nseq, steps, a.shape[-1])
    of_p, of_s = _fox(fq_p, fk_p, fv_p, bias_k, bias_qt, page_table, per_seq(fq_s), per_seq(fk_s), per_seq(fv_s),
                      per_seq(lf128_s), _page_suffix_matrix(), ckt, cvt, clt, tq, tk)
    of_s = of_s.reshape(nseq * steps, FOX_WIDTH)

    post_args = (onorm, vec3(norm_mix_post[layer]), vec3(norm_mlp_pre[layer]), vec3(norm_mlp_post[layer]),
                 wbh, wbf, wout, wup, wdn)
    y_p = _post(x_prompt, mod_p, oh_p, gate_p, of_p, mg_p, *post_args, groups=1, rows=tm)
    y_s = _post(x_sample, mod_s, oh_s, gate_s, of_s, mg_s, *post_args, groups=sgroups, rows=steps)

    k_prompt = fk_p.reshape(1, batch, FOX_HEADS, FOX_DH, seq).transpose(0, 1, 4, 2, 3)
    v_prompt = fv_p.reshape(1, batch, FOX_HEADS, FOX_DH, seq).transpose(0, 1, 4, 2, 3)
    logf_prompt = lf8_p.reshape(1, batch, seq, FOX_HEADS)
    k_sample = fk_s.reshape(1, nseq, steps, FOX_HEADS, FOX_DH)
    v_sample = fv_s.reshape(1, nseq, steps, FOX_HEADS, FOX_DH)
    logf_sample = lf8_s.reshape(1, nseq, steps, FOX_HEADS)
    return (y_p, y_s, k_prompt, v_prompt, logf_prompt, s_p[None], k_sample, v_sample, logf_sample, s_s[None])
```

```python
import functools

import jax
import jax.numpy as jnp
from jax import lax
from jax.experimental import pallas as pl
from jax.experimental.pallas import tpu as pltpu

F32 = jnp.float32
BF16 = jnp.bfloat16

LANES = 128
SUBLANES = 8
VMEM_LIMIT_BYTES = 56 * 1024 * 1024
TOKEN_TILE = 512
FOX_Q_TILE = 256
FOX_K_TILE = 512

HG_HEADS = 4
HG_DK = 128
HG_DV = 128
HG_CHUNK = 32
FOX_HEADS = 8
FOX_DH = 64
FOX_WIDTH = FOX_HEADS * FOX_DH
HG_WIDTH = HG_HEADS * HG_DV
PAGE_SIZE = 128
RMS_EPS = 1e-6
NEG_INF = float("-inf")
LOG2E = 1.4426950408889634
N_PIECES = 3
BIAS_SLOTS = 8

_NT = (((1,), (1,)), ((), ()))


def _params(*sem):
    return pltpu.CompilerParams(dimension_semantics=sem, vmem_limit_bytes=VMEM_LIMIT_BYTES)


def _resident(shape):
    nd = len(shape)
    return pl.BlockSpec(shape, lambda *_: (0,) * nd, pipeline_mode=pl.Buffered(1))


def _sigmoid(x):
    return 0.5 * jnp.tanh(0.5 * x) + 0.5


def _split3(x):
    hi = x.astype(BF16)
    r1 = x - hi.astype(F32)
    mid = r1.astype(BF16)
    lo = (r1 - mid.astype(F32)).astype(BF16)
    return hi, mid, lo


def _dot_f32_lhs(x, w):
    hi, mid, lo = _split3(x)
    d = lambda p: jnp.dot(p, w, preferred_element_type=F32)
    return (d(lo) + d(mid)) + d(hi)


def _cumsum_rows(x, period):
    row = lax.broadcasted_iota(jnp.int32, x.shape, 0) & (period - 1)
    s = 1
    while s < period:
        x = x + jnp.where(row >= s, pltpu.roll(x, s, axis=0), 0.0)
        s *= 2
    return x


def _rms(x, w):
    return x * lax.rsqrt(jnp.mean(x * x, axis=-1, keepdims=True) + RMS_EPS) * w


def _ada_kernel(c_ref, w_ref, b_ref, o_ref):
    c = c_ref[...]
    a = (c * _sigmoid(c)).astype(BF16)
    o_ref[...] = jnp.dot(a, w_ref[...].astype(BF16), preferred_element_type=F32) + b_ref[...]


def _ada(c, w, b, tn=1536):
    m, d = c.shape
    n = w.shape[1]
    return pl.pallas_call(
        _ada_kernel,
        grid=(n // tn,),
        in_specs=[pl.BlockSpec((m, d), lambda j: (0, 0)),
                  pl.BlockSpec((d, tn), lambda j: (0, j)),
                  pl.BlockSpec((1, tn), lambda j: (0, j))],
        out_specs=pl.BlockSpec((m, tn), lambda j: (0, j)),
        out_shape=jax.ShapeDtypeStruct((m, n), F32),
        compiler_params=_params("arbitrary"),
        name="ada",
    )(c, w, b)


def _inproj_kernel(x_ref, mod_ref, n1_ref, lbraw_ref, bf_ref, wa_ref, wff_ref, wmg_ref,
                   q_ref, g_ref, k_ref, v_ref, gate_ref, fq_ref, fk_ref, fv_ref,
                   lf8_ref, lf128_ref, mg_ref, *, fox_transposed):
    g_, r_, d = x_ref.shape
    m = g_ * r_
    x = x_ref[...]
    mod = mod_ref[...]
    sh1 = mod[:, :, 0:d]
    sc1 = mod[:, :, d:2 * d]
    h = _rms(x, n1_ref[...] * (1.0 + sc1)) + sh1
    hb = h.reshape(m, d).astype(BF16)

    raw = lbraw_ref[...]
    e = jnp.exp(raw - jnp.max(raw, axis=0, keepdims=True))
    lb = e[0:1, :] / jnp.sum(e, axis=0, keepdims=True)

    w = HG_WIDTH

    def proj(c):
        return lax.dot_general(hb, wa_ref[c * w:(c + 1) * w, :], _NT, preferred_element_type=F32)

    def proj_t(c):
        return lax.dot_general(wa_ref[c * w:(c + 1) * w, :], hb, _NT, preferred_element_type=F32)

    def merge_gate(c):
        mgc = lax.dot_general(hb, wmg_ref[c * w:(c + 1) * w, :], _NT, preferred_element_type=F32)
        mg_ref[:, c * w:(c + 1) * w] = _sigmoid(mgc).astype(BF16)

    def fox(c, ref, scale=None):
        y = proj_t(c) if fox_transposed else proj(c)
        y = y if scale is None else y * scale
        if fox_transposed:
            ref[0] = y.astype(ref.dtype)
        else:
            ref[...] = y.astype(ref.dtype)

    assert wmg_ref.shape[0] == 4 * w
    merge_gate(0)
    q_ref[...] = proj(0) * (HG_DK ** -0.5)
    merge_gate(1)
    v_ref[...] = proj(2).astype(BF16)
    merge_gate(2)
    fox(4, fq_ref, (LOG2E if fox_transposed else 1.0) * FOX_DH ** -0.5)
    merge_gate(3)
    fox(5, fk_ref)
    half_t = 0.5 * jnp.tanh(0.5 * proj(1))
    g_ref[...] = jnp.log(lb + (1.0 - lb) * (0.5 + half_t))
    k_ref[...] = (1.0 - lb) * (0.5 - half_t)
    fox(6, fv_ref)
    hg = proj(3)
    gate_ref[...] = (hg * _sigmoid(hg)).astype(BF16)
    z = lax.dot_general(hb, wff_ref[...], _NT, preferred_element_type=F32) + bf_ref[...]
    lf = jnp.minimum(z, 0.0) - jnp.log1p(jnp.exp(-jnp.abs(z)))
    lf128_ref[...] = lf
    lf8_ref[...] = lf[:, 0:FOX_HEADS]


def _inproj(x3, mod3, n1, lbraw, bf128, wa, wff, wmg, groups, rows, fox_transposed):
    assert groups == 1 or not fox_transposed
    nb, r_all, d = x3.shape
    nt = r_all // rows
    steps = (nb // groups) * nt
    t = nb * r_all
    m = groups * rows
    xmap = lambda s: (s // nt, s % nt, 0)
    mmap = lambda s: (s // nt, 0, 0)
    omap = lambda s: (s, 0)

    def out(width, dtype):
        return jax.ShapeDtypeStruct((t, width), dtype), pl.BlockSpec((m, width), omap)

    def fox(dtype):
        if not fox_transposed:
            return out(FOX_WIDTH, dtype)
        return (jax.ShapeDtypeStruct((nb, FOX_WIDTH, r_all), dtype),
                pl.BlockSpec((1, FOX_WIDTH, m), lambda s: (s // nt, 0, s % nt)))

    outs = [out(HG_WIDTH, F32), out(HG_WIDTH, F32), out(HG_WIDTH, F32), out(HG_WIDTH, BF16),
            out(HG_WIDTH, BF16), fox(BF16), fox(F32), fox(F32),
            out(FOX_HEADS, F32), out(LANES, F32), out(wmg.shape[0], BF16)]
    return pl.pallas_call(
        functools.partial(_inproj_kernel, fox_transposed=fox_transposed),
        grid=(steps,),
        in_specs=[pl.BlockSpec((groups, rows, d), xmap),
                  pl.BlockSpec((groups, 1, mod3.shape[2]), mmap),
                  _resident(n1.shape), _resident(lbraw.shape), _resident(bf128.shape),
                  _resident(wa.shape), _resident(wff.shape), _resident(wmg.shape)],
        out_specs=[o[1] for o in outs],
        out_shape=[o[0] for o in outs],
        compiler_params=_params("arbitrary"),
        name="inproj",
    )(x3, mod3, n1, lbraw, bf128, wa, wff, wmg)


def _decay_kernel(lf_ref, place_ref, bk_ref, bqt_ref):
    seq = lf_ref.shape[1]
    lane = lax.broadcasted_iota(jnp.int32, (LANES, LANES), 1)
    slot = lane & (BIAS_SLOTS - 1)
    used = lane < FOX_HEADS * BIAS_SLOTS
    ones_k = jnp.where(jnp.logical_and(used, slot < N_PIECES), 1.0, 0.0)
    ones_q = jnp.where(jnp.logical_and(used, jnp.logical_and(slot >= N_PIECES, slot < 2 * N_PIECES)), 1.0, 0.0)
    carry = jnp.zeros((1, LANES), F32)
    for j in range(seq // LANES):
        sl = slice(j * LANES, (j + 1) * LANES)
        cs = _cumsum_rows(lf_ref[0, sl, :], LANES) + carry
        carry = cs[LANES - 1:LANES, :]
        pieces = _split3(cs * LOG2E)

        def place(base):
            return sum(jnp.dot(pieces[j], place_ref[base + j], preferred_element_type=F32)
                       for j in range(N_PIECES))

        bk_ref[0, sl, :] = (ones_k - place(N_PIECES)).astype(BF16)
        bqt_ref[0, :, sl] = (ones_q + place(0)).T.astype(BF16)


def _decay(lf3, place):
    b, seq, _ = lf3.shape
    return pl.pallas_call(
        _decay_kernel,
        grid=(b,),
        in_specs=[pl.BlockSpec((1, seq, LANES), lambda i: (i, 0, 0)), _resident(place.shape)],
        out_specs=[pl.BlockSpec((1, seq, LANES), lambda i: (i, 0, 0)),
                   pl.BlockSpec((1, LANES, seq), lambda i: (i, 0, 0))],
        out_shape=[jax.ShapeDtypeStruct((b, seq, LANES), BF16),
                   jax.ShapeDtypeStruct((b, LANES, seq), BF16)],
        compiler_params=_params("arbitrary"),
        name="decay",
    )(lf3, place)


def _hgrn_decays(q, k, g, chunk):
    rows = q.shape[0]
    n_chunks = rows // chunk
    b = _cumsum_rows(g, chunk)
    qt = (q * jnp.exp(b)).astype(BF16)
    kt = (k * jnp.exp(-b)).astype(BF16)
    b3 = b.reshape(n_chunks, chunk, HG_DK)
    bl3 = b3[:, chunk - 1:chunk, :]
    kd = (k.reshape(n_chunks, chunk, HG_DK) * jnp.exp(bl3 - b3)).reshape(rows, HG_DK).astype(BF16)
    return qt, kt, kd, jnp.exp(bl3)


def _hgrn_products(qt, kt, kd, v, chunk):
    rows = qt.shape[0]
    n_chunks = rows // chunk
    shift = chunk.bit_length() - 1
    a = lax.dot_general(qt, kt, _NT, preferred_element_type=F32)
    v_t = v.astype(F32).T.astype(BF16)
    col_chunk = lax.broadcasted_iota(jnp.int32, v_t.shape, 1) >> shift
    v_blocks = jnp.concatenate([jnp.where(col_chunk == c, v_t, jnp.zeros_like(v_t)) for c in range(n_chunks)],
                               axis=0)
    incr = jnp.dot(v_blocks, kd, preferred_element_type=F32)
    ri = lax.broadcasted_iota(jnp.int32, (rows, rows), 0)
    ci = lax.broadcasted_iota(jnp.int32, (rows, rows), 1)
    keep = (ri - ci).astype(jnp.uint32) <= (ri & (chunk - 1)).astype(jnp.uint32)
    o_intra = jnp.dot(jnp.where(keep, a, 0.0).astype(BF16), v, preferred_element_type=F32)
    return o_intra, incr


def _hgrn_blocks(blocks, chunk, state_in, state_out):
    n_chunks = LANES // chunk
    decays = [_hgrn_decays(q, k, g, chunk) for (q, k, g, _) in blocks]
    products = [_hgrn_products(qt, kt, kd, blk[3], chunk) for (qt, kt, kd, _), blk in zip(decays, blocks)]
    entering = []
    st = None
    for n in range(len(blocks) * n_chunks):
        j, c = divmod(n, n_chunks)
        st = state_in(n, st)
        entering.append(st.astype(BF16))
        st = st * decays[j][3][c] + products[j][1][c * HG_DV:(c + 1) * HG_DV, :]
        state_out(n, st)
    outs = []
    for j, ((qt, _, _, _), (o_intra, _)) in enumerate(zip(decays, products)):
        o_inter = [lax.dot_general(qt[c * chunk:(c + 1) * chunk, :], entering[j * n_chunks + c], _NT,
                                   preferred_element_type=F32) for c in range(n_chunks)]
        outs.append(o_intra + jnp.concatenate(o_inter, axis=0))
    return outs


HGRN_GROUP = 8


def _hgrn_prompt_kernel(q_ref, k_ref, g_ref, v_ref, o_ref, s_ref, *, chunk):
    seq = q_ref.shape[0]
    group_rows = HGRN_GROUP * LANES

    def body(i, st0):
        base = i * group_rows
        row_slices = [pl.ds(base + j * LANES, LANES) for j in range(HGRN_GROUP)]
        blocks = [(q_ref[rs, :], k_ref[rs, :], g_ref[rs, :], v_ref[rs, :]) for rs in row_slices]
        last = []
        outs = _hgrn_blocks(blocks, chunk, lambda n, prev: st0 if prev is None else prev,
                            lambda n, st: last.append(st))
        for rs, o in zip(row_slices, outs):
            o_ref[rs, :] = o
        return last[-1]

    st = jnp.zeros((HG_DV, HG_DK), F32)
    for i in range(seq // group_rows):
        st = body(i, st)
    s_ref[0, 0] = st.T


def _hgrn_prompt(q, k, g, v, batch, seq, chunk):
    spec = pl.BlockSpec((seq, HG_DK), lambda b, h: (b, h))
    return pl.pallas_call(
        functools.partial(_hgrn_prompt_kernel, chunk=chunk),
        grid=(batch, HG_HEADS),
        in_specs=[spec, spec, spec, spec],
        out_specs=[spec, pl.BlockSpec((1, 1, HG_DK, HG_DV), lambda b, h: (b, h, 0, 0))],
        out_shape=[jax.ShapeDtypeStruct((batch * seq, HG_WIDTH), F32),
                   jax.ShapeDtypeStruct((batch, HG_HEADS, HG_DK, HG_DV), F32)],
        compiler_params=_params("arbitrary", "arbitrary"),
        name="hgrn_prompt",
    )(q, k, g, v)


def _hgrn_sample_kernel(q_ref, k_ref, g_ref, v_ref, s0_ref, o_ref, s_ref, *, chunk):
    per = LANES // chunk

    def state_in(n, prev):
        h, c = divmod(n, per)
        return s0_ref[c, h].T

    def state_out(n, st):
        h, c = divmod(n, per)
        s_ref[c, h] = st.T

    heads = [slice(h * HG_DK, (h + 1) * HG_DK) for h in range(HG_HEADS)]
    outs = _hgrn_blocks([(q_ref[:, hs], k_ref[:, hs], g_ref[:, hs], v_ref[:, hs]) for hs in heads],
                        chunk, state_in, state_out)
    for hs, o in zip(heads, outs):
        o_ref[:, hs] = o


def _hgrn_sample(q, k, g, v, s0, chunk):
    t = q.shape[0]
    per = LANES // chunk
    spec = pl.BlockSpec((LANES, HG_WIDTH), lambda i: (i, 0))
    sspec = pl.BlockSpec((per, HG_HEADS, HG_DK, HG_DV), lambda i: (i, 0, 0, 0))
    return pl.pallas_call(
        functools.partial(_hgrn_sample_kernel, chunk=chunk),
        grid=(t // LANES,),
        in_specs=[spec, spec, spec, spec, sspec],
        out_specs=[spec, sspec],
        out_shape=[jax.ShapeDtypeStruct((t, HG_WIDTH), F32),
                   jax.ShapeDtypeStruct(s0.shape, F32)],
        compiler_params=_params("arbitrary"),
        name="hgrn_sample",
    )(q, k, g, v, s0)


ONES_ROWS = 16
PAGE_PREFETCH_DEPTH = 3


def _prompt_attention(qt_ref, kt_ref, vt_ref, bk_ref, bqt_ref, o_ref,
                      kn_ref, va_ref, rhs_ref, mask_ref, s_ref, p_ref, *, pair, tq, tk, units):
    seq = kt_ref.shape[1]
    n_heads = LANES // FOX_DH
    n_units = len(units)
    limit = [i * tq - j * tk for i, j in units]
    masked_limits = sorted({v for v in limit if v < tk - 1})
    assert len(masked_limits) <= mask_ref.shape[0]
    is_first = [j == 0 for _, j in units]
    is_last = [u + 1 == n_units or units[u + 1][0] != units[u][0] for u in range(n_units)]

    def prepare():
        for c in range(seq // LANES):
            cs = slice(c * LANES, (c + 1) * LANES)
            kn_ref[cs, 0:LANES] = kt_ref[:, cs].T.astype(BF16)
        kn_ref[:, LANES:2 * LANES] = bk_ref[...]
        for c in range(seq // tk):
            for e in range(n_heads):
                va_ref[c, e, 0:FOX_DH, :] = vt_ref[e * FOX_DH:(e + 1) * FOX_DH, c * tk:(c + 1) * tk].astype(BF16)
                va_ref[c, e, FOX_DH:FOX_DH + ONES_ROWS, :] = jnp.ones((ONES_ROWS, tk), BF16)
        row = lax.broadcasted_iota(jnp.int32, (LANES, tq), 0)
        for i in range(seq // tq):
            qs = slice(i * tq, (i + 1) * tq)
            qt = qt_ref[:, qs]
            bqt = bqt_ref[:, qs]
            for e in range(n_heads):
                rhs_ref[i, e, 0:LANES, :] = jnp.where(row // FOX_DH == e, qt, jnp.zeros_like(qt))
                rhs_ref[i, e, LANES:2 * LANES, :] = jnp.where(row // BIAS_SLOTS == pair * n_heads + e, bqt,
                                                              jnp.zeros_like(bqt))
        key_minus_query = (lax.broadcasted_iota(jnp.int32, (tk, tq), 0)
                           - lax.broadcasted_iota(jnp.int32, (tk, tq), 1))
        for n, v in enumerate(masked_limits):
            mask_ref[n] = jnp.where(key_minus_query <= v, 0.0, NEG_INF)
        scores(0)
        return None, None, None, None

    n_keys = [min(tk, limit[u] + tq) for u in range(n_units)]

    def scores(u):
        qi, kj = units[u]
        lhs = kn_ref[kj * tk:kj * tk + n_keys[u], :]
        for e in range(n_heads):
            s_ref[u % 2, e, 0:n_keys[u], :] = jnp.dot(lhs, rhs_ref[qi, e], preferred_element_type=F32)

    def numerators(u, ms):
        new_ms, alphas = [], []
        for e in range(n_heads):
            s = s_ref[u % 2, e, 0:n_keys[u], :]
            if limit[u] < tk - 1:
                s = s + mask_ref[masked_limits.index(limit[u]), 0:n_keys[u], :]
            m_new = jnp.max(s, axis=0, keepdims=True)
            if not is_first[u]:
                m_new = jnp.maximum(ms[e], m_new)
                alphas.append(jnp.exp2(ms[e] - m_new))
            p_ref[u % 2, e, 0:n_keys[u], :] = jnp.exp2(s - m_new).astype(BF16)
            new_ms.append(m_new)
        return new_ms, alphas

    def values(u, alphas, ls, accs):
        qi, kj = units[u]
        new_ls, new_accs = [], []
        for e in range(n_heads):
            pv = jnp.dot(va_ref[kj, e, :, 0:n_keys[u]], p_ref[u % 2, e, 0:n_keys[u], :],
                         preferred_element_type=F32)
            l_e, acc_e = pv[FOX_DH:FOX_DH + 1, :], pv[0:FOX_DH, :]
            if not is_first[u]:
                l_e, acc_e = alphas[e] * ls[e] + l_e, alphas[e] * accs[e] + acc_e
            new_ls.append(l_e)
            new_accs.append(acc_e)
        if is_last[u]:
            o_t = jnp.concatenate([new_accs[e] / new_ls[e] for e in range(n_heads)], axis=0)
            o_ref[qi * tq:(qi + 1) * tq, :] = o_t.T.astype(BF16)
        return new_ls, new_accs

    def run(lo, hi, state):
        ms, alphas, ls, accs = state
        for u in range(lo, hi):
            if u > 0:
                ls, accs = values(u - 1, alphas, ls, accs)
            if u + 1 < n_units:
                scores(u + 1)
            ms, alphas = numerators(u, ms)
        return ms, alphas, ls, accs

    def finish(state):
        _, alphas, ls, accs = state
        values(n_units - 1, alphas, ls, accs)

    return prepare, run, finish


def _sample_attend(q, k_new, v_new, lf_new, sfx_ref, k_refs, v_refs, lf_refs):
    n_pages = len(k_refs)
    steps = q.shape[0]
    rows = FOX_HEADS * steps
    pad = PAGE_SIZE - steps

    def per_head_rows(x8):
        return jnp.broadcast_to(x8[:, None, :], (FOX_HEADS, steps, x8.shape[1])).reshape(rows, x8.shape[1])

    q = q.astype(F32)
    q_rows = jnp.concatenate([q] * FOX_HEADS, axis=0)
    row_head = lax.broadcasted_iota(jnp.int32, (rows, FOX_WIDTH), 0) // steps
    lane_head = lax.broadcasted_iota(jnp.int32, (rows, FOX_WIDTH), 1) // FOX_DH
    head_mask = row_head == lane_head
    qbd = jnp.where(head_mask, q_rows, 0.0).astype(BF16)

    pn = _cumsum_rows(lf_new, steps)
    pn_rows = jnp.concatenate([pn] * FOX_HEADS, axis=0)
    r_h = lax.broadcasted_iota(jnp.int32, (rows, LANES), 0) // steps
    r_i = lax.broadcasted_iota(jnp.int32, (rows, LANES), 0) % steps
    c_l = lax.broadcasted_iota(jnp.int32, (rows, LANES), 1)
    pcol = jnp.sum(jnp.where(c_l == r_h, pn_rows, 0.0), axis=1, keepdims=True)
    pn_t = jnp.concatenate([pn, jnp.zeros((pad, LANES), F32)], axis=0).T[0:FOX_HEADS, :]

    lf_all = jnp.concatenate([lf_refs[j][...] for j in range(n_pages)], axis=0)
    sfx = _dot_f32_lhs(lf_all, sfx_ref[...])
    carry = jnp.zeros((FOX_HEADS, PAGE_SIZE), F32)
    page_bias = [None] * n_pages
    for j in reversed(range(n_pages)):
        blk = sfx[j * FOX_HEADS:(j + 1) * FOX_HEADS, :]
        page_bias[j] = blk[:, 0:PAGE_SIZE] + carry
        carry = carry + blk[:, PAGE_SIZE:2 * PAGE_SIZE]

    s_tiles = []
    for j in range(n_pages):
        s = jnp.dot(qbd, k_refs[j][...].astype(BF16), preferred_element_type=F32)
        s_tiles.append(s + (per_head_rows(page_bias[j]) + pcol))
    kn = jnp.concatenate([k_new, jnp.zeros((pad, FOX_WIDTH), F32)], axis=0).astype(BF16)
    s_new = lax.dot_general(qbd, kn, _NT, preferred_element_type=F32) + (pcol - per_head_rows(pn_t))
    s_tiles.append(jnp.where(c_l <= r_i, s_new, NEG_INF))

    m_el = s_tiles[0]
    for s in s_tiles[1:]:
        m_el = jnp.maximum(m_el, s)
    m = jnp.max(m_el, axis=1, keepdims=True)
    p_tiles = [jnp.exp(s - m) for s in s_tiles]
    l_el = p_tiles[0]
    for p in p_tiles[1:]:
        l_el = l_el + p
    l = jnp.sum(l_el, axis=1, keepdims=True)

    vn = jnp.concatenate([v_new, jnp.zeros((pad, FOX_WIDTH), F32)], axis=0).astype(BF16)
    o = jnp.dot(p_tiles[n_pages].astype(BF16), vn, preferred_element_type=F32)
    for j in range(n_pages):
        o = o + lax.dot_general(p_tiles[j].astype(BF16), v_refs[j][...].astype(BF16), _NT,
                                preferred_element_type=F32)
    o = jnp.where(head_mask, o / l, 0.0)
    return jnp.sum(o.reshape(FOX_HEADS, steps, FOX_WIDTH), axis=0)


def _fox_kernel(pt_ref,
                qt_ref, kt_ref, vt_ref, bk_ref, bqt_ref,
                sq_ref, sk_ref, sv_ref, slf_ref, sfx_ref, ck_hbm, cv_hbm, cl_hbm,
                o_ref, so_ref,
                kn_ref, va_ref, rhs_ref, mask_ref, s_ref, p_ref, k_buf, v_buf, lf_buf, sems,
                *, tq, tk, units, n_pages):
    step = pl.program_id(0) * pl.num_programs(1) + pl.program_id(1)
    n_steps = pl.num_programs(0) * pl.num_programs(1)
    per_step = sq_ref.shape[0]
    depth = PAGE_PREFETCH_DEPTH
    assert 0 < depth < per_step
    last_seq = n_steps * per_step - 1

    def page_copies(seq, slot):
        copies = []
        for j in range(n_pages):
            page = pt_ref[seq * n_pages + j]
            copies.append(pltpu.make_async_copy(ck_hbm.at[page], k_buf.at[slot, j], sems.at[0, slot]))
            copies.append(pltpu.make_async_copy(cv_hbm.at[page], v_buf.at[slot, j], sems.at[0, slot]))
            copies.append(pltpu.make_async_copy(cl_hbm.at[page], lf_buf.at[slot, j], sems.at[1, slot]))
        return copies

    @pl.when(step == 0)
    def _():
        for d in range(depth):
            for c in page_copies(d, d):
                c.start()

    prepare, run, finish = _prompt_attention(
        qt_ref, kt_ref, vt_ref, bk_ref, bqt_ref, o_ref,
        kn_ref, va_ref, rhs_ref, mask_ref, s_ref, p_ref,
        pair=pl.program_id(1), tq=tq, tk=tk, units=units)
    state = prepare()
    bounds = [(j * len(units)) // per_step for j in range(per_step + 1)]
    for j in range(per_step):
        seq = step * per_step + j
        slot = j
        ahead = jnp.minimum(seq + depth, last_seq)
        for c in page_copies(ahead, (j + depth) % per_step):
            c.start()
        for c in page_copies(seq, slot):
            c.wait()
        so_ref[j] = _sample_attend(
            sq_ref[j], sk_ref[j], sv_ref[j], slf_ref[j], sfx_ref,
            [k_buf.at[slot, n] for n in range(n_pages)], [v_buf.at[slot, n] for n in range(n_pages)],
            [lf_buf.at[slot, n] for n in range(n_pages)]).astype(BF16)
        state = run(bounds[j], bounds[j + 1], state)
    finish(state)

    @pl.when(step == n_steps - 1)
    def _():
        for d in range(depth):
            for c in page_copies(last_seq, d):
                c.wait()


def _fox(fqt, fkt, fvt, bk, bqt, page_table, sq3, sk3, sv3, slf3, sfx_mat, cache_kt, cache_vt, cache_lft,
         tq, tk):
    batch, _, seq = fqt.shape
    nseq, n_pages = page_table.shape
    steps = sq3.shape[1]
    assert tk % tq == 0 and seq % tk == 0
    nq = seq // tq
    pairs = FOX_WIDTH // LANES
    n_heads = LANES // FOX_DH
    per_step = nseq // (batch * pairs)
    assert per_step * batch * pairs == nseq and per_step > PAGE_PREFETCH_DEPTH
    units = tuple((i, j) for i in range(nq) for j in range((i * tq) // tk + 1))
    n_masks = len({i * tq - j * tk for i, j in units if i * tq - j * tk < tk - 1})
    head_pair = lambda b, p, *_: (b, p, 0)
    per_batch = lambda b, p, *_: (b, 0, 0)
    sample = lambda w: pl.BlockSpec((per_step, steps, w), lambda b, p, *_: (b * pairs + p, 0, 0))
    in_hbm = pl.BlockSpec(memory_space=pl.ANY)
    grid_spec = pltpu.PrefetchScalarGridSpec(
        num_scalar_prefetch=1,
        grid=(batch, pairs),
        in_specs=[pl.BlockSpec((None, LANES, seq), head_pair),
                  pl.BlockSpec((None, LANES, seq), head_pair),
                  pl.BlockSpec((None, LANES, seq), head_pair),
                  pl.BlockSpec((None, seq, LANES), per_batch),
                  pl.BlockSpec((None, LANES, seq), per_batch),
                  sample(FOX_WIDTH), sample(FOX_WIDTH), sample(FOX_WIDTH), sample(LANES),
                  pl.BlockSpec(sfx_mat.shape, lambda b, p, *_: (0, 0)), in_hbm, in_hbm, in_hbm],
        out_specs=[pl.BlockSpec((seq, LANES), lambda b, p, *_: (b, p)), sample(FOX_WIDTH)],
        scratch_shapes=[pltpu.VMEM((seq, 2 * LANES), BF16),
                        pltpu.VMEM((seq // tk, n_heads, FOX_DH + ONES_ROWS, tk), BF16),
                        pltpu.VMEM((nq, n_heads, 2 * LANES, tq), BF16),
                        pltpu.VMEM((n_masks, tk, tq), F32),
                        pltpu.VMEM((2, n_heads, tk, tq), F32),
                        pltpu.VMEM((2, n_heads, tk, tq), BF16),
                        pltpu.VMEM((per_step, n_pages, FOX_WIDTH, PAGE_SIZE), cache_kt.dtype),
                        pltpu.VMEM((per_step, n_pages, FOX_WIDTH, PAGE_SIZE), cache_vt.dtype),
                        pltpu.VMEM((per_step, n_pages, FOX_HEADS, PAGE_SIZE), cache_lft.dtype),
                        pltpu.SemaphoreType.DMA((2, per_step))],
    )
    return pl.pallas_call(
        functools.partial(_fox_kernel, tq=tq, tk=tk, units=units, n_pages=n_pages),
        grid_spec=grid_spec,
        out_shape=[jax.ShapeDtypeStruct((batch * seq, FOX_WIDTH), BF16),
                   jax.ShapeDtypeStruct((nseq, steps, FOX_WIDTH), BF16)],
        compiler_params=_params("arbitrary", "arbitrary"),
        name="fox",
    )(page_table.reshape(-1), fqt, fkt, fvt, bk, bqt, sq3, sk3, sv3, slf3, sfx_mat,
      cache_kt, cache_vt, cache_lft)


def _post_kernel(x_ref, mod_ref, oh_ref, gate_ref, of_ref, mg_ref, onorm_ref, n1post_ref, n2pre_ref,
                 n2post_ref, wbh_ref, wbf_ref, wout_ref, wup_ref, wdn_ref, y_ref, *, ff_chunk):
    g_, r_, d = x_ref.shape
    m = g_ * r_
    mod = mod_ref[...]
    gt1 = mod[:, :, 2 * d:3 * d]
    sh2 = mod[:, :, 3 * d:4 * d]
    sc2 = mod[:, :, 4 * d:5 * d]
    gt2 = mod[:, :, 5 * d:6 * d]

    oh = oh_ref[...]
    parts = [_rms(oh[:, h * HG_DV:(h + 1) * HG_DV], onorm_ref[...]) for h in range(HG_HEADS)]
    ohn = (jnp.concatenate(parts, axis=1) * gate_ref[...].astype(F32)).astype(BF16)
    br_h = jnp.dot(ohn, wbh_ref[...], preferred_element_type=F32)
    br_f = jnp.dot(of_ref[...], wbf_ref[...], preferred_element_type=F32)
    mg = mg_ref[...].astype(F32)
    z = (mg[:, 0:d] * br_h + mg[:, d:2 * d] * br_f).astype(BF16)
    y = jnp.dot(z, wout_ref[...], preferred_element_type=F32).reshape(g_, r_, d)
    x1 = x_ref[...] + _rms(y, gt1 * n1post_ref[...])
    h2 = (_rms(x1, n2pre_ref[...] * (1.0 + sc2)) + sh2).reshape(m, d).astype(BF16)
    u = jnp.zeros((m, d), F32)
    for c in range(wup_ref.shape[1] // ff_chunk):
        cs = slice(c * ff_chunk, (c + 1) * ff_chunk)
        a = jnp.maximum(jnp.dot(h2, wup_ref[:, cs], preferred_element_type=F32), 0.0)
        u = u + jnp.dot((a * a).astype(BF16), wdn_ref[cs, :], preferred_element_type=F32)
    y_ref[...] = x1 + _rms(u.reshape(g_, r_, d), gt2 * n2post_ref[...])


def _post(x3, mod3, oh, gate, of, mg, onorm, n1post, n2pre, n2post, wbh, wbf, wout, wup, wdn,
          groups, rows, ff_chunk=2048):
    nb, r_all, d = x3.shape
    nt = r_all // rows
    steps = (nb // groups) * nt
    m = groups * rows
    xmap = lambda s: (s // nt, s % nt, 0)
    mmap = lambda s: (s // nt, 0, 0)
    tmap = lambda s: (s, 0)
    tok = lambda a: pl.BlockSpec((m, a.shape[1]), tmap)
    consts = [onorm, n1post, n2pre, n2post, wbh, wbf, wout, wup, wdn]
    return pl.pallas_call(
        functools.partial(_post_kernel, ff_chunk=ff_chunk),
        grid=(steps,),
        in_specs=[pl.BlockSpec((groups, rows, d), xmap),
                  pl.BlockSpec((groups, 1, mod3.shape[2]), mmap),
                  tok(oh), tok(gate), tok(of), tok(mg)] + [_resident(c.shape) for c in consts],
        out_specs=pl.BlockSpec((groups, rows, d), xmap),
        out_shape=jax.ShapeDtypeStruct(x3.shape, F32),
        compiler_params=_params("arbitrary"),
        name="post",
    )(x3, mod3, oh, gate, of, mg, *consts)


def _bias_place_matrices():
    shape = (2 * N_PIECES, LANES, LANES)
    s = lax.broadcasted_iota(jnp.int32, shape, 0)
    r = lax.broadcasted_iota(jnp.int32, shape, 1)
    c = lax.broadcasted_iota(jnp.int32, shape, 2)
    return jnp.logical_and(r < FOX_HEADS, c == BIAS_SLOTS * r + s).astype(BF16)


def _page_suffix_matrix():
    r = lax.broadcasted_iota(jnp.int32, (PAGE_SIZE, 2 * PAGE_SIZE), 0)
    c = lax.broadcasted_iota(jnp.int32, (PAGE_SIZE, 2 * PAGE_SIZE), 1)
    return jnp.logical_or(r > c, c >= PAGE_SIZE).astype(BF16)


def kernel(x_prompt, x_sample, c_prompt, c_sample, cache_k, cache_v, cache_logf, state_hgrn, page_table,
           ada_w, ada_b, norm_mix_pre, norm_mix_post, norm_mlp_pre, norm_mlp_post, w_in,
           hgrn_lower_bounds, hgrn_onorm, fox_b_f, w_br_h, w_br_f, w_out, w_mlp_up, w_mlp_down):
    batch, seq, d = x_prompt.shape
    nseq, steps, _ = x_sample.shape
    layer = 0
    n_phys = cache_k.shape[1]

    w_t = jnp.transpose(w_in[layer]).astype(BF16)
    n_a = 4 * HG_WIDTH + 3 * FOX_WIDTH
    wa = w_t[:n_a]
    wff = jnp.pad(w_t[n_a:n_a + FOX_HEADS], ((0, LANES - FOX_HEADS), (0, 0)))
    wmg = w_t[n_a + FOX_HEADS:]
    bf128 = jnp.pad(fox_b_f[layer], (0, LANES - FOX_HEADS)).reshape(1, LANES)
    lbraw = hgrn_lower_bounds
    vec3 = lambda v: v.reshape(1, 1, -1)
    wbh, wbf = w_br_h[layer].astype(BF16), w_br_f[layer].astype(BF16)
    wout = w_out[layer].astype(BF16)
    wup, wdn = w_mlp_up[layer].astype(BF16), w_mlp_down[layer].astype(BF16)
    onorm = hgrn_onorm[layer].reshape(1, HG_DV)

    n_c = batch + nseq
    c_pad = -n_c % (2 * SUBLANES)
    c_all = jnp.concatenate([c_prompt, c_sample, jnp.zeros((c_pad, d), F32)], axis=0)
    mod = _ada(c_all, ada_w[layer], ada_b[layer].reshape(1, -1))
    mod_p = mod[:batch].reshape(batch, 1, -1)
    mod_s = mod[batch:n_c].reshape(nseq, 1, -1)

    tm = TOKEN_TILE
    sgroups = tm // steps
    proj_args = (vec3(norm_mix_pre[layer]), lbraw, bf128, wa, wff, wmg)
    (q_p, g_p, k_p, v_p, gate_p, fq_p, fk_p, fv_p, lf8_p, lf128_p, mg_p) = _inproj(
        x_prompt, mod_p, *proj_args, groups=1, rows=tm, fox_transposed=True)
    (q_s, g_s, k_s, v_s, gate_s, fq_s, fk_s, fv_s, lf8_s, lf128_s, mg_s) = _inproj(
        x_sample, mod_s, *proj_args, groups=sgroups, rows=steps, fox_transposed=False)

    chunk_p = HG_CHUNK if seq % HG_CHUNK == 0 else seq
    oh_p, s_p = _hgrn_prompt(q_p, k_p, g_p, v_p, batch, seq, chunk_p)
    oh_s, s_s = _hgrn_sample(q_s, k_s, g_s, v_s, state_hgrn[layer], steps)

    tq, tk = FOX_Q_TILE, FOX_K_TILE
    bias_k, bias_qt = _decay(lf128_p.reshape(batch, seq, LANES), _bias_place_matrices())
    ckt = jnp.transpose(cache_k[layer], (0, 2, 3, 1)).reshape(n_phys, FOX_WIDTH, PAGE_SIZE)
    cvt = jnp.transpose(cache_v[layer], (0, 2, 3, 1)).reshape(n_phys, FOX_WIDTH, PAGE_SIZE)
    clt = jnp.transpose(cache_logf[layer], (0, 2, 1))
    per_seq = lambda a: a.reshape(nseq, steps, a.shape[-1])
    of_p, of_s = _fox(fq_p, fk_p, fv_p, bias_k, bias_qt, page_table, per_seq(fq_s), per_seq(fk_s), per_seq(fv_s),
                      per_seq(lf128_s), _page_suffix_matrix(), ckt, cvt, clt, tq, tk)
    of_s = of_s.reshape(nseq * steps, FOX_WIDTH)

    post_args = (onorm, vec3(norm_mix_post[layer]), vec3(norm_mlp_pre[layer]), vec3(norm_mlp_post[layer]),
                 wbh, wbf, wout, wup, wdn)
    y_p = _post(x_prompt, mod_p, oh_p, gate_p, of_p, mg_p, *post_args, groups=1, rows=tm)
    y_s = _post(x_sample, mod_s, oh_s, gate_s, of_s, mg_s, *post_args, groups=sgroups, rows=steps)

    k_prompt = fk_p.reshape(1, batch, FOX_HEADS, FOX_DH, seq).transpose(0, 1, 4, 2, 3)
    v_prompt = fv_p.reshape(1, batch, FOX_HEADS, FOX_DH, seq).transpose(0, 1, 4, 2, 3)
    logf_prompt = lf8_p.reshape(1, batch, seq, FOX_HEADS)
    k_sample = fk_s.reshape(1, nseq, steps, FOX_HEADS, FOX_DH)
    v_sample = fv_s.reshape(1, nseq, steps, FOX_HEADS, FOX_DH)
    logf_sample = lf8_s.reshape(1, nseq, steps, FOX_HEADS)
    return (y_p, y_s, k_prompt, v_prompt, logf_prompt, s_p[None], k_sample, v_sample, logf_sample, s_s[None])
```
